```python
import math
import jax, jax.numpy as jnp
from jax import lax
import numpy as np

D_MODEL = 1024
BATCH = 8
SEQ = 4096
DEPTH = 1

CTX_LEN = 256
GRID_W = 64

RET_HEADS = 4
RET_DK = 128
RET_DV = 128
RET_QK = RET_HEADS * RET_DK
RET_V = RET_HEADS * RET_DV
DIFF_HEADS = 4
DIFF_HD = 64
DIFF_DV = 2 * DIFF_HD
DIFF_QK = DIFF_HEADS * 2 * DIFF_HD
DIFF_V = DIFF_HEADS * DIFF_DV
MIX_WIDTH = RET_V + DIFF_V
IN_COLS = 2 * RET_QK + 2 * RET_V + 2 * DIFF_QK + DIFF_V
D_FF = 2816
CONV_WIDTH = 3
CHUNK = 128
Q_BLOCK = 128
ROPE_BASE = 10000.0
EPS = 1e-6
GN_EPS = 1e-5
N_MOD = 6

kernel_name = "hybrid_retention_diffattn_dit_layer"


def rms_norm(x, g):
    xf = x.astype(jnp.float32)
    y = xf * lax.rsqrt(jnp.mean(xf * xf, axis=-1, keepdims=True) + EPS)
    return (y * g.astype(jnp.float32)).astype(x.dtype)


def grid_rope(rows, head_dim):
    n_freq = head_dim // 4
    inv = ROPE_BASE ** (-jnp.arange(n_freq, dtype=jnp.float32) / n_freq)
    pos = jnp.arange(rows * GRID_W)
    row = (pos // GRID_W).astype(jnp.float32)
    col = (pos % GRID_W).astype(jnp.float32)
    ang = jnp.concatenate([row[:, None] * inv, col[:, None] * inv], axis=-1)
    return jnp.cos(ang), jnp.sin(ang)


def apply_rope(x, cos, sin):
    half = x.shape[-1] // 2
    x1, x2 = x[..., :half], x[..., half:]
    cs = cos[None, :, None, :].astype(x.dtype)
    sn = sin[None, :, None, :].astype(x.dtype)
    return jnp.concatenate([x1 * cs - x2 * sn, x1 * sn + x2 * cs], axis=-1)


def retention_chunked(q, k, v, log_gamma, state0):
    B, H, L, dk = q.shape
    dv = v.shape[-1]
    n = L // CHUNK
    idx = jnp.arange(CHUNK, dtype=jnp.float32)
    rel = idx[:, None] - idx[None, :]
    intra = jnp.where(rel >= 0, jnp.exp(jnp.maximum(rel, 0.0) * log_gamma[:, None, None]), 0.0)
    q_dec = jnp.exp((idx + 1.0) * log_gamma[:, None])
    k_dec = jnp.exp((CHUNK - 1.0 - idx) * log_gamma[:, None])
    c_dec = jnp.exp(CHUNK * log_gamma)

    def to_chunks(t):
        return jnp.moveaxis(t.reshape(B, H, n, CHUNK, t.shape[-1]), 2, 0)

    def step(S, inp):
        qc, kc, vc = inp
        scores = jnp.einsum('bhid,bhjd->bhij', qc, kc) * intra
        o = (jnp.einsum('bhij,bhje->bhie', scores, vc)
             + jnp.einsum('bhid,bhde->bhie', qc * q_dec[..., None], S))
        S = S * c_dec[:, None, None] + jnp.einsum('bhjd,bhje->bhde', kc * k_dec[..., None], vc)
        return S, o

    S, o = lax.scan(step, state0, (to_chunks(q), to_chunks(k), to_chunks(v)))
    o = jnp.moveaxis(o, 0, 2).reshape(B, H, L, dv)
    return o, S


def head_group_norm(o, g):
    B, H, L, dv = o.shape
    mu = jnp.mean(o, axis=-1, keepdims=True)
    var = jnp.mean(jnp.square(o - mu), axis=-1, keepdims=True)
    on = (o - mu) * lax.rsqrt(var + GN_EPS)
    return on.transpose(0, 2, 1, 3).reshape(B, L, H * dv) * g.astype(jnp.float32)


def diff_attention(q, k, v, lam):
    B, H, _, Lq, d = q.shape
    dv = v.shape[-1]
    scale = d ** -0.5

    def attend(qb):
        s = jnp.einsum('bhmqd,bhmkd->bhmqk', qb, k).astype(jnp.float32) * scale
        p = jax.nn.softmax(s, axis=-1)
        a = p[:, :, 0] - lam * p[:, :, 1]
        return jnp.einsum('bhqk,bhke->bhqe', a.astype(v.dtype), v)

    if Lq <= Q_BLOCK:
        return attend(q)
    nb = Lq // Q_BLOCK
    qb = jnp.moveaxis(q.reshape(B, H, 2, nb, Q_BLOCK, d), 3, 0)
    o = lax.map(attend, qb)
    return jnp.moveaxis(o, 0, 2).reshape(B, H, Lq, dv)


def diff_subln(o, g, lambda_init):
    B, H, L, dv = o.shape
    of = o.astype(jnp.float32)
    n = of * lax.rsqrt(jnp.mean(of * of, axis=-1, keepdims=True) + EPS)
    n = n.transpose(0, 2, 1, 3).reshape(B, L, H * dv) * g.astype(jnp.float32) * (1.0 - lambda_init)
    return n.astype(o.dtype)


def split_heads(p, rope_r, rope_d):
    B, L, _ = p.shape
    cuts = np.cumsum([RET_QK, RET_QK, RET_V, RET_V, DIFF_QK, DIFF_QK]).tolist()
    rq, rk, rv, rg, dq, dk, dvv = jnp.split(p, cuts, axis=-1)
    rq = rq.reshape(B, L, RET_HEADS, RET_DK)
    rk = rk.reshape(B, L, RET_HEADS, RET_DK)
    dq = dq.reshape(B, L, 2 * DIFF_HEADS, DIFF_HD)
    dk = dk.reshape(B, L, 2 * DIFF_HEADS, DIFF_HD)
    if rope_r is not None:
        rq = apply_rope(rq, *rope_r)
        rk = apply_rope(rk, *rope_r)
        dq = apply_rope(dq, *rope_d)
        dk = apply_rope(dk, *rope_d)
    rq = rq.transpose(0, 2, 1, 3)
    rk = rk.transpose(0, 2, 1, 3) * (RET_DK ** -0.5)
    rv = rv.reshape(B, L, RET_HEADS, RET_DV).transpose(0, 2, 1, 3)
    dq = dq.reshape(B, L, DIFF_HEADS, 2, DIFF_HD).transpose(0, 2, 3, 1, 4)
    dk = dk.reshape(B, L, DIFF_HEADS, 2, DIFF_HD).transpose(0, 2, 3, 1, 4)
    dvv = dvv.reshape(B, L, DIFF_HEADS, DIFF_DV).transpose(0, 2, 1, 3)
    return (rq, rk, rv, rg), (dq, dk, dvv)


def token_mixers(h_lat, h_ctx, w_in, decay_logit, gn_g, lam_par, subln_g, w_out,
                 lambda_init, rope_r, rope_d, need_ctx):
    dtype = h_lat.dtype
    B = h_lat.shape[0]
    (rq_l, rk_l, rv_l, rg_l), (dq_l, dk_l, dv_l) = split_heads(h_lat @ w_in, rope_r, rope_d)
    (rq_c, rk_c, rv_c, rg_c), (dq_c, dk_c, dv_c) = split_heads(h_ctx @ w_in, None, None)

    log_gamma = jax.nn.log_sigmoid(decay_logit.astype(jnp.float32))
    ret_lat = 0.0
    ret_ctx = 0.0
    for d in range(2):
        def orient(t):
            return jnp.flip(t, axis=2) if d == 1 else t
        qc, kc, vc = [orient(t.astype(jnp.float32)) for t in (rq_c, rk_c, rv_c)]
        ql, kl, vl = [orient(t.astype(jnp.float32)) for t in (rq_l, rk_l, rv_l)]
        s0 = jnp.zeros((B, RET_HEADS, RET_DK, RET_DV), jnp.float32)
        o_c, s_c = retention_chunked(qc, kc, vc, log_gamma[d], s0)
        o_l, _ = retention_chunked(ql, kl, vl, log_gamma[d], s_c)
        ret_lat = ret_lat + head_group_norm(orient(o_l), gn_g[d])
        if need_ctx:
            ret_ctx = ret_ctx + head_group_norm(orient(o_c), gn_g[d])
    y_ret_lat = jax.nn.silu(rg_l) * ret_lat.astype(dtype)

    lp = lam_par.astype(jnp.float32)
    lam = jnp.exp(jnp.sum(lp[0] * lp[1])) - jnp.exp(jnp.sum(lp[2] * lp[3])) + lambda_init
    k_all = jnp.concatenate([dk_c, dk_l], axis=3)
    v_all = jnp.concatenate([dv_c, dv_l], axis=2)
    y_diff_lat = diff_subln(diff_attention(dq_l, k_all, v_all, lam), subln_g, lambda_init)

    out_lat = jnp.concatenate([y_ret_lat, y_diff_lat], axis=-1) @ w_out
    out_ctx = None
    if need_ctx:
        y_ret_ctx = jax.nn.silu(rg_c) * ret_ctx.astype(dtype)
        y_diff_ctx = diff_subln(diff_attention(dq_c, dk_c, dv_c, lam), subln_g, lambda_init)
        out_ctx = jnp.concatenate([y_ret_ctx, y_diff_ctx], axis=-1) @ w_out
    return out_lat, out_ctx


def conv_ffn(h, w_up, conv_w, conv_b, w_down):
    u = h @ w_up
    up = jnp.pad(u, ((0, 0), (1, 1), (0, 0)))
    u = up[:, :-2] * conv_w[0] + up[:, 1:-1] * conv_w[1] + up[:, 2:] * conv_w[2] + conv_b
    a, b = jnp.split(u, 2, axis=-1)
    return (jax.nn.silu(a) * b) @ w_down


def setup_inputs(seed: int = 0) -> dict:
    key = jax.random.key(seed)
    ks = jax.random.split(key, 20)
    f32 = jnp.float32
    D = D_MODEL
    a = 5.0 + jnp.arange(RET_HEADS, dtype=f32)
    base_logit = jnp.log(2.0 ** a - 1.0)
    return {
        "x": jax.random.normal(ks[0], (BATCH, SEQ, D), f32),
        "c": jax.random.normal(ks[1], (BATCH, D), f32),
        "ctx": jax.random.normal(ks[2], (BATCH, CTX_LEN, D), f32),
        "c_ctx": jax.random.normal(ks[3], (D,), f32),
        "w_mod": jax.random.normal(ks[4], (DEPTH, D, N_MOD * D), f32) * (0.5 * D ** -0.5),
        "b_mod": jax.random.normal(ks[5], (DEPTH, N_MOD * D), f32) * 0.02,
        "norm1_g": 1.0 + 0.02 * jax.random.normal(ks[6], (DEPTH, D), f32),
        "w_in": jax.random.normal(ks[7], (DEPTH, D, IN_COLS), f32) * D ** -0.5,
        "ret_decay_logit": base_logit[None, None, :] + 0.1 * jax.random.normal(ks[8], (DEPTH, 2, RET_HEADS), f32),
        "ret_gn_g": 1.0 + 0.02 * jax.random.normal(ks[9], (DEPTH, 2, RET_V), f32),
        "diff_lambda": 0.1 * jax.random.normal(ks[10], (DEPTH, 4, DIFF_HD), f32),
        "diff_subln_g": 1.0 + 0.02 * jax.random.normal(ks[11], (DEPTH, DIFF_V), f32),
        "w_out": jax.random.normal(ks[12], (DEPTH, MIX_WIDTH, D), f32) * MIX_WIDTH ** -0.5,
        "norm2_g": 1.0 + 0.02 * jax.random.normal(ks[13], (DEPTH, D), f32),
        "w_up": jax.random.normal(ks[14], (DEPTH, D, 2 * D_FF), f32) * D ** -0.5,
        "conv_w": jax.random.normal(ks[15], (DEPTH, CONV_WIDTH, 2 * D_FF), f32) * CONV_WIDTH ** -0.5,
        "conv_b": 0.02 * jax.random.normal(ks[16], (DEPTH, 2 * D_FF), f32),
        "w_down": jax.random.normal(ks[17], (DEPTH, D_FF, D), f32) * D_FF ** -0.5,
        "final_g": 1.0 + 0.02 * jax.random.normal(ks[18], (D,), f32),
    }


def reference(x, c, ctx, c_ctx, w_mod, b_mod, norm1_g, w_in, ret_decay_logit, ret_gn_g,
              diff_lambda, diff_subln_g, w_out, norm2_g, w_up, conv_w, conv_b, w_down, final_g):
    L = x.shape[1]
    rows = L // GRID_W
    rope_r = grid_rope(rows, RET_DK)
    rope_d = grid_rope(rows, DIFF_HD)
    sc = jax.nn.silu(c)
    scc = jax.nn.silu(c_ctx)
    for l in range(DEPTH):
        last = l == DEPTH - 1
        lambda_init = 0.8 - 0.6 * math.exp(-0.3 * l)
        m_lat = sc @ w_mod[l] + b_mod[l]
        m_ctx = scc @ w_mod[l] + b_mod[l]
        sh1, sc1, g1, sh2, sc2, g2 = [t[:, None, :] for t in jnp.split(m_lat, N_MOD, axis=-1)]
        csh1, csc1, cg1, csh2, csc2, cg2 = jnp.split(m_ctx, N_MOD, axis=-1)

        h_lat = rms_norm(x, norm1_g[l]) * (1.0 + sc1) + sh1
        h_ctx = rms_norm(ctx, norm1_g[l]) * (1.0 + csc1) + csh1
        y_lat, y_ctx = token_mixers(h_lat, h_ctx, w_in[l], ret_decay_logit[l], ret_gn_g[l],
                                    diff_lambda[l], diff_subln_g[l], w_out[l], lambda_init,
                                    rope_r, rope_d, need_ctx=not last)
        x = x + g1 * y_lat
        h2 = rms_norm(x, norm2_g[l]) * (1.0 + sc2) + sh2
        x = x + g2 * conv_ffn(h2, w_up[l], conv_w[l], conv_b[l], w_down[l])
        if not last:
            ctx = ctx + cg1 * y_ctx
            h2c = rms_norm(ctx, norm2_g[l]) * (1.0 + csc2) + csh2
            ctx = ctx + cg2 * conv_ffn(h2c, w_up[l], conv_w[l], conv_b[l], w_down[l])
    return rms_norm(x, final_g)
```

```python
import functools
import math

import jax
import jax.numpy as jnp
from jax import lax
from jax.experimental import pallas as pl
from jax.experimental.pallas import tpu as pltpu

F32 = jnp.float32
BF16 = jnp.bfloat16

GRID_W = 64
RET_HEADS = 4
RET_DK = 128
DIFF_HEADS = 4
DIFF_HD = 64
HEAD_W = 128
GROUP_W = 512
N_GROUPS = 7
CHUNK = 128
ROPE_BASE = 10000.0
EPS = 1e-6
GN_EPS = 1e-5
N_MOD = 6
LAMBDA_INIT = 0.8 - 0.6 * math.exp(-0.3 * 0)

VMEM_LIMIT = 56 * 1024 * 1024


def _cparams(*sem):
    return pltpu.CompilerParams(dimension_semantics=sem, vmem_limit_bytes=VMEM_LIMIT)


def _silu(v):
    return v * (1.0 / (1.0 + jnp.exp(-v)))


def _split_dot(a, w):
    a_hi = a.astype(BF16)
    a_lo = (a - a_hi.astype(F32)).astype(BF16)
    w_hi = w.astype(BF16)
    w_lo = (w - w_hi.astype(F32)).astype(BF16)
    dot = functools.partial(jnp.dot, preferred_element_type=F32)
    return dot(a_hi, w_hi) + (dot(a_hi, w_lo) + dot(a_lo, w_hi))


def _mod_kernel(c_ref, w_ref, b_ref, o_ref):
    o_ref[...] = _split_dot(_silu(c_ref[...]), w_ref[...]) + b_ref[...]


def _modulation(cc, w_mod, b_mod):
    rows, d = cc.shape
    n = w_mod.shape[1]
    tn = 1024
    return pl.pallas_call(
        _mod_kernel,
        grid=(n // tn,),
        in_specs=[pl.BlockSpec((rows, d), lambda j: (0, 0)),
                  pl.BlockSpec((d, tn), lambda j: (0, j)),
                  pl.BlockSpec((1, tn), lambda j: (0, j))],
        out_specs=pl.BlockSpec((rows, tn), lambda j: (0, j)),
        out_shape=jax.ShapeDtypeStruct((rows, n), F32),
        compiler_params=_cparams("arbitrary"),
        name="mod",
    )(cc, w_mod, b_mod)


def _rope128(v, cos, sin):
    return v * cos + pltpu.roll(v, 64, axis=1) * sin


def _rope64(v, cos, sin_lo, sin_hi):
    return v * cos + pltpu.roll(v, 96, axis=1) * sin_lo + pltpu.roll(v, 32, axis=1) * sin_hi


def _inproj_kernel(x_ref, a_ref, s_ref, w_ref, t_ref, *rest):
    o_ref = rest[-1]
    x = x_ref[...]
    xn = x * lax.rsqrt(jnp.mean(x * x, axis=-1, keepdims=True) + EPS)
    h = (xn * a_ref[...] + s_ref[...]).astype(BF16)
    cos_r = t_ref[:, 0 * HEAD_W:1 * HEAD_W]
    sin_r = t_ref[:, 1 * HEAD_W:2 * HEAD_W]
    cos_d = t_ref[:, 2 * HEAD_W:3 * HEAD_W]
    sin_dl = t_ref[:, 3 * HEAD_W:4 * HEAD_W]
    sin_dh = t_ref[:, 4 * HEAD_W:5 * HEAD_W]
    for g in range(N_GROUPS):
        acc = jnp.dot(h, w_ref[:, g * GROUP_W:(g + 1) * GROUP_W], preferred_element_type=F32)
        for hd in range(GROUP_W // HEAD_W):
            v = acc[:, hd * HEAD_W:(hd + 1) * HEAD_W]
            if g == 0:
                v = _rope128(v, cos_r, sin_r)
            elif g == 1:
                v = _rope128(v, cos_r, sin_r) * (RET_DK ** -0.5)
            elif g == 4:
                v = _rope64(v, cos_d, sin_dl, sin_dh) * (DIFF_HD ** -0.5)
            elif g == 5:
                v = _rope64(v, cos_d, sin_dl, sin_dh)
            c0 = g * GROUP_W + hd * HEAD_W
            o_ref[:, c0:c0 + HEAD_W] = v.astype(BF16)


def _inproj(x, a, s, w_in, tables, *, tm, row_block0, n_rows_total, p_prev=None):
    b, lx, d = x.shape
    n_cols = w_in.shape[1]
    nt = lx // tm
    in_specs = [
        pl.BlockSpec((None, tm, d), lambda t, i: (i, t, 0)),
        pl.BlockSpec((None, 1, d), lambda t, i: (i, 0, 0)),
        pl.BlockSpec((None, 1, d), lambda t, i: (i, 0, 0)),
        pl.BlockSpec((d, n_cols), lambda t, i: (0, 0)),
        pl.BlockSpec((tm, 5 * HEAD_W), lambda t, i: (t, 0)),
    ]
    args = [x, a, s, w_in, tables]
    aliases = {}
    if p_prev is not None:
        in_specs.append(pl.BlockSpec(memory_space=pl.ANY))
        args.append(p_prev)
        aliases = {5: 0}
    return pl.pallas_call(
        _inproj_kernel,
        grid=(nt, b),
        in_specs=in_specs,
        out_specs=pl.BlockSpec((None, tm, n_cols), lambda t, i: (i, t + row_block0, 0)),
        out_shape=jax.ShapeDtypeStruct((b, n_rows_total, n_cols), BF16),
        input_output_aliases=aliases,
        compiler_params=_cparams("arbitrary", "arbitrary"),
        name="inproj_ctx" if p_prev is not None else "inproj_lat",
    )(*args)


def _ret_kernel(q_ref, k_ref, v_ref, g_ref, lg_ref, gn_ref, o_ref, sf_ref, sb_ref, *, n_lat, n_all):
    c = CHUNK
    row = lax.broadcasted_iota(jnp.int32, (c, c), 0).astype(F32)
    col = lax.broadcasted_iota(jnp.int32, (c, c), 1).astype(F32)
    rel = row - col
    lg_f = lg_ref[0:1, :]
    lg_b = lg_ref[1:2, :]
    mask_f = jnp.where(rel >= 0, jnp.exp(jnp.maximum(rel, 0.0) * lg_f), 0.0)
    mask_b = jnp.where(rel <= 0, jnp.exp(jnp.maximum(-rel, 0.0) * lg_b), 0.0)
    qdec_f = jnp.exp((row + 1.0) * lg_f)
    qdec_b = jnp.exp((c - row) * lg_b)
    kdec_f = jnp.exp((c - 1.0 - row) * lg_f)
    kdec_b = jnp.exp(row * lg_b)
    cdec_f = jnp.exp(c * lg_f)
    cdec_b = jnp.exp(c * lg_b)
    tn_dims = (((0,), (0,)), ((), ()))
    nt_dims = (((1,), (1,)), ((), ()))

    def chunk_rows(idx):
        return pl.ds(pl.multiple_of(idx * c, c), c)

    def scan(s_ref, kdec, cdec, order):
        def body(t, state):
            idx = order(t)
            rows = chunk_rows(idx)
            kd = (k_ref[rows, :].astype(F32) * kdec).astype(BF16)
            u = lax.dot_general(kd, v_ref[rows, :], tn_dims, preferred_element_type=F32)
            s_ref[idx] = state.astype(BF16)
            return state * cdec + u
        lax.fori_loop(0, n_all, body, jnp.zeros((c, c), F32))

    scan(sf_ref, kdec_f, cdec_f, lambda t: lax.rem(t + n_lat, n_all))
    scan(sb_ref, kdec_b, cdec_b, lambda t: n_all - 1 - t)

    gn_f = gn_ref[0:1, :]
    gn_b = gn_ref[1:2, :]

    def group_norm(o, g):
        mu = jnp.mean(o, axis=-1, keepdims=True)
        d = o - mu
        var = jnp.mean(d * d, axis=-1, keepdims=True)
        return d * lax.rsqrt(var + GN_EPS) * g

    def out_body(idx, carry):
        rows = chunk_rows(idx)
        q = q_ref[rows, :]
        k = k_ref[rows, :]
        v = v_ref[rows, :]
        qf = q.astype(F32)
        a = lax.dot_general(q, k, nt_dims, preferred_element_type=F32)
        o_f = (jnp.dot((a * mask_f).astype(BF16), v, preferred_element_type=F32)
               + jnp.dot((qf * qdec_f).astype(BF16), sf_ref[idx], preferred_element_type=F32))
        o_b = (jnp.dot((a * mask_b).astype(BF16), v, preferred_element_type=F32)
               + jnp.dot((qf * qdec_b).astype(BF16), sb_ref[idx], preferred_element_type=F32))
        y = _silu(g_ref[rows, :].astype(F32)) * (group_norm(o_f, gn_f) + group_norm(o_b, gn_b))
        o_ref[rows, :] = y.astype(BF16)
        return carry

    lax.fori_loop(0, n_lat, out_body, 0)


def _retention(p, lgv, gn_g, l_lat):
    b, l_all, _ = p.shape
    n_lat = l_lat // CHUNK
    n_all = l_all // CHUNK
    col = lambda g: (lambda i, h: (i, 0, g * RET_HEADS + h))
    kern = functools.partial(_ret_kernel, n_lat=n_lat, n_all=n_all)
    return pl.pallas_call(
        kern,
        grid=(b, RET_HEADS),
        in_specs=[pl.BlockSpec((None, l_all, HEAD_W), col(0)),
                  pl.BlockSpec((None, l_all, HEAD_W), col(1)),
                  pl.BlockSpec((None, l_all, HEAD_W), col(2)),
                  pl.BlockSpec((None, l_all, HEAD_W), col(3)),
                  pl.BlockSpec((None, 2, HEAD_W), lambda i, h: (h, 0, 0)),
                  pl.BlockSpec((2, HEAD_W), lambda i, h: (0, h))],
        out_specs=pl.BlockSpec((None, l_lat, HEAD_W), lambda i, h: (i, 0, h)),
        out_shape=jax.ShapeDtypeStruct((b, l_lat, RET_HEADS * HEAD_W), BF16),
        scratch_shapes=[pltpu.VMEM((n_all, CHUNK, CHUNK), BF16),
                        pltpu.VMEM((n_all, CHUNK, CHUNK), BF16)],
        compiler_params=_cparams("arbitrary", "arbitrary"),
        name="ret",
    )(p, p, p, p, lgv, gn_g)


def _attn_kernel(q_ref, k_ref, v_ref, lam_ref, g_ref, o_ref, *, tq, tk, n_k):
    lane = lax.broadcasted_iota(jnp.int32, (tq, HEAD_W), 1)
    q = q_ref[...]
    zero = jnp.zeros_like(q)
    q1 = jnp.where(lane < DIFF_HD, q, zero)
    q2 = jnp.where(lane >= DIFF_HD, q, zero)
    nt_dims = (((1,), (1,)), ((), ()))
    tn_dims = (((0,), (0,)), ((), ()))

    def step(kt, vt, qm, state):
        m, l, acc = state
        s = lax.dot_general(kt, qm, nt_dims, preferred_element_type=F32)
        m_new = jnp.maximum(m, jnp.max(s, axis=0, keepdims=True))
        alpha = jnp.exp(m - m_new)
        pr = jnp.exp(s - m_new)
        l = alpha * l + jnp.sum(pr, axis=0, keepdims=True)
        pv = lax.dot_general(vt, pr.astype(BF16), tn_dims, preferred_element_type=F32)
        return m_new, l, alpha * acc + pv

    def body(i, carry):
        rows = pl.ds(pl.multiple_of(i * tk, tk), tk)
        kt = k_ref[rows, :]
        vt = v_ref[rows, :]
        return step(kt, vt, q1, carry[0]), step(kt, vt, q2, carry[1])

    init = (jnp.full((1, tq), -jnp.inf, F32), jnp.zeros((1, tq), F32), jnp.zeros((HEAD_W, tq), F32))
    (_, l1, acc1), (_, l2, acc2) = lax.fori_loop(0, n_k, body, (init, init))

    lp = lam_ref[...]
    lam = (jnp.exp(jnp.sum(lp[0:1, :] * lp[1:2, :], axis=-1, keepdims=True))
           - jnp.exp(jnp.sum(lp[2:3, :] * lp[3:4, :], axis=-1, keepdims=True)) + LAMBDA_INIT)
    o = acc1 / l1 - lam * (acc2 / l2)
    n = o * lax.rsqrt(jnp.mean(o * o, axis=0, keepdims=True) + EPS)
    o_ref[...] = (n.T * g_ref[...] * (1.0 - LAMBDA_INIT)).astype(BF16)


def _diff_attention(p, lam_par, subln_g, l_lat, *, tq, tk):
    b, l_all, _ = p.shape
    kern = functools.partial(_attn_kernel, tq=tq, tk=tk, n_k=l_all // tk)
    return pl.pallas_call(
        kern,
        grid=(b, DIFF_HEADS, l_lat // tq),
        in_specs=[pl.BlockSpec((None, tq, HEAD_W), lambda i, h, j: (i, j, 16 + h)),
                  pl.BlockSpec((None, l_all, HEAD_W), lambda i, h, j: (i, 0, 20 + h)),
                  pl.BlockSpec((None, l_all, HEAD_W), lambda i, h, j: (i, 0, 24 + h)),
                  pl.BlockSpec((4, DIFF_HD), lambda i, h, j: (0, 0)),
                  pl.BlockSpec((1, HEAD_W), lambda i, h, j: (0, h))],
        out_specs=pl.BlockSpec((None, tq, HEAD_W), lambda i, h, j: (i, j, h)),
        out_shape=jax.ShapeDtypeStruct((b, l_lat, DIFF_HEADS * HEAD_W), BF16),
        compiler_params=_cparams("arbitrary", "arbitrary", "arbitrary"),
        name="attn",
    )(p, p, p, lam_par, subln_g)


def _outproj_kernel(x_ref, yr_ref, yd_ref, w_ref, g1_ref, a2_ref, s2_ref, x1_ref, h2_ref):
    half = yr_ref.shape[-1]
    y = (jnp.dot(yr_ref[...], w_ref[:half, :], preferred_element_type=F32)
         + jnp.dot(yd_ref[...], w_ref[half:, :], preferred_element_type=F32))
    x1 = x_ref[...] + g1_ref[...] * y
    x1_ref[...] = x1
    xn = x1 * lax.rsqrt(jnp.mean(x1 * x1, axis=-1, keepdims=True) + EPS)
    h2_ref[...] = (xn * a2_ref[...] + s2_ref[...]).astype(BF16)


def _outproj(x, y_ret, y_diff, w_out, g1, a2, s2, *, tm):
    b, l, d = x.shape
    half = y_ret.shape[-1]
    tile = lambda w: pl.BlockSpec((None, tm, w), lambda i, t: (i, t, 0))
    vec = pl.BlockSpec((None, 1, d), lambda i, t: (i, 0, 0))
    return pl.pallas_call(
        _outproj_kernel,
        grid=(b, l // tm),
        in_specs=[tile(d), tile(half), tile(half),
                  pl.BlockSpec((2 * half, d), lambda i, t: (0, 0)), vec, vec, vec],
        out_specs=[tile(d), tile(d)],
        out_shape=[jax.ShapeDtypeStruct((b, l, d), F32), jax.ShapeDtypeStruct((b, l, d), BF16)],
        compiler_params=_cparams("arbitrary", "arbitrary"),
        name="outproj",
    )(x, y_ret, y_diff, w_out, g1, a2, s2)


HALO = 16


def _ffn_kernel(hp_ref, h_ref, hn_ref, x1_ref, wu_ref, cw_ref, cb_ref, wd_ref, g2_ref, fg_ref,
                o_ref, u_ref, act_ref, *, tm, d_ff, tn):
    t = pl.program_id(1)
    nt = pl.num_programs(1)
    hp = hp_ref[...]
    hn = hn_ref[...]
    hp = jnp.where(t > 0, hp, jnp.zeros_like(hp))
    hn = jnp.where(t < nt - 1, hn, jnp.zeros_like(hn))
    hh = jnp.concatenate([hp, h_ref[...], hn], axis=0)
    for j in range(d_ff // tn):
        for part in range(2):
            c0 = part * d_ff + j * tn
            u_ref[...] = jnp.dot(hh, wu_ref[:, c0:c0 + tn], preferred_element_type=F32)
            w = cw_ref[:, c0:c0 + tn]
            conv = (u_ref[pl.ds(HALO - 1, tm), :] * w[0:1, :]
                    + u_ref[pl.ds(HALO, tm), :] * w[1:2, :]
                    + u_ref[pl.ds(HALO + 1, tm), :] * w[2:3, :]
                    + cb_ref[:, c0:c0 + tn])
            if part == 0:
                gate = _silu(conv)
            else:
                act_ref[:, j * tn:(j + 1) * tn] = (gate * conv).astype(BF16)
    f = jnp.dot(act_ref[...], wd_ref[...], preferred_element_type=F32)
    x2 = x1_ref[...] + g2_ref[...] * f
    o_ref[...] = x2 * lax.rsqrt(jnp.mean(x2 * x2, axis=-1, keepdims=True) + EPS) * fg_ref[...]


def _ffn(h2, x1, w_up, conv_w, conv_b, w_down, g2, final_g, *, tm, tn):
    b, l, d = x1.shape
    d_ff = w_down.shape[0]
    nh = tm // HALO
    n_halo_blocks = l // HALO
    kern = functools.partial(_ffn_kernel, tm=tm, d_ff=d_ff, tn=tn)
    tile = pl.BlockSpec((None, tm, d), lambda i, t: (i, t, 0))
    prev = pl.BlockSpec((None, HALO, d), lambda i, t: (i, jnp.maximum(t * nh - 1, 0), 0))
    nxt = pl.BlockSpec((None, HALO, d), lambda i, t: (i, jnp.minimum((t + 1) * nh, n_halo_blocks - 1), 0))
    full = lambda a: pl.BlockSpec(a.shape, lambda i, t: (0,) * a.ndim)
    return pl.pallas_call(
        kern,
        grid=(b, l // tm),
        in_specs=[prev, tile, nxt, tile, full(w_up), full(conv_w), full(conv_b), full(w_down),
                  pl.BlockSpec((None, 1, d), lambda i, t: (i, 0, 0)), full(final_g)],
        out_specs=tile,
        out_shape=jax.ShapeDtypeStruct((b, l, d), F32),
        scratch_shapes=[pltpu.VMEM((tm + 2 * HALO, tn), F32), pltpu.VMEM((tm, d_ff), BF16)],
        compiler_params=_cparams("arbitrary", "arbitrary"),
        name="ffn",
    )(h2, h2, h2, x1, w_up, conv_w, conv_b, w_down, g2, final_g)


def _rope_tables(l):
    pos = jnp.arange(l)
    row = (pos // GRID_W).astype(F32)[:, None]
    col = (pos % GRID_W).astype(F32)[:, None]

    def angles(head_dim):
        n_freq = head_dim // 4
        inv = ROPE_BASE ** (-jnp.arange(n_freq, dtype=F32) / n_freq)
        return jnp.concatenate([row * inv, col * inv], axis=-1)

    ar = angles(RET_DK)
    ad = angles(DIFF_HD)
    cr, sr = jnp.cos(ar), jnp.sin(ar)
    cd, sd = jnp.cos(ad), jnp.sin(ad)
    zd = jnp.zeros_like(sd)
    return jnp.concatenate([
        cr, cr, -sr, sr,
        cd, cd, cd, cd,
        -sd, zd, -sd, zd,
        zd, sd, zd, sd], axis=-1)


def _identity_tables(l):
    one = jnp.ones((l, HEAD_W), F32)
    zero = jnp.zeros((l, HEAD_W), F32)
    return jnp.concatenate([one, zero, one, zero, zero], axis=-1)


def kernel(x, c, ctx, c_ctx, w_mod, b_mod, norm1_g, w_in, ret_decay_logit, ret_gn_g, diff_lambda,
           diff_subln_g, w_out, norm2_g, w_up, conv_w, conv_b, w_down, final_g):
    b, l, d = x.shape
    l_ctx = ctx.shape[1]
    l_all = l + l_ctx

    rows = -(-(b + 1) // 8) * 8
    cc = jnp.zeros((rows, d), F32).at[:b].set(c).at[b].set(c_ctx)
    m = _modulation(cc, w_mod[0], b_mod[0][None, :])
    sh1, sc1, g1, sh2, sc2, g2 = [m[:, i * d:(i + 1) * d] for i in range(N_MOD)]
    a1 = norm1_g[0][None, :] * (1.0 + sc1)
    a2 = norm2_g[0][None, :] * (1.0 + sc2)
    vec = lambda v: v[:b, None, :]
    ctx_vec = lambda v: jnp.broadcast_to(v[b][None, None, :], (b, 1, d))

    w_in_b = w_in[0].astype(BF16)
    p = _inproj(x, vec(a1), vec(sh1), w_in_b, _rope_tables(l),
                tm=min(512, l), row_block0=0, n_rows_total=l_all)
    p = _inproj(ctx, ctx_vec(a1), ctx_vec(sh1), w_in_b, _identity_tables(l_ctx),
                tm=l_ctx, row_block0=l // l_ctx, n_rows_total=l_all, p_prev=p)

    lg = jax.nn.log_sigmoid(ret_decay_logit[0].astype(F32))
    lgv = jnp.broadcast_to(lg.T[:, :, None], (RET_HEADS, 2, HEAD_W))
    y_ret = _retention(p, lgv, ret_gn_g[0], l)
    y_diff = _diff_attention(p, diff_lambda[0], diff_subln_g[0][None, :], l,
                             tq=min(256, l), tk=256)

    x1, h2 = _outproj(x, y_ret, y_diff, w_out[0].astype(BF16), vec(g1), vec(a2), vec(sh2),
                      tm=min(512, l))
    return _ffn(h2, x1, w_up[0].astype(BF16), conv_w[0], conv_b[0][None, :], w_down[0].astype(BF16),
                vec(g2), final_g[None, :], tm=min(512, l), tn=256)
```

```python
import functools
import math

import jax
import jax.numpy as jnp
from jax import lax
from jax.experimental import pallas as pl
from jax.experimental.pallas import tpu as pltpu

F32 = jnp.float32
BF16 = jnp.bfloat16

GRID_W = 64
RET_HEADS = 4
RET_DK = 128
DIFF_HEADS = 4
DIFF_HD = 64
HEAD_W = 128
GROUP_W = 512
N_GROUPS = 7
CHUNK = 128
ROPE_BASE = 10000.0
EPS = 1e-6
GN_EPS = 1e-5
N_MOD = 6
LAMBDA_INIT = 0.8 - 0.6 * math.exp(-0.3 * 0)
LOG2E = math.log2(math.e)
LOOKAHEAD = 2

VMEM_LIMIT = 56 * 1024 * 1024


def _cparams(*sem):
    return pltpu.CompilerParams(dimension_semantics=sem, vmem_limit_bytes=VMEM_LIMIT)


def _silu(v):
    return v * (1.0 / (1.0 + jnp.exp(-v)))


def _split_dot(a, w):
    a_hi = a.astype(BF16)
    a_lo = (a - a_hi.astype(F32)).astype(BF16)
    w_hi = w.astype(BF16)
    w_lo = (w - w_hi.astype(F32)).astype(BF16)
    dot = functools.partial(jnp.dot, preferred_element_type=F32)
    return dot(a_hi, w_hi) + (dot(a_hi, w_lo) + dot(a_lo, w_hi))


def _mod_kernel(c_ref, w_ref, b_ref, o_ref):
    o_ref[...] = _split_dot(_silu(c_ref[...]), w_ref[...]) + b_ref[...]


def _modulation(cc, w_mod, b_mod):
    rows, d = cc.shape
    n = w_mod.shape[1]
    tn = 1024
    return pl.pallas_call(
        _mod_kernel,
        grid=(n // tn,),
        in_specs=[pl.BlockSpec((rows, d), lambda j: (0, 0)),
                  pl.BlockSpec((d, tn), lambda j: (0, j)),
                  pl.BlockSpec((1, tn), lambda j: (0, j))],
        out_specs=pl.BlockSpec((rows, tn), lambda j: (0, j)),
        out_shape=jax.ShapeDtypeStruct((rows, n), F32),
        compiler_params=_cparams("arbitrary"),
        name="mod",
    )(cc, w_mod, b_mod)


def _rope128(v, cos, sin):
    return v * cos + pltpu.roll(v, 64, axis=1) * sin


def _rope64(v, cos, sin_lo, sin_hi):
    return v * cos + pltpu.roll(v, 96, axis=1) * sin_lo + pltpu.roll(v, 32, axis=1) * sin_hi


def _inproj_kernel(x_ref, a_ref, s_ref, w_ref, t_ref, *rest):
    o_ref = rest[-1]
    x = x_ref[...]
    xn = x * lax.rsqrt(jnp.mean(x * x, axis=-1, keepdims=True) + EPS)
    h = (xn * a_ref[...] + s_ref[...]).astype(BF16)
    cos_r = t_ref[:, 0 * HEAD_W:1 * HEAD_W]
    sin_r = t_ref[:, 1 * HEAD_W:2 * HEAD_W]
    cos_d = t_ref[:, 2 * HEAD_W:3 * HEAD_W]
    sin_dl = t_ref[:, 3 * HEAD_W:4 * HEAD_W]
    sin_dh = t_ref[:, 4 * HEAD_W:5 * HEAD_W]
    for g in range(N_GROUPS):
        acc = jnp.dot(h, w_ref[:, g * GROUP_W:(g + 1) * GROUP_W], preferred_element_type=F32)
        for hd in range(GROUP_W // HEAD_W):
            v = acc[:, hd * HEAD_W:(hd + 1) * HEAD_W]
            if g == 0:
                v = _rope128(v, cos_r, sin_r)
            elif g == 1:
                v = _rope128(v, cos_r, sin_r) * (RET_DK ** -0.5)
            elif g == 4:
                v = _rope64(v, cos_d, sin_dl, sin_dh) * (DIFF_HD ** -0.5 * LOG2E)
            elif g == 5:
                v = _rope64(v, cos_d, sin_dl, sin_dh)
            c0 = g * GROUP_W + hd * HEAD_W
            o_ref[:, c0:c0 + HEAD_W] = v.astype(BF16)


def _inproj(x, a, s, w_in, tables, *, tm, row_block0, n_rows_total, p_prev=None):
    b, lx, d = x.shape
    n_cols = w_in.shape[1]
    nt = lx // tm
    in_specs = [
        pl.BlockSpec((None, tm, d), lambda t, i: (i, t, 0)),
        pl.BlockSpec((None, 1, d), lambda t, i: (i, 0, 0)),
        pl.BlockSpec((None, 1, d), lambda t, i: (i, 0, 0)),
        pl.BlockSpec((d, n_cols), lambda t, i: (0, 0)),
        pl.BlockSpec((tm, 5 * HEAD_W), lambda t, i: (t, 0)),
    ]
    args = [x, a, s, w_in, tables]
    aliases = {}
    if p_prev is not None:
        in_specs.append(pl.BlockSpec(memory_space=pl.ANY))
        args.append(p_prev)
        aliases = {5: 0}
    return pl.pallas_call(
        _inproj_kernel,
        grid=(nt, b),
        in_specs=in_specs,
        out_specs=pl.BlockSpec((None, tm, n_cols), lambda t, i: (i, t + row_block0, 0)),
        out_shape=jax.ShapeDtypeStruct((b, n_rows_total, n_cols), BF16),
        input_output_aliases=aliases,
        compiler_params=_cparams("arbitrary", "arbitrary"),
        name="inproj_ctx" if p_prev is not None else "inproj_lat",
    )(*args)


def _ret_kernel(q_ref, k_ref, v_ref, g_ref, lg_ref, gn_ref, o_ref, sf_ref, sb_ref, *, n_lat, n_all):
    c = CHUNK
    row = lax.broadcasted_iota(jnp.int32, (c, c), 0).astype(F32)
    col = lax.broadcasted_iota(jnp.int32, (c, c), 1).astype(F32)
    rel = row - col
    lg_f = lg_ref[0:1, :]
    lg_b = lg_ref[1:2, :]
    mask_f = jnp.where(rel >= 0, jnp.exp(jnp.maximum(rel, 0.0) * lg_f), 0.0)
    mask_b = jnp.where(rel <= 0, jnp.exp(jnp.maximum(-rel, 0.0) * lg_b), 0.0)
    qdec_f = jnp.exp((row + 1.0) * lg_f)
    qdec_b = jnp.exp((c - row) * lg_b)
    kdec_f = jnp.exp((c - 1.0 - row) * lg_f)
    kdec_b = jnp.exp(row * lg_b)
    cdec_f = jnp.exp(c * lg_f)
    cdec_b = jnp.exp(c * lg_b)
    tn_dims = (((0,), (0,)), ((), ()))
    nt_dims = (((1,), (1,)), ((), ()))

    def chunk_rows(idx):
        return pl.ds(pl.multiple_of(idx * c, c), c)

    def scan(s_ref, kdec, cdec, order):
        def body(t, state):
            idx = order(t)
            rows = chunk_rows(idx)
            kd = (k_ref[rows, :].astype(F32) * kdec).astype(BF16)
            u = lax.dot_general(kd, v_ref[rows, :], tn_dims, preferred_element_type=F32)
            s_ref[idx] = state.astype(BF16)
            return state * cdec + u
        lax.fori_loop(0, n_all, body, jnp.zeros((c, c), F32))

    scan(sf_ref, kdec_f, cdec_f, lambda t: lax.rem(t + n_lat, n_all))
    scan(sb_ref, kdec_b, cdec_b, lambda t: n_all - 1 - t)

    gn_f = gn_ref[0:1, :]
    gn_b = gn_ref[1:2, :]

    def group_norm(o, g):
        mu = jnp.mean(o, axis=-1, keepdims=True)
        d = o - mu
        var = jnp.mean(d * d, axis=-1, keepdims=True)
        return d * lax.rsqrt(var + GN_EPS) * g

    def out_body(idx, carry):
        rows = chunk_rows(idx)
        q = q_ref[rows, :]
        k = k_ref[rows, :]
        v = v_ref[rows, :]
        qf = q.astype(F32)
        a = lax.dot_general(q, k, nt_dims, preferred_element_type=F32)
        o_f = (jnp.dot((a * mask_f).astype(BF16), v, preferred_element_type=F32)
               + jnp.dot((qf * qdec_f).astype(BF16), sf_ref[idx], preferred_element_type=F32))
        o_b = (jnp.dot((a * mask_b).astype(BF16), v, preferred_element_type=F32)
               + jnp.dot((qf * qdec_b).astype(BF16), sb_ref[idx], preferred_element_type=F32))
        y = _silu(g_ref[rows, :].astype(F32)) * (group_norm(o_f, gn_f) + group_norm(o_b, gn_b))
        o_ref[rows, :] = y.astype(BF16)
        return carry

    lax.fori_loop(0, n_lat, out_body, 0)


def _retention(p, lgv, gn_g, l_lat):
    b, l_all, _ = p.shape
    n_lat = l_lat // CHUNK
    n_all = l_all // CHUNK
    col = lambda g: (lambda i, h: (i, 0, g * RET_HEADS + h))
    kern = functools.partial(_ret_kernel, n_lat=n_lat, n_all=n_all)
    return pl.pallas_call(
        kern,
        grid=(b, RET_HEADS),
        in_specs=[pl.BlockSpec((None, l_all, HEAD_W), col(0)),
                  pl.BlockSpec((None, l_all, HEAD_W), col(1)),
                  pl.BlockSpec((None, l_all, HEAD_W), col(2)),
                  pl.BlockSpec((None, l_all, HEAD_W), col(3)),
                  pl.BlockSpec((None, 2, HEAD_W), lambda i, h: (h, 0, 0)),
                  pl.BlockSpec((2, HEAD_W), lambda i, h: (0, h))],
        out_specs=pl.BlockSpec((None, l_lat, HEAD_W), lambda i, h: (i, 0, h)),
        out_shape=jax.ShapeDtypeStruct((b, l_lat, RET_HEADS * HEAD_W), BF16),
        scratch_shapes=[pltpu.VMEM((n_all, CHUNK, CHUNK), BF16),
                        pltpu.VMEM((n_all, CHUNK, CHUNK), BF16)],
        compiler_params=_cparams("arbitrary", "arbitrary"),
        name="ret",
    )(p, p, p, p, lgv, gn_g)


def _attn_kernel(q_ref, k_ref, v_ref, lam_ref, g_ref, o_ref, *, tq, tk, n_k):
    lane = lax.broadcasted_iota(jnp.int32, (tq, HEAD_W), 1)
    q = q_ref[...]
    zero = jnp.zeros_like(q)
    q1 = jnp.where(lane < DIFF_HD, q, zero)
    q2 = jnp.where(lane >= DIFF_HD, q, zero)
    nt_dims = (((1,), (1,)), ((), ()))
    tn_dims = (((0,), (0,)), ((), ()))

    def scores(i):
        kt = k_ref[i * tk:(i + 1) * tk, :]
        return (lax.dot_general(kt, q1, nt_dims, preferred_element_type=F32),
                lax.dot_general(kt, q2, nt_dims, preferred_element_type=F32))

    def update(s, vt, state):
        m, l, acc = state
        m_new = jnp.maximum(m, jnp.max(s, axis=0, keepdims=True))
        alpha = jnp.exp2(m - m_new)
        pr = jnp.exp2(s - m_new)
        l = alpha * l + jnp.sum(pr, axis=0, keepdims=True)
        pv = lax.dot_general(vt, pr.astype(BF16), tn_dims, preferred_element_type=F32)
        return m_new, l, alpha * acc + pv

    init = (jnp.full((1, tq), -jnp.inf, F32), jnp.zeros((1, tq), F32), jnp.zeros((HEAD_W, tq), F32))
    st1 = st2 = init
    ahead = [scores(i) for i in range(min(LOOKAHEAD, n_k))]
    for i in range(n_k):
        s1, s2 = ahead.pop(0)
        if i + LOOKAHEAD < n_k:
            ahead.append(scores(i + LOOKAHEAD))
        vt = v_ref[i * tk:(i + 1) * tk, :]
        st1 = update(s1, vt, st1)
        st2 = update(s2, vt, st2)
    (_, l1, acc1), (_, l2, acc2) = st1, st2

    lp = lam_ref[...]
    lam = (jnp.exp(jnp.sum(lp[0:1, :] * lp[1:2, :], axis=-1, keepdims=True))
           - jnp.exp(jnp.sum(lp[2:3, :] * lp[3:4, :], axis=-1, keepdims=True)) + LAMBDA_INIT)
    o = acc1 / l1 - lam * (acc2 / l2)
    n = o * lax.rsqrt(jnp.mean(o * o, axis=0, keepdims=True) + EPS)
    o_ref[...] = (n.T * g_ref[...] * (1.0 - LAMBDA_INIT)).astype(BF16)


def _diff_attention(p, lam_par, subln_g, l_lat, *, tq, tk):
    b, l_all, _ = p.shape
    kern = functools.partial(_attn_kernel, tq=tq, tk=tk, n_k=l_all // tk)
    return pl.pallas_call(
        kern,
        grid=(b, DIFF_HEADS, l_lat // tq),
        in_specs=[pl.BlockSpec((None, tq, HEAD_W), lambda i, h, j: (i, j, 16 + h)),
                  pl.BlockSpec((None, l_all, HEAD_W), lambda i, h, j: (i, 0, 20 + h)),
                  pl.BlockSpec((None, l_all, HEAD_W), lambda i, h, j: (i, 0, 24 + h)),
                  pl.BlockSpec((4, DIFF_HD), lambda i, h, j: (0, 0)),
                  pl.BlockSpec((1, HEAD_W), lambda i, h, j: (0, h))],
        out_specs=pl.BlockSpec((None, tq, HEAD_W), lambda i, h, j: (i, j, h)),
        out_shape=jax.ShapeDtypeStruct((b, l_lat, DIFF_HEADS * HEAD_W), BF16),
        compiler_params=_cparams("arbitrary", "arbitrary", "arbitrary"),
        name="attn",
    )(p, p, p, lam_par, subln_g)


def _outproj_kernel(x_ref, yr_ref, yd_ref, w_ref, g1_ref, a2_ref, s2_ref, x1_ref, h2_ref):
    half = yr_ref.shape[-1]
    y = (jnp.dot(yr_ref[...], w_ref[:half, :], preferred_element_type=F32)
         + jnp.dot(yd_ref[...], w_ref[half:, :], preferred_element_type=F32))
    x1 = x_ref[...] + g1_ref[...] * y
    x1_ref[...] = x1
    xn = x1 * lax.rsqrt(jnp.mean(x1 * x1, axis=-1, keepdims=True) + EPS)
    h2_ref[...] = (xn * a2_ref[...] + s2_ref[...]).astype(BF16)


def _outproj(x, y_ret, y_diff, w_out, g1, a2, s2, *, tm):
    b, l, d = x.shape
    half = y_ret.shape[-1]
    tile = lambda w: pl.BlockSpec((None, tm, w), lambda i, t: (i, t, 0))
    vec = pl.BlockSpec((None, 1, d), lambda i, t: (i, 0, 0))
    return pl.pallas_call(
        _outproj_kernel,
        grid=(b, l // tm),
        in_specs=[tile(d), tile(half), tile(half),
                  pl.BlockSpec((2 * half, d), lambda i, t: (0, 0)), vec, vec, vec],
        out_specs=[tile(d), tile(d)],
        out_shape=[jax.ShapeDtypeStruct((b, l, d), F32), jax.ShapeDtypeStruct((b, l, d), BF16)],
        compiler_params=_cparams("arbitrary", "arbitrary"),
        name="outproj",
    )(x, y_ret, y_diff, w_out, g1, a2, s2)


HALO = 16


def _ffn_kernel(hp_ref, h_ref, hn_ref, x1_ref, wu_ref, cw_ref, cb_ref, wd_ref, g2_ref, fg_ref,
                o_ref, u_ref, act_ref, *, tm, d_ff, tn):
    t = pl.program_id(1)
    nt = pl.num_programs(1)
    hp = hp_ref[...]
    hn = hn_ref[...]
    hp = jnp.where(t > 0, hp, jnp.zeros_like(hp))
    hn = jnp.where(t < nt - 1, hn, jnp.zeros_like(hn))
    hh = jnp.concatenate([hp, h_ref[...], hn], axis=0)
    for j in range(d_ff // tn):
        for part in range(2):
            c0 = part * d_ff + j * tn
            u_ref[...] = jnp.dot(hh, wu_ref[:, c0:c0 + tn], preferred_element_type=F32)
            w = cw_ref[:, c0:c0 + tn]
            conv = (u_ref[pl.ds(HALO - 1, tm), :] * w[0:1, :]
                    + u_ref[pl.ds(HALO, tm), :] * w[1:2, :]
                    + u_ref[pl.ds(HALO + 1, tm), :] * w[2:3, :]
                    + cb_ref[:, c0:c0 + tn])
            if part == 0:
                gate = _silu(conv)
            else:
                act_ref[:, j * tn:(j + 1) * tn] = (gate * conv).astype(BF16)
    f = jnp.dot(act_ref[...], wd_ref[...], preferred_element_type=F32)
    x2 = x1_ref[...] + g2_ref[...] * f
    o_ref[...] = x2 * lax.rsqrt(jnp.mean(x2 * x2, axis=-1, keepdims=True) + EPS) * fg_ref[...]


def _ffn(h2, x1, w_up, conv_w, conv_b, w_down, g2, final_g, *, tm, tn):
    b, l, d = x1.shape
    d_ff = w_down.shape[0]
    nh = tm // HALO
    n_halo_blocks = l // HALO
    kern = functools.partial(_ffn_kernel, tm=tm, d_ff=d_ff, tn=tn)
    tile = pl.BlockSpec((None, tm, d), lambda i, t: (i, t, 0))
    prev = pl.BlockSpec((None, HALO, d), lambda i, t: (i, jnp.maximum(t * nh - 1, 0), 0))
    nxt = pl.BlockSpec((None, HALO, d), lambda i, t: (i, jnp.minimum((t + 1) * nh, n_halo_blocks - 1), 0))
    full = lambda a: pl.BlockSpec(a.shape, lambda i, t: (0,) * a.ndim)
    return pl.pallas_call(
        kern,
        grid=(b, l // tm),
        in_specs=[prev, tile, nxt, tile, full(w_up), full(conv_w), full(conv_b), full(w_down),
                  pl.BlockSpec((None, 1, d), lambda i, t: (i, 0, 0)), full(final_g)],
        out_specs=tile,
        out_shape=jax.ShapeDtypeStruct((b, l, d), F32),
        scratch_shapes=[pltpu.VMEM((tm + 2 * HALO, tn), F32), pltpu.VMEM((tm, d_ff), BF16)],
        compiler_params=_cparams("arbitrary", "arbitrary"),
        name="ffn",
    )(h2, h2, h2, x1, w_up, conv_w, conv_b, w_down, g2, final_g)


def _rope_tables(l):
    pos = jnp.arange(l)
    row = (pos // GRID_W).astype(F32)[:, None]
    col = (pos % GRID_W).astype(F32)[:, None]

    def angles(head_dim):
        n_freq = head_dim // 4
        inv = ROPE_BASE ** (-jnp.arange(n_freq, dtype=F32) / n_freq)
        return jnp.concatenate([row * inv, col * inv], axis=-1)

    ar = angles(RET_DK)
    ad = angles(DIFF_HD)
    cr, sr = jnp.cos(ar), jnp.sin(ar)
    cd, sd = jnp.cos(ad), jnp.sin(ad)
    zd = jnp.zeros_like(sd)
    return jnp.concatenate([
        cr, cr, -sr, sr,
        cd, cd, cd, cd,
        -sd, zd, -sd, zd,
        zd, sd, zd, sd], axis=-1)


def _identity_tables(l):
    one = jnp.ones((l, HEAD_W), F32)
    zero = jnp.zeros((l, HEAD_W), F32)
    return jnp.concatenate([one, zero, one, zero, zero], axis=-1)


def kernel(x, c, ctx, c_ctx, w_mod, b_mod, norm1_g, w_in, ret_decay_logit, ret_gn_g, diff_lambda,
           diff_subln_g, w_out, norm2_g, w_up, conv_w, conv_b, w_down, final_g):
    b, l, d = x.shape
    l_ctx = ctx.shape[1]
    l_all = l + l_ctx

    rows = -(-(b + 1) // 8) * 8
    cc = jnp.zeros((rows, d), F32).at[:b].set(c).at[b].set(c_ctx)
    m = _modulation(cc, w_mod[0], b_mod[0][None, :])
    sh1, sc1, g1, sh2, sc2, g2 = [m[:, i * d:(i + 1) * d] for i in range(N_MOD)]
    a1 = norm1_g[0][None, :] * (1.0 + sc1)
    a2 = norm2_g[0][None, :] * (1.0 + sc2)
    vec = lambda v: v[:b, None, :]
    ctx_vec = lambda v: jnp.broadcast_to(v[b][None, None, :], (b, 1, d))

    w_in_b = w_in[0].astype(BF16)
    p = _inproj(x, vec(a1), vec(sh1), w_in_b, _rope_tables(l),
                tm=min(512, l), row_block0=0, n_rows_total=l_all)
    p = _inproj(ctx, ctx_vec(a1), ctx_vec(sh1), w_in_b, _identity_tables(l_ctx),
                tm=l_ctx, row_block0=l // l_ctx, n_rows_total=l_all, p_prev=p)

    lg = jax.nn.log_sigmoid(ret_decay_logit[0].astype(F32))
    lgv = jnp.broadcast_to(lg.T[:, :, None], (RET_HEADS, 2, HEAD_W))
    y_ret = _retention(p, lgv, ret_gn_g[0], l)
    y_diff = _diff_attention(p, diff_lambda[0], diff_subln_g[0][None, :], l,
                             tq=min(256, l), tk=256)

    x1, h2 = _outproj(x, y_ret, y_diff, w_out[0].astype(BF16), vec(g1), vec(a2), vec(sh2),
                      tm=min(512, l))
    return _ffn(h2, x1, w_up[0].astype(BF16), conv_w[0], conv_b[0][None, :], w_down[0].astype(BF16),
                vec(g2), final_g[None, :], tm=min(512, l), tn=256)
```

```python
import functools
import math

import jax
import jax.numpy as jnp
from jax import lax
from jax.experimental import pallas as pl
from jax.experimental.pallas import tpu as pltpu

F32 = jnp.float32
BF16 = jnp.bfloat16

GRID_W = 64
RET_HEADS = 4
RET_DK = 128
DIFF_HEADS = 4
DIFF_HD = 64
HEAD_W = 128
GROUP_W = 512
N_GROUPS = 7
RET_CHUNK = 256
ROPE_BASE = 10000.0
EPS = 1e-6
GN_EPS = 1e-5
N_MOD = 6
LAMBDA_INIT = 0.8 - 0.6 * math.exp(-0.3 * 0)
LOG2E = math.log2(math.e)
LOOKAHEAD = 2

VMEM_LIMIT = 56 * 1024 * 1024


def _cparams(*sem):
    return pltpu.CompilerParams(dimension_semantics=sem, vmem_limit_bytes=VMEM_LIMIT)


def _silu(v):
    return v * (1.0 / (1.0 + jnp.exp(-v)))


def _split_dot(a, w):
    a_hi = a.astype(BF16)
    a_lo = (a - a_hi.astype(F32)).astype(BF16)
    w_hi = w.astype(BF16)
    w_lo = (w - w_hi.astype(F32)).astype(BF16)
    dot = functools.partial(jnp.dot, preferred_element_type=F32)
    return dot(a_hi, w_hi) + (dot(a_hi, w_lo) + dot(a_lo, w_hi))


def _mod_kernel(c_ref, w_ref, b_ref, o_ref):
    o_ref[...] = _split_dot(_silu(c_ref[...]), w_ref[...]) + b_ref[...]


def _modulation(cc, w_mod, b_mod):
    rows, d = cc.shape
    n = w_mod.shape[1]
    tn = 1024
    return pl.pallas_call(
        _mod_kernel,
        grid=(n // tn,),
        in_specs=[pl.BlockSpec((rows, d), lambda j: (0, 0)),
                  pl.BlockSpec((d, tn), lambda j: (0, j)),
                  pl.BlockSpec((1, tn), lambda j: (0, j))],
        out_specs=pl.BlockSpec((rows, tn), lambda j: (0, j)),
        out_shape=jax.ShapeDtypeStruct((rows, n), F32),
        compiler_params=_cparams("arbitrary"),
        name="mod",
    )(cc, w_mod, b_mod)


def _rope128(v, cos, sin):
    return v * cos + pltpu.roll(v, 64, axis=1) * sin


def _rope64(v, cos, sin_lo, sin_hi):
    return v * cos + pltpu.roll(v, 96, axis=1) * sin_lo + pltpu.roll(v, 32, axis=1) * sin_hi


def _inproj_kernel(x_ref, a_ref, s_ref, w_ref, t_ref, o_ref):
    x = x_ref[...]
    xn = x * lax.rsqrt(jnp.mean(x * x, axis=-1, keepdims=True) + EPS)
    h = (xn * a_ref[...] + s_ref[...]).astype(BF16)
    cos_r = t_ref[:, 0 * HEAD_W:1 * HEAD_W]
    sin_r = t_ref[:, 1 * HEAD_W:2 * HEAD_W]
    cos_d = t_ref[:, 2 * HEAD_W:3 * HEAD_W]
    sin_dl = t_ref[:, 3 * HEAD_W:4 * HEAD_W]
    sin_dh = t_ref[:, 4 * HEAD_W:5 * HEAD_W]
    for g in range(N_GROUPS):
        acc = jnp.dot(h, w_ref[:, g * GROUP_W:(g + 1) * GROUP_W], preferred_element_type=F32)
        for hd in range(GROUP_W // HEAD_W):
            v = acc[:, hd * HEAD_W:(hd + 1) * HEAD_W]
            if g == 0:
                v = _rope128(v, cos_r, sin_r)
            elif g == 1:
                v = _rope128(v, cos_r, sin_r) * (RET_DK ** -0.5)
            elif g == 4:
                v = _rope64(v, cos_d, sin_dl, sin_dh) * (DIFF_HD ** -0.5 * LOG2E)
            elif g == 5:
                v = _rope64(v, cos_d, sin_dl, sin_dh)
            c0 = g * GROUP_W + hd * HEAD_W
            o_ref[:, c0:c0 + HEAD_W] = v.astype(BF16)


def _inproj(x, a, s, w_in, tables, *, tm, name):
    b, lx, d = x.shape
    n_cols = w_in.shape[1]
    return pl.pallas_call(
        _inproj_kernel,
        grid=(lx // tm, b),
        in_specs=[pl.BlockSpec((None, tm, d), lambda t, i: (i, t, 0)),
                  pl.BlockSpec((None, 1, d), lambda t, i: (i, 0, 0)),
                  pl.BlockSpec((None, 1, d), lambda t, i: (i, 0, 0)),
                  pl.BlockSpec((d, n_cols), lambda t, i: (0, 0)),
                  pl.BlockSpec((tm, 5 * HEAD_W), lambda t, i: (t, 0))],
        out_specs=pl.BlockSpec((None, tm, n_cols), lambda t, i: (i, t, 0)),
        out_shape=jax.ShapeDtypeStruct((b, lx, n_cols), BF16),
        compiler_params=_cparams("arbitrary", "arbitrary"),
        name=name,
    )(x, a, s, w_in, tables)


def _ret_kernel(q_ref, k_ref, v_ref, g_ref, kc_ref, vc_ref, lg_ref, gn_ref, o_ref, u_ref, s_ref,
                *, n_lat, n_all):
    c = RET_CHUNK
    dk = HEAD_W
    row = lax.broadcasted_iota(jnp.int32, (c, c), 0).astype(F32)
    col = lax.broadcasted_iota(jnp.int32, (c, c), 1).astype(F32)
    tok = lax.broadcasted_iota(jnp.int32, (c, dk), 0).astype(F32)
    lg_f = lg_ref[0:1, :]
    lg_b = lg_ref[1:2, :]
    lg_f2 = jnp.concatenate([lg_f, lg_f], axis=1)
    lg_b2 = jnp.concatenate([lg_b, lg_b], axis=1)
    mask_f = jnp.where(col >= row, jnp.exp(jnp.maximum(col - row, 0.0) * lg_f2), 0.0)
    mask_b = jnp.where(row >= col, jnp.exp(jnp.maximum(row - col, 0.0) * lg_b2), 0.0)
    qdec_f = jnp.exp((tok + 1.0) * lg_f)
    qdec_b = jnp.exp((c - tok) * lg_b)
    kdec_f = jnp.exp((c - 1.0 - tok) * lg_f)
    kdec_b = jnp.exp(tok * lg_b)
    cdec_f = jnp.exp(c * lg_f)
    cdec_b = jnp.exp(c * lg_b)
    tn_dims = (((0,), (0,)), ((), ()))
    nt_dims = (((1,), (1,)), ((), ()))

    def rows(idx):
        return slice(idx * c, (idx + 1) * c)

    def kv_chunk(idx):
        if idx < n_lat:
            return k_ref[rows(idx), :], v_ref[rows(idx), :]
        return kc_ref[rows(idx - n_lat), :], vc_ref[rows(idx - n_lat), :]

    for idx in range(n_all):
        k, v = kv_chunk(idx)
        kf = k.astype(F32)
        kd = jnp.concatenate([(kf * kdec_f).astype(BF16), (kf * kdec_b).astype(BF16)], axis=1)
        u_ref[idx] = lax.dot_general(v, kd, tn_dims, preferred_element_type=F32)

    state = jnp.zeros((dk, dk), F32)
    for idx in [*range(n_lat, n_all), *range(n_lat)]:
        s_ref[idx, :, 0:dk] = state.astype(BF16)
        state = state * cdec_f + u_ref[idx, :, 0:dk]
    state = jnp.zeros((dk, dk), F32)
    for idx in reversed(range(n_all)):
        s_ref[idx, :, dk:2 * dk] = state.astype(BF16)
        state = state * cdec_b + u_ref[idx, :, dk:2 * dk]

    gn_f = jnp.broadcast_to(gn_ref[0], (dk, c))
    gn_b = jnp.broadcast_to(gn_ref[1], (dk, c))

    def group_norm_t(o, g):
        mu = jnp.mean(o, axis=0, keepdims=True)
        d = o - mu
        var = jnp.mean(d * d, axis=0, keepdims=True)
        return d * lax.rsqrt(var + GN_EPS) * g

    def scores_t(idx):
        return lax.dot_general(k_ref[rows(idx), :], q_ref[rows(idx), :], nt_dims,
                               preferred_element_type=F32)

    ahead = [scores_t(i) for i in range(min(LOOKAHEAD, n_lat))]
    for idx in range(n_lat):
        a_t = ahead.pop(0)
        if idx + LOOKAHEAD < n_lat:
            ahead.append(scores_t(idx + LOOKAHEAD))
        v = v_ref[rows(idx), :]
        qf = q_ref[rows(idx), :].astype(F32)
        o_f = (lax.dot_general(v, (a_t * mask_f).astype(BF16), tn_dims, preferred_element_type=F32)
               + lax.dot_general(s_ref[idx, :, 0:dk], (qf * qdec_f).astype(BF16), nt_dims,
                                 preferred_element_type=F32))
        o_b = (lax.dot_general(v, (a_t * mask_b).astype(BF16), tn_dims, preferred_element_type=F32)
               + lax.dot_general(s_ref[idx, :, dk:2 * dk], (qf * qdec_b).astype(BF16), nt_dims,
                                 preferred_element_type=F32))
        n_t = group_norm_t(o_f, gn_f) + group_norm_t(o_b, gn_b)
        y = _silu(g_ref[rows(idx), :].astype(F32)) * n_t.T
        o_ref[rows(idx), :] = y.astype(BF16)


def _retention(p_lat, p_ctx, lgv, gn_cols):
    b, l_lat, _ = p_lat.shape
    l_ctx = p_ctx.shape[1]
    n_lat = l_lat // RET_CHUNK
    n_all = n_lat + l_ctx // RET_CHUNK
    col = lambda rows, g: pl.BlockSpec((None, rows, HEAD_W), lambda i, h: (i, 0, g * RET_HEADS + h))
    kern = functools.partial(_ret_kernel, n_lat=n_lat, n_all=n_all)
    return pl.pallas_call(
        kern,
        grid=(b, RET_HEADS),
        in_specs=[col(l_lat, 0), col(l_lat, 1), col(l_lat, 2), col(l_lat, 3),
                  col(l_ctx, 1), col(l_ctx, 2),
                  pl.BlockSpec((None, 2, HEAD_W), lambda i, h: (h, 0, 0)),
                  pl.BlockSpec((None, 2, HEAD_W, 1), lambda i, h: (h, 0, 0, 0))],
        out_specs=pl.BlockSpec((None, l_lat, HEAD_W), lambda i, h: (i, 0, h)),
        out_shape=jax.ShapeDtypeStruct((b, l_lat, RET_HEADS * HEAD_W), BF16),
        scratch_shapes=[pltpu.VMEM((n_all, HEAD_W, 2 * HEAD_W), F32),
                        pltpu.VMEM((n_all, HEAD_W, 2 * HEAD_W), BF16)],
        compiler_params=_cparams("arbitrary", "arbitrary"),
        name="ret",
    )(p_lat, p_lat, p_lat, p_lat, p_ctx, p_ctx, lgv, gn_cols)


def _attn_kernel(q_ref, k_ref, v_ref, kc_ref, vc_ref, lam_ref, g_ref, o_ref, *, tq, tk, n_lat, n_k):
    lane = lax.broadcasted_iota(jnp.int32, (tq, HEAD_W), 1)
    q = q_ref[...]
    zero = jnp.zeros_like(q)
    q1 = jnp.where(lane < DIFF_HD, q, zero)
    q2 = jnp.where(lane >= DIFF_HD, q, zero)
    nt_dims = (((1,), (1,)), ((), ()))
    tn_dims = (((0,), (0,)), ((), ()))

    def tile(i, lat_ref, ctx_ref):
        if i < n_lat:
            return lat_ref[i * tk:(i + 1) * tk, :]
        return ctx_ref[(i - n_lat) * tk:(i - n_lat + 1) * tk, :]

    def scores(i):
        kt = tile(i, k_ref, kc_ref)
        return (lax.dot_general(kt, q1, nt_dims, preferred_element_type=F32),
                lax.dot_general(kt, q2, nt_dims, preferred_element_type=F32))

    def update(s, vt, state):
        m, l, acc = state
        m_new = jnp.maximum(m, jnp.max(s, axis=0, keepdims=True))
        alpha = jnp.exp2(m - m_new)
        pr = jnp.exp2(s - m_new)
        l = alpha * l + jnp.sum(pr, axis=0, keepdims=True)
        pv = lax.dot_general(vt, pr.astype(BF16), tn_dims, preferred_element_type=F32)
        return m_new, l, alpha * acc + pv

    init = (jnp.full((1, tq), -jnp.inf, F32), jnp.zeros((1, tq), F32), jnp.zeros((HEAD_W, tq), F32))
    st1 = st2 = init
    ahead = [scores(i) for i in range(min(LOOKAHEAD, n_k))]
    for i in range(n_k):
        s1, s2 = ahead.pop(0)
        if i + LOOKAHEAD < n_k:
            ahead.append(scores(i + LOOKAHEAD))
        vt = tile(i, v_ref, vc_ref)
        st1 = update(s1, vt, st1)
        st2 = update(s2, vt, st2)
    (_, l1, acc1), (_, l2, acc2) = st1, st2

    lp = lam_ref[...]
    lam = (jnp.exp(jnp.sum(lp[0:1, :] * lp[1:2, :], axis=-1, keepdims=True))
           - jnp.exp(jnp.sum(lp[2:3, :] * lp[3:4, :], axis=-1, keepdims=True)) + LAMBDA_INIT)
    o = acc1 / l1 - lam * (acc2 / l2)
    n = o * lax.rsqrt(jnp.mean(o * o, axis=0, keepdims=True) + EPS)
    o_ref[...] = (n.T * g_ref[...] * (1.0 - LAMBDA_INIT)).astype(BF16)


def _diff_attention(p_lat, p_ctx, lam_par, subln_g, *, tq, tk):
    b, l_lat, _ = p_lat.shape
    l_ctx = p_ctx.shape[1]
    kern = functools.partial(_attn_kernel, tq=tq, tk=tk, n_lat=l_lat // tk, n_k=(l_lat + l_ctx) // tk)
    col = lambda rows, g: pl.BlockSpec((None, rows, HEAD_W), lambda i, h, j: (i, 0, g * DIFF_HEADS + h))
    return pl.pallas_call(
        kern,
        grid=(b, DIFF_HEADS, l_lat // tq),
        in_specs=[pl.BlockSpec((None, tq, HEAD_W), lambda i, h, j: (i, j, 4 * DIFF_HEADS + h)),
                  col(l_lat, 5), col(l_lat, 6), col(l_ctx, 5), col(l_ctx, 6),
                  pl.BlockSpec((4, DIFF_HD), lambda i, h, j: (0, 0)),
                  pl.BlockSpec((1, HEAD_W), lambda i, h, j: (0, h))],
        out_specs=pl.BlockSpec((None, tq, HEAD_W), lambda i, h, j: (i, j, h)),
        out_shape=jax.ShapeDtypeStruct((b, l_lat, DIFF_HEADS * HEAD_W), BF16),
        compiler_params=_cparams("arbitrary", "arbitrary", "arbitrary"),
        name="attn",
    )(p_lat, p_lat, p_lat, p_ctx, p_ctx, lam_par, subln_g)


def _outproj_kernel(x_ref, yr_ref, yd_ref, w_ref, g1_ref, a2_ref, s2_ref, x1_ref, h2_ref):
    half = yr_ref.shape[-1]
    y = (jnp.dot(yr_ref[...], w_ref[:half, :], preferred_element_type=F32)
         + jnp.dot(yd_ref[...], w_ref[half:, :], preferred_element_type=F32))
    x1 = x_ref[...] + g1_ref[...] * y
    x1_ref[...] = x1
    xn = x1 * lax.rsqrt(jnp.mean(x1 * x1, axis=-1, keepdims=True) + EPS)
    h2_ref[...] = (xn * a2_ref[...] + s2_ref[...]).astype(BF16)


def _outproj(x, y_ret, y_diff, w_out, g1, a2, s2, *, tm):
    b, l, d = x.shape
    half = y_ret.shape[-1]
    tile = lambda w: pl.BlockSpec((None, tm, w), lambda i, t: (i, t, 0))
    vec = pl.BlockSpec((None, 1, d), lambda i, t: (i, 0, 0))
    return pl.pallas_call(
        _outproj_kernel,
        grid=(b, l // tm),
        in_specs=[tile(d), tile(half), tile(half),
                  pl.BlockSpec((2 * half, d), lambda i, t: (0, 0)), vec, vec, vec],
        out_specs=[tile(d), tile(d)],
        out_shape=[jax.ShapeDtypeStruct((b, l, d), F32), jax.ShapeDtypeStruct((b, l, d), BF16)],
        compiler_params=_cparams("arbitrary", "arbitrary"),
        name="outproj",
    )(x, y_ret, y_diff, w_out, g1, a2, s2)


HALO = 16


def _ffn_kernel(hp_ref, h_ref, hn_ref, x1_ref, wu_ref, cw_ref, cb_ref, wd_ref, g2_ref, fg_ref,
                o_ref, u_ref, act_ref, *, tm, d_ff, tn):
    t = pl.program_id(1)
    nt = pl.num_programs(1)
    hp = hp_ref[...]
    hn = hn_ref[...]
    hp = jnp.where(t > 0, hp, jnp.zeros_like(hp))
    hn = jnp.where(t < nt - 1, hn, jnp.zeros_like(hn))
    hh = jnp.concatenate([hp, h_ref[...], hn], axis=0)
    for j in range(d_ff // tn):
        for part in range(2):
            c0 = part * d_ff + j * tn
            u_ref[...] = jnp.dot(hh, wu_ref[:, c0:c0 + tn], preferred_element_type=F32)
            w = cw_ref[:, c0:c0 + tn]
            conv = (u_ref[pl.ds(HALO - 1, tm), :] * w[0:1, :]
                    + u_ref[pl.ds(HALO, tm), :] * w[1:2, :]
                    + u_ref[pl.ds(HALO + 1, tm), :] * w[2:3, :]
                    + cb_ref[:, c0:c0 + tn])
            if part == 0:
                gate = _silu(conv)
            else:
                act_ref[:, j * tn:(j + 1) * tn] = (gate * conv).astype(BF16)
    f = jnp.dot(act_ref[...], wd_ref[...], preferred_element_type=F32)
    x2 = x1_ref[...] + g2_ref[...] * f
    o_ref[...] = x2 * lax.rsqrt(jnp.mean(x2 * x2, axis=-1, keepdims=True) + EPS) * fg_ref[...]


def _ffn(h2, x1, w_up, conv_w, conv_b, w_down, g2, final_g, *, tm, tn):
    b, l, d = x1.shape
    d_ff = w_down.shape[0]
    nh = tm // HALO
    n_halo_blocks = l // HALO
    kern = functools.partial(_ffn_kernel, tm=tm, d_ff=d_ff, tn=tn)
    tile = pl.BlockSpec((None, tm, d), lambda i, t: (i, t, 0))
    prev = pl.BlockSpec((None, HALO, d), lambda i, t: (i, jnp.maximum(t * nh - 1, 0), 0))
    nxt = pl.BlockSpec((None, HALO, d), lambda i, t: (i, jnp.minimum((t + 1) * nh, n_halo_blocks - 1), 0))
    full = lambda a: pl.BlockSpec(a.shape, lambda i, t: (0,) * a.ndim)
    return pl.pallas_call(
        kern,
        grid=(b, l // tm),
        in_specs=[prev, tile, nxt, tile, full(w_up), full(conv_w), full(conv_b), full(w_down),
                  pl.BlockSpec((None, 1, d), lambda i, t: (i, 0, 0)), full(final_g)],
        out_specs=tile,
        out_shape=jax.ShapeDtypeStruct((b, l, d), F32),
        scratch_shapes=[pltpu.VMEM((tm + 2 * HALO, tn), F32), pltpu.VMEM((tm, d_ff), BF16)],
        compiler_params=_cparams("arbitrary", "arbitrary"),
        name="ffn",
    )(h2, h2, h2, x1, w_up, conv_w, conv_b, w_down, g2, final_g)


def _rope_tables(l):
    pos = jnp.arange(l)
    row = (pos // GRID_W).astype(F32)[:, None]
    col = (pos % GRID_W).astype(F32)[:, None]

    def angles(head_dim):
        n_freq = head_dim // 4
        inv = ROPE_BASE ** (-jnp.arange(n_freq, dtype=F32) / n_freq)
        return jnp.concatenate([row * inv, col * inv], axis=-1)

    ar = angles(RET_DK)
    ad = angles(DIFF_HD)
    cr, sr = jnp.cos(ar), jnp.sin(ar)
    cd, sd = jnp.cos(ad), jnp.sin(ad)
    zd = jnp.zeros_like(sd)
    return jnp.concatenate([
        cr, cr, -sr, sr,
        cd, cd, cd, cd,
        -sd, zd, -sd, zd,
        zd, sd, zd, sd], axis=-1)


def _identity_tables(l):
    one = jnp.ones((l, HEAD_W), F32)
    zero = jnp.zeros((l, HEAD_W), F32)
    return jnp.concatenate([one, zero, one, zero, zero], axis=-1)


def kernel(x, c, ctx, c_ctx, w_mod, b_mod, norm1_g, w_in, ret_decay_logit, ret_gn_g, diff_lambda,
           diff_subln_g, w_out, norm2_g, w_up, conv_w, conv_b, w_down, final_g):
    b, l, d = x.shape
    l_ctx = ctx.shape[1]

    rows = -(-(b + 1) // 8) * 8
    cc = jnp.zeros((rows, d), F32).at[:b].set(c).at[b].set(c_ctx)
    m = _modulation(cc, w_mod[0], b_mod[0][None, :])
    sh1, sc1, g1, sh2, sc2, g2 = [m[:, i * d:(i + 1) * d] for i in range(N_MOD)]
    a1 = norm1_g[0][None, :] * (1.0 + sc1)
    a2 = norm2_g[0][None, :] * (1.0 + sc2)
    vec = lambda v: v[:b, None, :]
    ctx_vec = lambda v: jnp.broadcast_to(v[b][None, None, :], (b, 1, d))

    w_in_b = w_in[0].astype(BF16)
    p_lat = _inproj(x, vec(a1), vec(sh1), w_in_b, _rope_tables(l), tm=min(512, l), name="inproj_lat")
    p_ctx = _inproj(ctx, ctx_vec(a1), ctx_vec(sh1), w_in_b, _identity_tables(l_ctx), tm=l_ctx,
                    name="inproj_ctx")

    lg = jax.nn.log_sigmoid(ret_decay_logit[0].astype(F32))
    lgv = jnp.broadcast_to(lg.T[:, :, None], (RET_HEADS, 2, HEAD_W))
    gn_cols = ret_gn_g[0].reshape(2, RET_HEADS, HEAD_W).transpose(1, 0, 2)[..., None]
    y_ret = _retention(p_lat, p_ctx, lgv, gn_cols)
    y_diff = _diff_attention(p_lat, p_ctx, diff_lambda[0], diff_subln_g[0][None, :],
                             tq=min(256, l), tk=256)

    x1, h2 = _outproj(x, y_ret, y_diff, w_out[0].astype(BF16), vec(g1), vec(a2), vec(sh2),
                      tm=min(512, l))
    return _ffn(h2, x1, w_up[0].astype(BF16), conv_w[0], conv_b[0][None, :], w_down[0].astype(BF16),
                vec(g2), final_g[None, :], tm=min(512, l), tn=256)
```

```python
import functools
import math

import jax
import jax.numpy as jnp
from jax import lax
from jax.experimental import pallas as pl
from jax.experimental.pallas import tpu as pltpu

F32 = jnp.float32
BF16 = jnp.bfloat16

GRID_W = 64
RET_HEADS = 4
RET_DK = 128
DIFF_HEADS = 4
DIFF_HD = 64
HEAD_W = 128
GROUP_W = 512
N_GROUPS = 7
RET_CHUNK = 256
ROPE_BASE = 10000.0
EPS = 1e-6
GN_EPS = 1e-5
N_MOD = 6
LAMBDA_INIT = 0.8 - 0.6 * math.exp(-0.3 * 0)
LOG2E = math.log2(math.e)
RET_LOOKAHEAD = 2
ATTN_LOOKAHEAD = 1

VMEM_LIMIT = 56 * 1024 * 1024


def _cparams(*sem):
    return pltpu.CompilerParams(dimension_semantics=sem, vmem_limit_bytes=VMEM_LIMIT)


def _silu(v):
    return v * (1.0 / (1.0 + jnp.exp(-v)))


def _split_dot(a, w):
    a_hi = a.astype(BF16)
    a_lo = (a - a_hi.astype(F32)).astype(BF16)
    w_hi = w.astype(BF16)
    w_lo = (w - w_hi.astype(F32)).astype(BF16)
    dot = functools.partial(jnp.dot, preferred_element_type=F32)
    return dot(a_hi, w_hi) + (dot(a_hi, w_lo) + dot(a_lo, w_hi))


def _mod_kernel(c_ref, w_ref, b_ref, o_ref):
    o_ref[...] = _split_dot(_silu(c_ref[...]), w_ref[...]) + b_ref[...]


def _modulation(cc, w_mod, b_mod):
    rows, d = cc.shape
    n = w_mod.shape[1]
    tn = 1024
    return pl.pallas_call(
        _mod_kernel,
        grid=(n // tn,),
        in_specs=[pl.BlockSpec((rows, d), lambda j: (0, 0)),
                  pl.BlockSpec((d, tn), lambda j: (0, j)),
                  pl.BlockSpec((1, tn), lambda j: (0, j))],
        out_specs=pl.BlockSpec((rows, tn), lambda j: (0, j)),
        out_shape=jax.ShapeDtypeStruct((rows, n), F32),
        compiler_params=_cparams("arbitrary"),
        name="mod",
    )(cc, w_mod, b_mod)


def _rope128(v, cos, sin):
    return v * cos + pltpu.roll(v, 64, axis=1) * sin


def _rope64(v, cos, sin_lo, sin_hi):
    return v * cos + pltpu.roll(v, 96, axis=1) * sin_lo + pltpu.roll(v, 32, axis=1) * sin_hi


def _inproj_kernel(x_ref, a_ref, s_ref, w_ref, t_ref, o_ref):
    x = x_ref[...]
    xn = x * lax.rsqrt(jnp.mean(x * x, axis=-1, keepdims=True) + EPS)
    h = (xn * a_ref[...] + s_ref[...]).astype(BF16)
    cos_r = t_ref[:, 0 * HEAD_W:1 * HEAD_W]
    sin_r = t_ref[:, 1 * HEAD_W:2 * HEAD_W]
    cos_d = t_ref[:, 2 * HEAD_W:3 * HEAD_W]
    sin_dl = t_ref[:, 3 * HEAD_W:4 * HEAD_W]
    sin_dh = t_ref[:, 4 * HEAD_W:5 * HEAD_W]
    for g in range(N_GROUPS):
        acc = jnp.dot(h, w_ref[:, g * GROUP_W:(g + 1) * GROUP_W], preferred_element_type=F32)
        for hd in range(GROUP_W // HEAD_W):
            v = acc[:, hd * HEAD_W:(hd + 1) * HEAD_W]
            if g == 0:
                v = _rope128(v, cos_r, sin_r)
            elif g == 1:
                v = _rope128(v, cos_r, sin_r) * (RET_DK ** -0.5)
            elif g == 4:
                v = _rope64(v, cos_d, sin_dl, sin_dh) * (DIFF_HD ** -0.5 * LOG2E)
            elif g == 5:
                v = _rope64(v, cos_d, sin_dl, sin_dh)
            c0 = g * GROUP_W + hd * HEAD_W
            o_ref[:, c0:c0 + HEAD_W] = v.astype(BF16)


def _inproj(x, a, s, w_in, tables, *, tm, name):
    b, lx, d = x.shape
    n_cols = w_in.shape[1]
    return pl.pallas_call(
        _inproj_kernel,
        grid=(lx // tm, b),
        in_specs=[pl.BlockSpec((None, tm, d), lambda t, i: (i, t, 0)),
                  pl.BlockSpec((None, 1, d), lambda t, i: (i, 0, 0)),
                  pl.BlockSpec((None, 1, d), lambda t, i: (i, 0, 0)),
                  pl.BlockSpec((d, n_cols), lambda t, i: (0, 0)),
                  pl.BlockSpec((tm, 5 * HEAD_W), lambda t, i: (t, 0))],
        out_specs=pl.BlockSpec((None, tm, n_cols), lambda t, i: (i, t, 0)),
        out_shape=jax.ShapeDtypeStruct((b, lx, n_cols), BF16),
        compiler_params=_cparams("arbitrary", "arbitrary"),
        name=name,
    )(x, a, s, w_in, tables)


def _ret_kernel(q_ref, k_ref, v_ref, g_ref, kc_ref, vc_ref, lg_ref, gn_ref, o_ref, u_ref, s_ref,
                *, n_lat, n_all):
    c = RET_CHUNK
    dk = HEAD_W
    row = lax.broadcasted_iota(jnp.int32, (c, c), 0).astype(F32)
    col = lax.broadcasted_iota(jnp.int32, (c, c), 1).astype(F32)
    tok = lax.broadcasted_iota(jnp.int32, (c, dk), 0).astype(F32)
    lg_f = lg_ref[0:1, :]
    lg_b = lg_ref[1:2, :]
    lg_f2 = jnp.concatenate([lg_f, lg_f], axis=1)
    lg_b2 = jnp.concatenate([lg_b, lg_b], axis=1)
    mask_f = jnp.where(col >= row, jnp.exp(jnp.maximum(col - row, 0.0) * lg_f2), 0.0)
    mask_b = jnp.where(row >= col, jnp.exp(jnp.maximum(row - col, 0.0) * lg_b2), 0.0)
    qdec_f = jnp.exp((tok + 1.0) * lg_f)
    qdec_b = jnp.exp((c - tok) * lg_b)
    kdec_f = jnp.exp((c - 1.0 - tok) * lg_f)
    kdec_b = jnp.exp(tok * lg_b)
    cdec_f = jnp.exp(c * lg_f)
    cdec_b = jnp.exp(c * lg_b)
    tn_dims = (((0,), (0,)), ((), ()))
    nt_dims = (((1,), (1,)), ((), ()))

    def rows(idx):
        return slice(idx * c, (idx + 1) * c)

    def kv_chunk(idx):
        if idx < n_lat:
            return k_ref[rows(idx), :], v_ref[rows(idx), :]
        return kc_ref[rows(idx - n_lat), :], vc_ref[rows(idx - n_lat), :]

    for idx in range(n_all):
        k, v = kv_chunk(idx)
        kf = k.astype(F32)
        kd = jnp.concatenate([(kf * kdec_f).astype(BF16), (kf * kdec_b).astype(BF16)], axis=1)
        u_ref[idx] = lax.dot_general(v, kd, tn_dims, preferred_element_type=F32)

    state = jnp.zeros((dk, dk), F32)
    for idx in [*range(n_lat, n_all), *range(n_lat)]:
        s_ref[idx, :, 0:dk] = state.astype(BF16)
        state = state * cdec_f + u_ref[idx, :, 0:dk]
    state = jnp.zeros((dk, dk), F32)
    for idx in reversed(range(n_all)):
        s_ref[idx, :, dk:2 * dk] = state.astype(BF16)
        state = state * cdec_b + u_ref[idx, :, dk:2 * dk]

    gn_f = jnp.broadcast_to(gn_ref[0], (dk, c))
    gn_b = jnp.broadcast_to(gn_ref[1], (dk, c))

    def group_norm_t(o, g):
        mu = jnp.mean(o, axis=0, keepdims=True)
        d = o - mu
        var = jnp.mean(d * d, axis=0, keepdims=True)
        return d * lax.rsqrt(var + GN_EPS) * g

    def scores_t(idx):
        return lax.dot_general(k_ref[rows(idx), :], q_ref[rows(idx), :], nt_dims,
                               preferred_element_type=F32)

    ahead = [scores_t(i) for i in range(min(RET_LOOKAHEAD, n_lat))]
    for idx in range(n_lat):
        a_t = ahead.pop(0)
        if idx + RET_LOOKAHEAD < n_lat:
            ahead.append(scores_t(idx + RET_LOOKAHEAD))
        v = v_ref[rows(idx), :]
        qf = q_ref[rows(idx), :].astype(F32)
        o_f = (lax.dot_general(v, (a_t * mask_f).astype(BF16), tn_dims, preferred_element_type=F32)
               + lax.dot_general(s_ref[idx, :, 0:dk], (qf * qdec_f).astype(BF16), nt_dims,
                                 preferred_element_type=F32))
        o_b = (lax.dot_general(v, (a_t * mask_b).astype(BF16), tn_dims, preferred_element_type=F32)
               + lax.dot_general(s_ref[idx, :, dk:2 * dk], (qf * qdec_b).astype(BF16), nt_dims,
                                 preferred_element_type=F32))
        n_t = group_norm_t(o_f, gn_f) + group_norm_t(o_b, gn_b)
        y = _silu(g_ref[rows(idx), :].astype(F32)) * n_t.T
        o_ref[rows(idx), :] = y.astype(BF16)


def _retention(p_lat, p_ctx, lgv, gn_cols):
    b, l_lat, _ = p_lat.shape
    l_ctx = p_ctx.shape[1]
    n_lat = l_lat // RET_CHUNK
    n_all = n_lat + l_ctx // RET_CHUNK
    col = lambda rows, g: pl.BlockSpec((None, rows, HEAD_W), lambda i, h: (i, 0, g * RET_HEADS + h))
    kern = functools.partial(_ret_kernel, n_lat=n_lat, n_all=n_all)
    return pl.pallas_call(
        kern,
        grid=(b, RET_HEADS),
        in_specs=[col(l_lat, 0), col(l_lat, 1), col(l_lat, 2), col(l_lat, 3),
                  col(l_ctx, 1), col(l_ctx, 2),
                  pl.BlockSpec((None, 2, HEAD_W), lambda i, h: (h, 0, 0)),
                  pl.BlockSpec((None, 2, HEAD_W, 1), lambda i, h: (h, 0, 0, 0))],
        out_specs=pl.BlockSpec((None, l_lat, HEAD_W), lambda i, h: (i, 0, h)),
        out_shape=jax.ShapeDtypeStruct((b, l_lat, RET_HEADS * HEAD_W), BF16),
        scratch_shapes=[pltpu.VMEM((n_all, HEAD_W, 2 * HEAD_W), F32),
                        pltpu.VMEM((n_all, HEAD_W, 2 * HEAD_W), BF16)],
        compiler_params=_cparams("arbitrary", "arbitrary"),
        name="ret",
    )(p_lat, p_lat, p_lat, p_lat, p_ctx, p_ctx, lgv, gn_cols)


ONES_ROWS = 16


def _attn_kernel(q_ref, k_ref, v_ref, kc_ref, vc_ref, lam_ref, g_ref, o_ref, vt_ref,
                 *, tq, tk, l_lat, l_ctx):
    dv = HEAD_W
    l_all = l_lat + l_ctx
    spans = ([(False, o, min(tk, l_lat - o)) for o in range(0, l_lat, tk)]
             + [(True, o, min(tk, l_ctx - o)) for o in range(0, l_ctx, tk)])
    n_k = len(spans)

    @pl.when(pl.program_id(2) == 0)
    def _():
        for o in range(0, l_lat, RET_CHUNK):
            vt_ref[0:dv, o:o + RET_CHUNK] = v_ref[o:o + RET_CHUNK, :].astype(F32).T.astype(BF16)
        for o in range(0, l_ctx, RET_CHUNK):
            vt_ref[0:dv, l_lat + o:l_lat + o + RET_CHUNK] = (
                vc_ref[o:o + RET_CHUNK, :].astype(F32).T.astype(BF16))
        vt_ref[dv:dv + ONES_ROWS, :] = jnp.ones((ONES_ROWS, l_all), BF16)

    lane = lax.broadcasted_iota(jnp.int32, (tq, HEAD_W), 1)
    q = q_ref[...]
    zero = jnp.zeros_like(q)
    q1 = jnp.where(lane < DIFF_HD, q, zero)
    q2 = jnp.where(lane >= DIFF_HD, q, zero)
    nt_dims = (((1,), (1,)), ((), ()))

    def scores(i):
        is_ctx, o, size = spans[i]
        kt = (kc_ref if is_ctx else k_ref)[o:o + size, :]
        return (lax.dot_general(kt, q1, nt_dims, preferred_element_type=F32),
                lax.dot_general(kt, q2, nt_dims, preferred_element_type=F32))

    def update(s, vt, state):
        m, acc = state
        m_new = jnp.maximum(m, jnp.max(s, axis=0, keepdims=True))
        alpha = jnp.exp2(m - m_new)
        pr = jnp.exp2(s - m_new).astype(BF16)
        return m_new, alpha * acc + jnp.dot(vt, pr, preferred_element_type=F32)

    init = (jnp.full((1, tq), -jnp.inf, F32), jnp.zeros((dv + ONES_ROWS, tq), F32))
    st1 = st2 = init
    ahead = [scores(i) for i in range(min(ATTN_LOOKAHEAD, n_k))]
    for i in range(n_k):
        s1, s2 = ahead.pop(0)
        if i + ATTN_LOOKAHEAD < n_k:
            ahead.append(scores(i + ATTN_LOOKAHEAD))
        is_ctx, o, size = spans[i]
        start = (l_lat if is_ctx else 0) + o
        vt = vt_ref[:, start:start + size]
        st1 = update(s1, vt, st1)
        st2 = update(s2, vt, st2)
    (_, acc1), (_, acc2) = st1, st2

    lp = lam_ref[...]
    lam = (jnp.exp(jnp.sum(lp[0:1, :] * lp[1:2, :], axis=-1, keepdims=True))
           - jnp.exp(jnp.sum(lp[2:3, :] * lp[3:4, :], axis=-1, keepdims=True)) + LAMBDA_INIT)
    o = acc1[0:dv] / acc1[dv:dv + 1] - lam * (acc2[0:dv] / acc2[dv:dv + 1])
    n = o * lax.rsqrt(jnp.mean(o * o, axis=0, keepdims=True) + EPS)
    o_ref[...] = (n.T * g_ref[...] * (1.0 - LAMBDA_INIT)).astype(BF16)


def _diff_attention(p_lat, p_ctx, lam_par, subln_g, *, tq, tk):
    b, l_lat, _ = p_lat.shape
    l_ctx = p_ctx.shape[1]
    kern = functools.partial(_attn_kernel, tq=tq, tk=tk, l_lat=l_lat, l_ctx=l_ctx)
    col = lambda rows, g: pl.BlockSpec((None, rows, HEAD_W), lambda i, h, j: (i, 0, g * DIFF_HEADS + h))
    return pl.pallas_call(
        kern,
        grid=(b, DIFF_HEADS, l_lat // tq),
        in_specs=[pl.BlockSpec((None, tq, HEAD_W), lambda i, h, j: (i, j, 4 * DIFF_HEADS + h)),
                  col(l_lat, 5), col(l_lat, 6), col(l_ctx, 5), col(l_ctx, 6),
                  pl.BlockSpec((4, DIFF_HD), lambda i, h, j: (0, 0)),
                  pl.BlockSpec((1, HEAD_W), lambda i, h, j: (0, h))],
        out_specs=pl.BlockSpec((None, tq, HEAD_W), lambda i, h, j: (i, j, h)),
        out_shape=jax.ShapeDtypeStruct((b, l_lat, DIFF_HEADS * HEAD_W), BF16),
        scratch_shapes=[pltpu.VMEM((HEAD_W + ONES_ROWS, l_lat + l_ctx), BF16)],
        compiler_params=_cparams("arbitrary", "arbitrary", "arbitrary"),
        name="attn",
    )(p_lat, p_lat, p_lat, p_ctx, p_ctx, lam_par, subln_g)


def _outproj_kernel(x_ref, yr_ref, yd_ref, w_ref, g1_ref, a2_ref, s2_ref, x1_ref, h2_ref):
    half = yr_ref.shape[-1]
    y = (jnp.dot(yr_ref[...], w_ref[:half, :], preferred_element_type=F32)
         + jnp.dot(yd_ref[...], w_ref[half:, :], preferred_element_type=F32))
    x1 = x_ref[...] + g1_ref[...] * y
    x1_ref[...] = x1
    xn = x1 * lax.rsqrt(jnp.mean(x1 * x1, axis=-1, keepdims=True) + EPS)
    h2_ref[...] = (xn * a2_ref[...] + s2_ref[...]).astype(BF16)


def _outproj(x, y_ret, y_diff, w_out, g1, a2, s2, *, tm):
    b, l, d = x.shape
    half = y_ret.shape[-1]
    tile = lambda w: pl.BlockSpec((None, tm, w), lambda i, t: (i, t, 0))
    vec = pl.BlockSpec((None, 1, d), lambda i, t: (i, 0, 0))
    return pl.pallas_call(
        _outproj_kernel,
        grid=(b, l // tm),
        in_specs=[tile(d), tile(half), tile(half),
                  pl.BlockSpec((2 * half, d), lambda i, t: (0, 0)), vec, vec, vec],
        out_specs=[tile(d), tile(d)],
        out_shape=[jax.ShapeDtypeStruct((b, l, d), F32), jax.ShapeDtypeStruct((b, l, d), BF16)],
        compiler_params=_cparams("arbitrary", "arbitrary"),
        name="outproj",
    )(x, y_ret, y_diff, w_out, g1, a2, s2)


HALO = 16


def _ffn_kernel(hp_ref, h_ref, hn_ref, x1_ref, wu_ref, cw_ref, cb_ref, wd_ref, g2_ref, fg_ref,
                o_ref, u_ref, act_ref, *, tm, d_ff, tn):
    t = pl.program_id(1)
    nt = pl.num_programs(1)
    hp = hp_ref[...]
    hn = hn_ref[...]
    hp = jnp.where(t > 0, hp, jnp.zeros_like(hp))
    hn = jnp.where(t < nt - 1, hn, jnp.zeros_like(hn))
    hh = jnp.concatenate([hp, h_ref[...], hn], axis=0)
    for j in range(d_ff // tn):
        for part in range(2):
            c0 = part * d_ff + j * tn
            u_ref[...] = jnp.dot(hh, wu_ref[:, c0:c0 + tn], preferred_element_type=F32)
            w = cw_ref[:, c0:c0 + tn]
            conv = (u_ref[pl.ds(HALO - 1, tm), :] * w[0:1, :]
                    + u_ref[pl.ds(HALO, tm), :] * w[1:2, :]
                    + u_ref[pl.ds(HALO + 1, tm), :] * w[2:3, :]
                    + cb_ref[:, c0:c0 + tn])
            if part == 0:
                gate = _silu(conv)
            else:
                act_ref[:, j * tn:(j + 1) * tn] = (gate * conv).astype(BF16)
    f = jnp.dot(act_ref[...], wd_ref[...], preferred_element_type=F32)
    x2 = x1_ref[...] + g2_ref[...] * f
    o_ref[...] = x2 * lax.rsqrt(jnp.mean(x2 * x2, axis=-1, keepdims=True) + EPS) * fg_ref[...]


def _ffn(h2, x1, w_up, conv_w, conv_b, w_down, g2, final_g, *, tm, tn):
    b, l, d = x1.shape
    d_ff = w_down.shape[0]
    nh = tm // HALO
    n_halo_blocks = l // HALO
    kern = functools.partial(_ffn_kernel, tm=tm, d_ff=d_ff, tn=tn)
    tile = pl.BlockSpec((None, tm, d), lambda i, t: (i, t, 0))
    prev = pl.BlockSpec((None, HALO, d), lambda i, t: (i, jnp.maximum(t * nh - 1, 0), 0))
    nxt = pl.BlockSpec((None, HALO, d), lambda i, t: (i, jnp.minimum((t + 1) * nh, n_halo_blocks - 1), 0))
    full = lambda a: pl.BlockSpec(a.shape, lambda i, t: (0,) * a.ndim)
    return pl.pallas_call(
        kern,
        grid=(b, l // tm),
        in_specs=[prev, tile, nxt, tile, full(w_up), full(conv_w), full(conv_b), full(w_down),
                  pl.BlockSpec((None, 1, d), lambda i, t: (i, 0, 0)), full(final_g)],
        out_specs=tile,
        out_shape=jax.ShapeDtypeStruct((b, l, d), F32),
        scratch_shapes=[pltpu.VMEM((tm + 2 * HALO, tn), F32), pltpu.VMEM((tm, d_ff), BF16)],
        compiler_params=_cparams("arbitrary", "arbitrary"),
        name="ffn",
    )(h2, h2, h2, x1, w_up, conv_w, conv_b, w_down, g2, final_g)


def _rope_tables(l):
    pos = jnp.arange(l)
    row = (pos // GRID_W).astype(F32)[:, None]
    col = (pos % GRID_W).astype(F32)[:, None]

    def angles(head_dim):
        n_freq = head_dim // 4
        inv = ROPE_BASE ** (-jnp.arange(n_freq, dtype=F32) / n_freq)
        return jnp.concatenate([row * inv, col * inv], axis=-1)

    ar = angles(RET_DK)
    ad = angles(DIFF_HD)
    cr, sr = jnp.cos(ar), jnp.sin(ar)
    cd, sd = jnp.cos(ad), jnp.sin(ad)
    zd = jnp.zeros_like(sd)
    return jnp.concatenate([
        cr, cr, -sr, sr,
        cd, cd, cd, cd,
        -sd, zd, -sd, zd,
        zd, sd, zd, sd], axis=-1)


def _identity_tables(l):
    one = jnp.ones((l, HEAD_W), F32)
    zero = jnp.zeros((l, HEAD_W), F32)
    return jnp.concatenate([one, zero, one, zero, zero], axis=-1)


def kernel(x, c, ctx, c_ctx, w_mod, b_mod, norm1_g, w_in, ret_decay_logit, ret_gn_g, diff_lambda,
           diff_subln_g, w_out, norm2_g, w_up, conv_w, conv_b, w_down, final_g):
    b, l, d = x.shape
    l_ctx = ctx.shape[1]

    rows = -(-(b + 1) // 8) * 8
    cc = jnp.zeros((rows, d), F32).at[:b].set(c).at[b].set(c_ctx)
    m = _modulation(cc, w_mod[0], b_mod[0][None, :])
    sh1, sc1, g1, sh2, sc2, g2 = [m[:, i * d:(i + 1) * d] for i in range(N_MOD)]
    a1 = norm1_g[0][None, :] * (1.0 + sc1)
    a2 = norm2_g[0][None, :] * (1.0 + sc2)
    vec = lambda v: v[:b, None, :]
    ctx_vec = lambda v: jnp.broadcast_to(v[b][None, None, :], (b, 1, d))

    w_in_b = w_in[0].astype(BF16)
    p_lat = _inproj(x, vec(a1), vec(sh1), w_in_b, _rope_tables(l), tm=min(512, l), name="inproj_lat")
    p_ctx = _inproj(ctx, ctx_vec(a1), ctx_vec(sh1), w_in_b, _identity_tables(l_ctx), tm=l_ctx,
                    name="inproj_ctx")

    lg = jax.nn.log_sigmoid(ret_decay_logit[0].astype(F32))
    lgv = jnp.broadcast_to(lg.T[:, :, None], (RET_HEADS, 2, HEAD_W))
    gn_cols = ret_gn_g[0].reshape(2, RET_HEADS, HEAD_W).transpose(1, 0, 2)[..., None]
    y_ret = _retention(p_lat, p_ctx, lgv, gn_cols)
    y_diff = _diff_attention(p_lat, p_ctx, diff_lambda[0], diff_subln_g[0][None, :],
                             tq=min(1024, l), tk=512)

    x1, h2 = _outproj(x, y_ret, y_diff, w_out[0].astype(BF16), vec(g1), vec(a2), vec(sh2),
                      tm=min(512, l))
    return _ffn(h2, x1, w_up[0].astype(BF16), conv_w[0], conv_b[0][None, :], w_down[0].astype(BF16),
                vec(g2), final_g[None, :], tm=min(512, l), tn=256)
```

```python
import functools
import math

import jax
import jax.numpy as jnp
import numpy as np
from jax import lax
from jax.experimental import pallas as pl
from jax.experimental.pallas import tpu as pltpu

F32 = jnp.float32
BF16 = jnp.bfloat16

GRID_W = 64
RET_HEADS = 4
RET_DK = 128
DIFF_HEADS = 4
DIFF_HD = 64
HEAD_W = 128
GROUP_W = 512
N_GROUPS = 7
RET_CHUNK = 256
ROPE_BASE = 10000.0
EPS = 1e-6
GN_EPS = 1e-5
N_MOD = 6
LAMBDA_INIT = 0.8 - 0.6 * math.exp(-0.3 * 0)
LOG2E = math.log2(math.e)
RET_LOOKAHEAD = 2
ATTN_LOOKAHEAD = 1

VMEM_LIMIT = 56 * 1024 * 1024


def _cparams(*sem):
    return pltpu.CompilerParams(dimension_semantics=sem, vmem_limit_bytes=VMEM_LIMIT)


def _silu(v):
    return v * (1.0 / (1.0 + jnp.exp(-v)))


def _split_dot(a, w):
    a_hi = a.astype(BF16)
    a_lo = (a - a_hi.astype(F32)).astype(BF16)
    w_hi = w.astype(BF16)
    w_lo = (w - w_hi.astype(F32)).astype(BF16)
    dot = functools.partial(jnp.dot, preferred_element_type=F32)
    return dot(a_hi, w_hi) + (dot(a_hi, w_lo) + dot(a_lo, w_hi))


def _mod_kernel(c_ref, w_ref, b_ref, o_ref):
    o_ref[...] = _split_dot(_silu(c_ref[...]), w_ref[...]) + b_ref[...]


def _modulation(cc, w_mod, b_mod):
    rows, d = cc.shape
    n = w_mod.shape[1]
    tn = 1024
    return pl.pallas_call(
        _mod_kernel,
        grid=(n // tn,),
        in_specs=[pl.BlockSpec((rows, d), lambda j: (0, 0)),
                  pl.BlockSpec((d, tn), lambda j: (0, j)),
                  pl.BlockSpec((1, tn), lambda j: (0, j))],
        out_specs=pl.BlockSpec((rows, tn), lambda j: (0, j)),
        out_shape=jax.ShapeDtypeStruct((rows, n), F32),
        compiler_params=_cparams("arbitrary"),
        name="mod",
    )(cc, w_mod, b_mod)


def _rope128(v, cos, sin):
    return v * cos + pltpu.roll(v, 64, axis=1) * sin


def _rope64(v, cos, sin_lo, sin_hi):
    return v * cos + pltpu.roll(v, 96, axis=1) * sin_lo + pltpu.roll(v, 32, axis=1) * sin_hi


def _inproj_kernel(x_ref, a_ref, s_ref, w_ref, t_ref, o_ref):
    x = x_ref[...]
    xn = x * lax.rsqrt(jnp.mean(x * x, axis=-1, keepdims=True) + EPS)
    h = (xn * a_ref[...] + s_ref[...]).astype(BF16)
    cos_r = t_ref[:, 0 * HEAD_W:1 * HEAD_W]
    sin_r = t_ref[:, 1 * HEAD_W:2 * HEAD_W]
    cos_d = t_ref[:, 2 * HEAD_W:3 * HEAD_W]
    sin_dl = t_ref[:, 3 * HEAD_W:4 * HEAD_W]
    sin_dh = t_ref[:, 4 * HEAD_W:5 * HEAD_W]
    for g in range(N_GROUPS):
        acc = jnp.dot(h, w_ref[:, g * GROUP_W:(g + 1) * GROUP_W], preferred_element_type=F32)
        for hd in range(GROUP_W // HEAD_W):
            v = acc[:, hd * HEAD_W:(hd + 1) * HEAD_W]
            if g == 0:
                v = _rope128(v, cos_r, sin_r)
            elif g == 1:
                v = _rope128(v, cos_r, sin_r) * (RET_DK ** -0.5)
            elif g == 4:
                v = _rope64(v, cos_d, sin_dl, sin_dh) * (DIFF_HD ** -0.5 * LOG2E)
            elif g == 5:
                v = _rope64(v, cos_d, sin_dl, sin_dh)
            c0 = g * GROUP_W + hd * HEAD_W
            o_ref[:, c0:c0 + HEAD_W] = v.astype(BF16)


def _inproj(x, a, s, w_in, tables, *, tm, name):
    b, lx, d = x.shape
    n_cols = w_in.shape[1]
    return pl.pallas_call(
        _inproj_kernel,
        grid=(lx // tm, b),
        in_specs=[pl.BlockSpec((None, tm, d), lambda t, i: (i, t, 0)),
                  pl.BlockSpec((None, 1, d), lambda t, i: (i, 0, 0)),
                  pl.BlockSpec((None, 1, d), lambda t, i: (i, 0, 0)),
                  pl.BlockSpec((d, n_cols), lambda t, i: (0, 0)),
                  pl.BlockSpec((tm, 5 * HEAD_W), lambda t, i: (t, 0))],
        out_specs=pl.BlockSpec((None, tm, n_cols), lambda t, i: (i, t, 0)),
        out_shape=jax.ShapeDtypeStruct((b, lx, n_cols), BF16),
        compiler_params=_cparams("arbitrary", "arbitrary"),
        name=name,
    )(x, a, s, w_in, tables)


def _ret_kernel(q_ref, k_ref, v_ref, g_ref, kc_ref, vc_ref, lg_ref, gn_ref, o_ref, u_ref, s_ref,
                *, n_lat, n_all):
    c = RET_CHUNK
    dk = HEAD_W
    row = lax.broadcasted_iota(jnp.int32, (c, c), 0).astype(F32)
    col = lax.broadcasted_iota(jnp.int32, (c, c), 1).astype(F32)
    tok = lax.broadcasted_iota(jnp.int32, (c, dk), 0).astype(F32)
    lg_f = lg_ref[0:1, :]
    lg_b = lg_ref[1:2, :]
    lg_f2 = jnp.concatenate([lg_f, lg_f], axis=1)
    lg_b2 = jnp.concatenate([lg_b, lg_b], axis=1)
    mask_f = jnp.where(col >= row, jnp.exp(jnp.maximum(col - row, 0.0) * lg_f2), 0.0)
    mask_b = jnp.where(row >= col, jnp.exp(jnp.maximum(row - col, 0.0) * lg_b2), 0.0)
    qdec_f = jnp.exp((tok + 1.0) * lg_f)
    qdec_b = jnp.exp((c - tok) * lg_b)
    kdec_f = jnp.exp((c - 1.0 - tok) * lg_f)
    kdec_b = jnp.exp(tok * lg_b)
    cdec_f = jnp.exp(c * lg_f)
    cdec_b = jnp.exp(c * lg_b)
    tn_dims = (((0,), (0,)), ((), ()))
    nt_dims = (((1,), (1,)), ((), ()))

    def rows(idx):
        return slice(idx * c, (idx + 1) * c)

    def kv_chunk(idx):
        if idx < n_lat:
            return k_ref[rows(idx), :], v_ref[rows(idx), :]
        return kc_ref[rows(idx - n_lat), :], vc_ref[rows(idx - n_lat), :]

    for idx in range(n_all):
        k, v = kv_chunk(idx)
        kf = k.astype(F32)
        kd = jnp.concatenate([(kf * kdec_f).astype(BF16), (kf * kdec_b).astype(BF16)], axis=1)
        u_ref[idx] = lax.dot_general(v, kd, tn_dims, preferred_element_type=F32)

    state = jnp.zeros((dk, dk), F32)
    for idx in [*range(n_lat, n_all), *range(n_lat)]:
        s_ref[idx, :, 0:dk] = state.astype(BF16)
        state = state * cdec_f + u_ref[idx, :, 0:dk]
    state = jnp.zeros((dk, dk), F32)
    for idx in reversed(range(n_all)):
        s_ref[idx, :, dk:2 * dk] = state.astype(BF16)
        state = state * cdec_b + u_ref[idx, :, dk:2 * dk]

    gn_f = jnp.broadcast_to(gn_ref[0], (dk, c))
    gn_b = jnp.broadcast_to(gn_ref[1], (dk, c))

    def group_norm_t(o, g):
        mu = jnp.mean(o, axis=0, keepdims=True)
        d = o - mu
        var = jnp.mean(d * d, axis=0, keepdims=True)
        return d * lax.rsqrt(var + GN_EPS) * g

    def scores_t(idx):
        return lax.dot_general(k_ref[rows(idx), :], q_ref[rows(idx), :], nt_dims,
                               preferred_element_type=F32)

    ahead = [scores_t(i) for i in range(min(RET_LOOKAHEAD, n_lat))]
    for idx in range(n_lat):
        a_t = ahead.pop(0)
        if idx + RET_LOOKAHEAD < n_lat:
            ahead.append(scores_t(idx + RET_LOOKAHEAD))
        v = v_ref[rows(idx), :]
        qf = q_ref[rows(idx), :].astype(F32)
        o_f = (lax.dot_general(v, (a_t * mask_f).astype(BF16), tn_dims, preferred_element_type=F32)
               + lax.dot_general(s_ref[idx, :, 0:dk], (qf * qdec_f).astype(BF16), nt_dims,
                                 preferred_element_type=F32))
        o_b = (lax.dot_general(v, (a_t * mask_b).astype(BF16), tn_dims, preferred_element_type=F32)
               + lax.dot_general(s_ref[idx, :, dk:2 * dk], (qf * qdec_b).astype(BF16), nt_dims,
                                 preferred_element_type=F32))
        n_t = group_norm_t(o_f, gn_f) + group_norm_t(o_b, gn_b)
        y = _silu(g_ref[rows(idx), :].astype(F32)) * n_t.T
        o_ref[rows(idx), :] = y.astype(BF16)


def _retention(p_lat, p_ctx, lgv, gn_cols):
    b, l_lat, _ = p_lat.shape
    l_ctx = p_ctx.shape[1]
    n_lat = l_lat // RET_CHUNK
    n_all = n_lat + l_ctx // RET_CHUNK
    col = lambda rows, g: pl.BlockSpec((None, rows, HEAD_W), lambda i, h: (i, 0, g * RET_HEADS + h))
    kern = functools.partial(_ret_kernel, n_lat=n_lat, n_all=n_all)
    return pl.pallas_call(
        kern,
        grid=(b, RET_HEADS),
        in_specs=[col(l_lat, 0), col(l_lat, 1), col(l_lat, 2), col(l_lat, 3),
                  col(l_ctx, 1), col(l_ctx, 2),
                  pl.BlockSpec((None, 2, HEAD_W), lambda i, h: (h, 0, 0)),
                  pl.BlockSpec((None, 2, HEAD_W, 1), lambda i, h: (h, 0, 0, 0))],
        out_specs=pl.BlockSpec((None, l_lat, HEAD_W), lambda i, h: (i, 0, h)),
        out_shape=jax.ShapeDtypeStruct((b, l_lat, RET_HEADS * HEAD_W), BF16),
        scratch_shapes=[pltpu.VMEM((n_all, HEAD_W, 2 * HEAD_W), F32),
                        pltpu.VMEM((n_all, HEAD_W, 2 * HEAD_W), BF16)],
        compiler_params=_cparams("arbitrary", "arbitrary"),
        name="ret",
    )(p_lat, p_lat, p_lat, p_lat, p_ctx, p_ctx, lgv, gn_cols)


ONES_ROWS = 16


def _attn_kernel(q_ref, k_ref, v_ref, kc_ref, vc_ref, lam_ref, g_ref, o_ref, vt_ref,
                 *, tq, tk, l_lat, l_ctx):
    dv = HEAD_W
    l_all = l_lat + l_ctx
    spans = ([(False, o, min(tk, l_lat - o)) for o in range(0, l_lat, tk)]
             + [(True, o, min(tk, l_ctx - o)) for o in range(0, l_ctx, tk)])
    n_k = len(spans)

    @pl.when(pl.program_id(2) == 0)
    def _():
        for o in range(0, l_lat, RET_CHUNK):
            vt_ref[0:dv, o:o + RET_CHUNK] = v_ref[o:o + RET_CHUNK, :].astype(F32).T.astype(BF16)
        for o in range(0, l_ctx, RET_CHUNK):
            vt_ref[0:dv, l_lat + o:l_lat + o + RET_CHUNK] = (
                vc_ref[o:o + RET_CHUNK, :].astype(F32).T.astype(BF16))
        vt_ref[dv:dv + ONES_ROWS, :] = jnp.ones((ONES_ROWS, l_all), BF16)

    lane = lax.broadcasted_iota(jnp.int32, (tq, HEAD_W), 1)
    q = q_ref[...]
    zero = jnp.zeros_like(q)
    q1 = jnp.where(lane < DIFF_HD, q, zero)
    q2 = jnp.where(lane >= DIFF_HD, q, zero)
    nt_dims = (((1,), (1,)), ((), ()))

    def scores(i):
        is_ctx, o, size = spans[i]
        kt = (kc_ref if is_ctx else k_ref)[o:o + size, :]
        return (lax.dot_general(kt, q1, nt_dims, preferred_element_type=F32),
                lax.dot_general(kt, q2, nt_dims, preferred_element_type=F32))

    def update(s, vt, state):
        m, acc = state
        m_new = jnp.maximum(m, jnp.max(s, axis=0, keepdims=True))
        alpha = jnp.exp2(m - m_new)
        pr = jnp.exp2(s - m_new).astype(BF16)
        return m_new, alpha * acc + jnp.dot(vt, pr, preferred_element_type=F32)

    init = (jnp.full((1, tq), -jnp.inf, F32), jnp.zeros((dv + ONES_ROWS, tq), F32))
    st1 = st2 = init
    ahead = [scores(i) for i in range(min(ATTN_LOOKAHEAD, n_k))]
    for i in range(n_k):
        s1, s2 = ahead.pop(0)
        if i + ATTN_LOOKAHEAD < n_k:
            ahead.append(scores(i + ATTN_LOOKAHEAD))
        is_ctx, o, size = spans[i]
        start = (l_lat if is_ctx else 0) + o
        vt = vt_ref[:, start:start + size]
        st1 = update(s1, vt, st1)
        st2 = update(s2, vt, st2)
    (_, acc1), (_, acc2) = st1, st2

    lp = lam_ref[...]
    lam = (jnp.exp(jnp.sum(lp[0:1, :] * lp[1:2, :], axis=-1, keepdims=True))
           - jnp.exp(jnp.sum(lp[2:3, :] * lp[3:4, :], axis=-1, keepdims=True)) + LAMBDA_INIT)
    o = acc1[0:dv] / acc1[dv:dv + 1] - lam * (acc2[0:dv] / acc2[dv:dv + 1])
    n = o * lax.rsqrt(jnp.mean(o * o, axis=0, keepdims=True) + EPS)
    o_ref[...] = (n.T * g_ref[...] * (1.0 - LAMBDA_INIT)).astype(BF16)


def _diff_attention(p_lat, p_ctx, lam_par, subln_g, *, tq, tk):
    b, l_lat, _ = p_lat.shape
    l_ctx = p_ctx.shape[1]
    kern = functools.partial(_attn_kernel, tq=tq, tk=tk, l_lat=l_lat, l_ctx=l_ctx)
    col = lambda rows, g: pl.BlockSpec((None, rows, HEAD_W), lambda i, h, j: (i, 0, g * DIFF_HEADS + h))
    return pl.pallas_call(
        kern,
        grid=(b, DIFF_HEADS, l_lat // tq),
        in_specs=[pl.BlockSpec((None, tq, HEAD_W), lambda i, h, j: (i, j, 4 * DIFF_HEADS + h)),
                  col(l_lat, 5), col(l_lat, 6), col(l_ctx, 5), col(l_ctx, 6),
                  pl.BlockSpec((4, DIFF_HD), lambda i, h, j: (0, 0)),
                  pl.BlockSpec((1, HEAD_W), lambda i, h, j: (0, h))],
        out_specs=pl.BlockSpec((None, tq, HEAD_W), lambda i, h, j: (i, j, h)),
        out_shape=jax.ShapeDtypeStruct((b, l_lat, DIFF_HEADS * HEAD_W), BF16),
        scratch_shapes=[pltpu.VMEM((HEAD_W + ONES_ROWS, l_lat + l_ctx), BF16)],
        compiler_params=_cparams("arbitrary", "arbitrary", "arbitrary"),
        name="attn",
    )(p_lat, p_lat, p_lat, p_ctx, p_ctx, lam_par, subln_g)


def _outproj_kernel(x_ref, yr_ref, yd_ref, w_ref, g1_ref, a2_ref, s2_ref, x1_ref, h2_ref):
    half = yr_ref.shape[-1]
    y = (jnp.dot(yr_ref[...], w_ref[:half, :], preferred_element_type=F32)
         + jnp.dot(yd_ref[...], w_ref[half:, :], preferred_element_type=F32))
    x1 = x_ref[...] + g1_ref[...] * y
    x1_ref[...] = x1
    xn = x1 * lax.rsqrt(jnp.mean(x1 * x1, axis=-1, keepdims=True) + EPS)
    h2_ref[...] = (xn * a2_ref[...] + s2_ref[...]).astype(BF16)


def _outproj(x, y_ret, y_diff, w_out, g1, a2, s2, *, tm):
    b, l, d = x.shape
    half = y_ret.shape[-1]
    tile = lambda w: pl.BlockSpec((None, tm, w), lambda i, t: (i, t, 0))
    vec = pl.BlockSpec((None, 1, d), lambda i, t: (i, 0, 0))
    return pl.pallas_call(
        _outproj_kernel,
        grid=(b, l // tm),
        in_specs=[tile(d), tile(half), tile(half),
                  pl.BlockSpec((2 * half, d), lambda i, t: (0, 0)), vec, vec, vec],
        out_specs=[tile(d), tile(d)],
        out_shape=[jax.ShapeDtypeStruct((b, l, d), F32), jax.ShapeDtypeStruct((b, l, d), BF16)],
        compiler_params=_cparams("arbitrary", "arbitrary"),
        name="outproj",
    )(x, y_ret, y_diff, w_out, g1, a2, s2)


HALO = 16


def _ffn_kernel(hp_ref, h_ref, hn_ref, x1_ref, wu_ref, cw_ref, cb_ref, wd_ref, g2_ref, fg_ref,
                o_ref, u_ref, act_ref, *, tm, d_ff, tn):
    t = pl.program_id(1)
    nt = pl.num_programs(1)
    hp = hp_ref[...]
    hn = hn_ref[...]
    hp = jnp.where(t > 0, hp, jnp.zeros_like(hp))
    hn = jnp.where(t < nt - 1, hn, jnp.zeros_like(hn))
    hh = jnp.concatenate([hp, h_ref[...], hn], axis=0)
    for j in range(d_ff // tn):
        for part in range(2):
            c0 = part * d_ff + j * tn
            u_ref[...] = jnp.dot(hh, wu_ref[:, c0:c0 + tn], preferred_element_type=F32)
            w = cw_ref[:, c0:c0 + tn]
            conv = (u_ref[pl.ds(HALO - 1, tm), :] * w[0:1, :]
                    + u_ref[pl.ds(HALO, tm), :] * w[1:2, :]
                    + u_ref[pl.ds(HALO + 1, tm), :] * w[2:3, :]
                    + cb_ref[:, c0:c0 + tn])
            if part == 0:
                gate = _silu(conv)
            else:
                act_ref[:, j * tn:(j + 1) * tn] = (gate * conv).astype(BF16)
    f = jnp.dot(act_ref[...], wd_ref[...], preferred_element_type=F32)
    x2 = x1_ref[...] + g2_ref[...] * f
    o_ref[...] = x2 * lax.rsqrt(jnp.mean(x2 * x2, axis=-1, keepdims=True) + EPS) * fg_ref[...]


def _ffn(h2, x1, w_up, conv_w, conv_b, w_down, g2, final_g, *, tm, tn):
    b, l, d = x1.shape
    d_ff = w_down.shape[0]
    nh = tm // HALO
    n_halo_blocks = l // HALO
    kern = functools.partial(_ffn_kernel, tm=tm, d_ff=d_ff, tn=tn)
    tile = pl.BlockSpec((None, tm, d), lambda i, t: (i, t, 0))
    prev = pl.BlockSpec((None, HALO, d), lambda i, t: (i, jnp.maximum(t * nh - 1, 0), 0))
    nxt = pl.BlockSpec((None, HALO, d), lambda i, t: (i, jnp.minimum((t + 1) * nh, n_halo_blocks - 1), 0))
    full = lambda a: pl.BlockSpec(a.shape, lambda i, t: (0,) * a.ndim)
    return pl.pallas_call(
        kern,
        grid=(b, l // tm),
        in_specs=[prev, tile, nxt, tile, full(w_up), full(conv_w), full(conv_b), full(w_down),
                  pl.BlockSpec((None, 1, d), lambda i, t: (i, 0, 0)), full(final_g)],
        out_specs=tile,
        out_shape=jax.ShapeDtypeStruct((b, l, d), F32),
        scratch_shapes=[pltpu.VMEM((tm + 2 * HALO, tn), F32), pltpu.VMEM((tm, d_ff), BF16)],
        compiler_params=_cparams("arbitrary", "arbitrary"),
        name="ffn",
    )(h2, h2, h2, x1, w_up, conv_w, conv_b, w_down, g2, final_g)


def _rope_tables(l):
    pos = np.arange(l)
    row = (pos // GRID_W).astype(np.float64)[:, None]
    col = (pos % GRID_W).astype(np.float64)[:, None]

    def angles(head_dim):
        n_freq = head_dim // 4
        inv = ROPE_BASE ** (-np.arange(n_freq, dtype=np.float64) / n_freq)
        return np.concatenate([row * inv, col * inv], axis=-1)

    ar = angles(RET_DK)
    ad = angles(DIFF_HD)
    cr, sr = np.cos(ar), np.sin(ar)
    cd, sd = np.cos(ad), np.sin(ad)
    zd = np.zeros_like(sd)
    return jnp.asarray(np.concatenate([
        cr, cr, -sr, sr,
        cd, cd, cd, cd,
        -sd, zd, -sd, zd,
        zd, sd, zd, sd], axis=-1), F32)


def _identity_tables(l):
    one = np.ones((l, HEAD_W), np.float32)
    zero = np.zeros((l, HEAD_W), np.float32)
    return jnp.asarray(np.concatenate([one, zero, one, zero, zero], axis=-1))


def kernel(x, c, ctx, c_ctx, w_mod, b_mod, norm1_g, w_in, ret_decay_logit, ret_gn_g, diff_lambda,
           diff_subln_g, w_out, norm2_g, w_up, conv_w, conv_b, w_down, final_g):
    b, l, d = x.shape
    l_ctx = ctx.shape[1]

    rows = -(-(b + 1) // 8) * 8
    cc = jnp.zeros((rows, d), F32).at[:b].set(c).at[b].set(c_ctx)
    m = _modulation(cc, w_mod[0], b_mod[0][None, :])
    sh1, sc1, g1, sh2, sc2, g2 = [m[:, i * d:(i + 1) * d] for i in range(N_MOD)]
    a1 = norm1_g[0][None, :] * (1.0 + sc1)
    a2 = norm2_g[0][None, :] * (1.0 + sc2)
    vec = lambda v: v[:b, None, :]
    ctx_vec = lambda v: jnp.broadcast_to(v[b][None, None, :], (b, 1, d))

    w_in_b = w_in[0].astype(BF16)
    p_lat = _inproj(x, vec(a1), vec(sh1), w_in_b, _rope_tables(l), tm=min(512, l), name="inproj_lat")
    p_ctx = _inproj(ctx, ctx_vec(a1), ctx_vec(sh1), w_in_b, _identity_tables(l_ctx), tm=l_ctx,
                    name="inproj_ctx")

    lg = jax.nn.log_sigmoid(ret_decay_logit[0].astype(F32))
    lgv = jnp.broadcast_to(lg.T[:, :, None], (RET_HEADS, 2, HEAD_W))
    gn_cols = ret_gn_g[0].reshape(2, RET_HEADS, HEAD_W).transpose(1, 0, 2)[..., None]
    y_ret = _retention(p_lat, p_ctx, lgv, gn_cols)
    y_diff = _diff_attention(p_lat, p_ctx, diff_lambda[0], diff_subln_g[0][None, :],
                             tq=min(1024, l), tk=512)

    x1, h2 = _outproj(x, y_ret, y_diff, w_out[0].astype(BF16), vec(g1), vec(a2), vec(sh2),
                      tm=min(512, l))
    return _ffn(h2, x1, w_up[0].astype(BF16), conv_w[0], conv_b[0][None, :], w_down[0].astype(BF16),
                vec(g2), final_g[None, :], tm=min(512, l), tn=256)
```

```python
import functools
import math

import jax
import jax.numpy as jnp
import numpy as np
from jax import lax
from jax.experimental import pallas as pl
from jax.experimental.pallas import tpu as pltpu

F32 = jnp.float32
BF16 = jnp.bfloat16

GRID_W = 64
RET_HEADS = 4
RET_DK = 128
DIFF_HEADS = 4
DIFF_HD = 64
HEAD_W = 128
GROUP_W = 512
N_GROUPS = 7
RET_CHUNK = 256
ROPE_BASE = 10000.0
EPS = 1e-6
GN_EPS = 1e-5
N_MOD = 6
LAMBDA_INIT = 0.8 - 0.6 * math.exp(-0.3 * 0)
LOG2E = math.log2(math.e)
RET_LOOKAHEAD = 2
ATTN_LOOKAHEAD = 1

VMEM_LIMIT = 56 * 1024 * 1024


def _cparams(*sem):
    return pltpu.CompilerParams(dimension_semantics=sem, vmem_limit_bytes=VMEM_LIMIT)


def _silu(v):
    return v * (1.0 / (1.0 + jnp.exp(-v)))


def _split_dot(a, w):
    a_hi = a.astype(BF16)
    a_lo = (a - a_hi.astype(F32)).astype(BF16)
    w_hi = w.astype(BF16)
    w_lo = (w - w_hi.astype(F32)).astype(BF16)
    dot = functools.partial(jnp.dot, preferred_element_type=F32)
    return dot(a_hi, w_hi) + (dot(a_hi, w_lo) + dot(a_lo, w_hi))


def _mod_kernel(c_ref, w_ref, b_ref, o_ref):
    o_ref[...] = _split_dot(_silu(c_ref[...]), w_ref[...]) + b_ref[...]


def _modulation(cc, w_mod, b_mod):
    rows, d = cc.shape
    n = w_mod.shape[1]
    tn = 1024
    return pl.pallas_call(
        _mod_kernel,
        grid=(n // tn,),
        in_specs=[pl.BlockSpec((rows, d), lambda j: (0, 0)),
                  pl.BlockSpec((d, tn), lambda j: (0, j)),
                  pl.BlockSpec((1, tn), lambda j: (0, j))],
        out_specs=pl.BlockSpec((rows, tn), lambda j: (0, j)),
        out_shape=jax.ShapeDtypeStruct((rows, n), F32),
        compiler_params=_cparams("arbitrary"),
        name="mod",
    )(cc, w_mod, b_mod)


def _rope128(v, cos, sin):
    return v * cos + pltpu.roll(v, 64, axis=1) * sin


def _rope64(v, cos, sin_lo, sin_hi):
    return v * cos + pltpu.roll(v, 96, axis=1) * sin_lo + pltpu.roll(v, 32, axis=1) * sin_hi


def _inproj_kernel(x_ref, a_ref, s_ref, w_ref, t_ref, o_ref):
    x = x_ref[...]
    xn = x * lax.rsqrt(jnp.mean(x * x, axis=-1, keepdims=True) + EPS)
    h = (xn * a_ref[...] + s_ref[...]).astype(BF16)
    cos_r = t_ref[:, 0 * HEAD_W:1 * HEAD_W]
    sin_r = t_ref[:, 1 * HEAD_W:2 * HEAD_W]
    cos_d = t_ref[:, 2 * HEAD_W:3 * HEAD_W]
    sin_dl = t_ref[:, 3 * HEAD_W:4 * HEAD_W]
    sin_dh = t_ref[:, 4 * HEAD_W:5 * HEAD_W]
    for g in range(N_GROUPS):
        acc = jnp.dot(h, w_ref[:, g * GROUP_W:(g + 1) * GROUP_W], preferred_element_type=F32)
        for hd in range(GROUP_W // HEAD_W):
            v = acc[:, hd * HEAD_W:(hd + 1) * HEAD_W]
            if g == 0:
                v = _rope128(v, cos_r, sin_r)
            elif g == 1:
                v = _rope128(v, cos_r, sin_r) * (RET_DK ** -0.5)
            elif g == 4:
                v = _rope64(v, cos_d, sin_dl, sin_dh) * (DIFF_HD ** -0.5 * LOG2E)
            elif g == 5:
                v = _rope64(v, cos_d, sin_dl, sin_dh)
            c0 = g * GROUP_W + hd * HEAD_W
            o_ref[:, c0:c0 + HEAD_W] = v.astype(BF16)


def _inproj(x, a, s, w_in, tables, *, tm, name):
    b, lx, d = x.shape
    n_cols = w_in.shape[1]
    return pl.pallas_call(
        _inproj_kernel,
        grid=(lx // tm, b),
        in_specs=[pl.BlockSpec((None, tm, d), lambda t, i: (i, t, 0)),
                  pl.BlockSpec((None, 1, d), lambda t, i: (i, 0, 0)),
                  pl.BlockSpec((None, 1, d), lambda t, i: (i, 0, 0)),
                  pl.BlockSpec((d, n_cols), lambda t, i: (0, 0)),
                  pl.BlockSpec((tm, 5 * HEAD_W), lambda t, i: (t, 0))],
        out_specs=pl.BlockSpec((None, tm, n_cols), lambda t, i: (i, t, 0)),
        out_shape=jax.ShapeDtypeStruct((b, lx, n_cols), BF16),
        compiler_params=_cparams("arbitrary", "arbitrary"),
        name=name,
    )(x, a, s, w_in, tables)


def _ret_kernel(q_ref, k_ref, v_ref, g_ref, kc_ref, vc_ref, lg_ref, gn_ref, o_ref, u_ref, s_ref,
                *, n_lat, n_all):
    c = RET_CHUNK
    dk = HEAD_W
    row = lax.broadcasted_iota(jnp.int32, (c, c), 0).astype(F32)
    col = lax.broadcasted_iota(jnp.int32, (c, c), 1).astype(F32)
    tok = lax.broadcasted_iota(jnp.int32, (c, dk), 0).astype(F32)
    lg_f = lg_ref[0:1, :]
    lg_b = lg_ref[1:2, :]
    lg_f2 = jnp.concatenate([lg_f, lg_f], axis=1)
    lg_b2 = jnp.concatenate([lg_b, lg_b], axis=1)
    mask_f = jnp.where(col >= row, jnp.exp(jnp.maximum(col - row, 0.0) * lg_f2), 0.0)
    mask_b = jnp.where(row >= col, jnp.exp(jnp.maximum(row - col, 0.0) * lg_b2), 0.0)
    qdec_f = jnp.exp((tok + 1.0) * lg_f)
    qdec_b = jnp.exp((c - tok) * lg_b)
    kdec_f = jnp.exp((c - 1.0 - tok) * lg_f)
    kdec_b = jnp.exp(tok * lg_b)
    cdec_f = jnp.exp(c * lg_f)
    cdec_b = jnp.exp(c * lg_b)
    tn_dims = (((0,), (0,)), ((), ()))
    nt_dims = (((1,), (1,)), ((), ()))

    def rows(idx):
        return slice(idx * c, (idx + 1) * c)

    def kv_chunk(idx):
        if idx < n_lat:
            return k_ref[rows(idx), :], v_ref[rows(idx), :]
        return kc_ref[rows(idx - n_lat), :], vc_ref[rows(idx - n_lat), :]

    for idx in range(n_all):
        k, v = kv_chunk(idx)
        kf = k.astype(F32)
        kd = jnp.concatenate([(kf * kdec_f).astype(BF16), (kf * kdec_b).astype(BF16)], axis=1)
        u_ref[idx] = lax.dot_general(v, kd, tn_dims, preferred_element_type=F32)

    state = jnp.zeros((dk, dk), F32)
    for idx in [*range(n_lat, n_all), *range(n_lat)]:
        s_ref[idx, :, 0:dk] = state.astype(BF16)
        state = state * cdec_f + u_ref[idx, :, 0:dk]
    state = jnp.zeros((dk, dk), F32)
    for idx in reversed(range(n_all)):
        s_ref[idx, :, dk:2 * dk] = state.astype(BF16)
        state = state * cdec_b + u_ref[idx, :, dk:2 * dk]

    gn_f = jnp.broadcast_to(gn_ref[0], (dk, c))
    gn_b = jnp.broadcast_to(gn_ref[1], (dk, c))

    def group_norm_t(o, g):
        mu = jnp.mean(o, axis=0, keepdims=True)
        d = o - mu
        var = jnp.mean(d * d, axis=0, keepdims=True)
        return d * lax.rsqrt(var + GN_EPS) * g

    def scores_t(idx):
        return lax.dot_general(k_ref[rows(idx), :], q_ref[rows(idx), :], nt_dims,
                               preferred_element_type=F32)

    ahead = [scores_t(i) for i in range(min(RET_LOOKAHEAD, n_lat))]
    for idx in range(n_lat):
        a_t = ahead.pop(0)
        if idx + RET_LOOKAHEAD < n_lat:
            ahead.append(scores_t(idx + RET_LOOKAHEAD))
        v = v_ref[rows(idx), :]
        qf = q_ref[rows(idx), :].astype(F32)
        o_f = (lax.dot_general(v, (a_t * mask_f).astype(BF16), tn_dims, preferred_element_type=F32)
               + lax.dot_general(s_ref[idx, :, 0:dk], (qf * qdec_f).astype(BF16), nt_dims,
                                 preferred_element_type=F32))
        o_b = (lax.dot_general(v, (a_t * mask_b).astype(BF16), tn_dims, preferred_element_type=F32)
               + lax.dot_general(s_ref[idx, :, dk:2 * dk], (qf * qdec_b).astype(BF16), nt_dims,
                                 preferred_element_type=F32))
        n_t = group_norm_t(o_f, gn_f) + group_norm_t(o_b, gn_b)
        y = _silu(g_ref[rows(idx), :].astype(F32)) * n_t.T
        o_ref[rows(idx), :] = y.astype(BF16)


def _retention(p_lat, p_ctx, lgv, gn_cols):
    b, l_lat, _ = p_lat.shape
    l_ctx = p_ctx.shape[1]
    n_lat = l_lat // RET_CHUNK
    n_all = n_lat + l_ctx // RET_CHUNK
    col = lambda rows, g: pl.BlockSpec((None, rows, HEAD_W), lambda i, h: (i, 0, g * RET_HEADS + h))
    kern = functools.partial(_ret_kernel, n_lat=n_lat, n_all=n_all)
    return pl.pallas_call(
        kern,
        grid=(b, RET_HEADS),
        in_specs=[col(l_lat, 0), col(l_lat, 1), col(l_lat, 2), col(l_lat, 3),
                  col(l_ctx, 1), col(l_ctx, 2),
                  pl.BlockSpec((None, 2, HEAD_W), lambda i, h: (h, 0, 0)),
                  pl.BlockSpec((None, 2, HEAD_W, 1), lambda i, h: (h, 0, 0, 0))],
        out_specs=pl.BlockSpec((None, l_lat, HEAD_W), lambda i, h: (i, 0, h)),
        out_shape=jax.ShapeDtypeStruct((b, l_lat, RET_HEADS * HEAD_W), BF16),
        scratch_shapes=[pltpu.VMEM((n_all, HEAD_W, 2 * HEAD_W), F32),
                        pltpu.VMEM((n_all, HEAD_W, 2 * HEAD_W), BF16)],
        compiler_params=_cparams("arbitrary", "arbitrary"),
        name="ret",
    )(p_lat, p_lat, p_lat, p_lat, p_ctx, p_ctx, lgv, gn_cols)


ONES_ROWS = 16


def _attn_kernel(q_ref, k_ref, v_ref, kc_ref, vc_ref, lam_ref, g_ref, o_ref, vt_ref,
                 *, tq, tk, l_lat, l_ctx):
    dv = HEAD_W
    l_all = l_lat + l_ctx
    spans = ([(False, o, min(tk, l_lat - o)) for o in range(0, l_lat, tk)]
             + [(True, o, min(tk, l_ctx - o)) for o in range(0, l_ctx, tk)])
    n_k = len(spans)

    @pl.when(pl.program_id(2) == 0)
    def _():
        for o in range(0, l_lat, RET_CHUNK):
            vt_ref[0:dv, o:o + RET_CHUNK] = v_ref[o:o + RET_CHUNK, :].astype(F32).T.astype(BF16)
        for o in range(0, l_ctx, RET_CHUNK):
            vt_ref[0:dv, l_lat + o:l_lat + o + RET_CHUNK] = (
                vc_ref[o:o + RET_CHUNK, :].astype(F32).T.astype(BF16))
        vt_ref[dv:dv + ONES_ROWS, :] = jnp.ones((ONES_ROWS, l_all), BF16)

    lane = lax.broadcasted_iota(jnp.int32, (tq, HEAD_W), 1)
    q = q_ref[...]
    zero = jnp.zeros_like(q)
    q1 = jnp.where(lane < DIFF_HD, q, zero)
    q2 = jnp.where(lane >= DIFF_HD, q, zero)
    nt_dims = (((1,), (1,)), ((), ()))

    def scores(i):
        is_ctx, o, size = spans[i]
        kt = (kc_ref if is_ctx else k_ref)[o:o + size, :]
        return (lax.dot_general(kt, q1, nt_dims, preferred_element_type=F32),
                lax.dot_general(kt, q2, nt_dims, preferred_element_type=F32))

    def update(s, vt, state):
        m, acc = state
        m_new = jnp.maximum(m, jnp.max(s, axis=0, keepdims=True))
        alpha = jnp.exp2(m - m_new)
        pr = jnp.exp2(s - m_new).astype(BF16)
        return m_new, alpha * acc + jnp.dot(vt, pr, preferred_element_type=F32)

    init = (jnp.full((1, tq), -jnp.inf, F32), jnp.zeros((dv + ONES_ROWS, tq), F32))
    st1 = st2 = init
    ahead = [scores(i) for i in range(min(ATTN_LOOKAHEAD, n_k))]
    for i in range(n_k):
        s1, s2 = ahead.pop(0)
        if i + ATTN_LOOKAHEAD < n_k:
            ahead.append(scores(i + ATTN_LOOKAHEAD))
        is_ctx, o, size = spans[i]
        start = (l_lat if is_ctx else 0) + o
        vt = vt_ref[:, start:start + size]
        st1 = update(s1, vt, st1)
        st2 = update(s2, vt, st2)
    (_, acc1), (_, acc2) = st1, st2

    lp = lam_ref[...]
    lam = (jnp.exp(jnp.sum(lp[0:1, :] * lp[1:2, :], axis=-1, keepdims=True))
           - jnp.exp(jnp.sum(lp[2:3, :] * lp[3:4, :], axis=-1, keepdims=True)) + LAMBDA_INIT)
    o = acc1[0:dv] / acc1[dv:dv + 1] - lam * (acc2[0:dv] / acc2[dv:dv + 1])
    n = o * lax.rsqrt(jnp.mean(o * o, axis=0, keepdims=True) + EPS)
    o_ref[...] = (n.T * g_ref[...] * (1.0 - LAMBDA_INIT)).astype(BF16)


def _diff_attention(p_lat, p_ctx, lam_par, subln_g, *, tq, tk):
    b, l_lat, _ = p_lat.shape
    l_ctx = p_ctx.shape[1]
    kern = functools.partial(_attn_kernel, tq=tq, tk=tk, l_lat=l_lat, l_ctx=l_ctx)
    col = lambda rows, g: pl.BlockSpec((None, rows, HEAD_W), lambda i, h, j: (i, 0, g * DIFF_HEADS + h))
    return pl.pallas_call(
        kern,
        grid=(b, DIFF_HEADS, l_lat // tq),
        in_specs=[pl.BlockSpec((None, tq, HEAD_W), lambda i, h, j: (i, j, 4 * DIFF_HEADS + h)),
                  col(l_lat, 5), col(l_lat, 6), col(l_ctx, 5), col(l_ctx, 6),
                  pl.BlockSpec((4, DIFF_HD), lambda i, h, j: (0, 0)),
                  pl.BlockSpec((1, HEAD_W), lambda i, h, j: (0, h))],
        out_specs=pl.BlockSpec((None, tq, HEAD_W), lambda i, h, j: (i, j, h)),
        out_shape=jax.ShapeDtypeStruct((b, l_lat, DIFF_HEADS * HEAD_W), BF16),
        scratch_shapes=[pltpu.VMEM((HEAD_W + ONES_ROWS, l_lat + l_ctx), BF16)],
        compiler_params=_cparams("arbitrary", "arbitrary", "arbitrary"),
        name="attn",
    )(p_lat, p_lat, p_lat, p_ctx, p_ctx, lam_par, subln_g)


HALO = 16


def _ffn_kernel(xp_ref, x_ref, xn_ref, rp_ref, r_ref, rn_ref, dp_ref, d_ref, dn_ref,
                wo_ref, g1_ref, a2_ref, s2_ref, wu_ref, cw_ref, cb_ref, wd_ref, g2_ref, fg_ref,
                o_ref, u_ref, act_ref, *, tm, d_ff, tn):
    t = pl.program_id(1)
    nt = pl.num_programs(1)
    rows = tm + 2 * HALO
    half = r_ref.shape[-1]
    xs = jnp.concatenate([xp_ref[...], x_ref[...], xn_ref[...]], axis=0)
    yr = jnp.concatenate([rp_ref[...], r_ref[...], rn_ref[...]], axis=0)
    yd = jnp.concatenate([dp_ref[...], d_ref[...], dn_ref[...]], axis=0)
    y = (jnp.dot(yr, wo_ref[:half, :], preferred_element_type=F32)
         + jnp.dot(yd, wo_ref[half:, :], preferred_element_type=F32))
    x1s = xs + g1_ref[...] * y
    h2 = (x1s * lax.rsqrt(jnp.mean(x1s * x1s, axis=-1, keepdims=True) + EPS)) * a2_ref[...] + s2_ref[...]
    row = lax.broadcasted_iota(jnp.int32, (rows, 1), 0)
    inside = ((row >= HALO) | (t > 0)) & ((row < tm + HALO) | (t < nt - 1))
    hh = jnp.where(inside, h2, 0.0).astype(BF16)
    for j in range(d_ff // tn):
        for part in range(2):
            c0 = part * d_ff + j * tn
            u_ref[...] = jnp.dot(hh, wu_ref[:, c0:c0 + tn], preferred_element_type=F32)
            w = cw_ref[:, c0:c0 + tn]
            conv = (u_ref[pl.ds(HALO - 1, tm), :] * w[0:1, :]
                    + u_ref[pl.ds(HALO, tm), :] * w[1:2, :]
                    + u_ref[pl.ds(HALO + 1, tm), :] * w[2:3, :]
                    + cb_ref[:, c0:c0 + tn])
            if part == 0:
                gate = _silu(conv)
            else:
                act_ref[:, j * tn:(j + 1) * tn] = (gate * conv).astype(BF16)
    f = jnp.dot(act_ref[...], wd_ref[...], preferred_element_type=F32)
    x2 = x1s[HALO:HALO + tm, :] + g2_ref[...] * f
    o_ref[...] = x2 * lax.rsqrt(jnp.mean(x2 * x2, axis=-1, keepdims=True) + EPS) * fg_ref[...]


def _ffn(x, y_ret, y_diff, w_out, g1, a2, s2, w_up, conv_w, conv_b, w_down, g2, final_g, *, tm, tn):
    b, l, d = x.shape
    half = y_ret.shape[-1]
    d_ff = w_down.shape[0]
    nh = tm // HALO
    n_halo_blocks = l // HALO
    kern = functools.partial(_ffn_kernel, tm=tm, d_ff=d_ff, tn=tn)

    def halo_specs(w):
        return [pl.BlockSpec((None, HALO, w), lambda i, t: (i, jnp.maximum(t * nh - 1, 0), 0)),
                pl.BlockSpec((None, tm, w), lambda i, t: (i, t, 0)),
                pl.BlockSpec((None, HALO, w), lambda i, t: (i, jnp.minimum((t + 1) * nh, n_halo_blocks - 1), 0))]

    full = lambda a: pl.BlockSpec(a.shape, lambda i, t: (0,) * a.ndim, pipeline_mode=pl.Buffered(1))
    vec = pl.BlockSpec((None, 1, d), lambda i, t: (i, 0, 0))
    return pl.pallas_call(
        kern,
        grid=(b, l // tm),
        in_specs=[*halo_specs(d), *halo_specs(half), *halo_specs(half),
                  full(w_out), vec, vec, vec,
                  full(w_up), full(conv_w), full(conv_b), full(w_down), vec, full(final_g)],
        out_specs=pl.BlockSpec((None, tm, d), lambda i, t: (i, t, 0)),
        out_shape=jax.ShapeDtypeStruct((b, l, d), F32),
        scratch_shapes=[pltpu.VMEM((tm + 2 * HALO, tn), F32), pltpu.VMEM((tm, d_ff), BF16)],
        compiler_params=_cparams("arbitrary", "arbitrary"),
        name="ffn",
    )(x, x, x, y_ret, y_ret, y_ret, y_diff, y_diff, y_diff, w_out, g1, a2, s2,
      w_up, conv_w, conv_b, w_down, g2, final_g)


def _rope_tables(l):
    pos = np.arange(l)
    row = (pos // GRID_W).astype(np.float64)[:, None]
    col = (pos % GRID_W).astype(np.float64)[:, None]

    def angles(head_dim):
        n_freq = head_dim // 4
        inv = ROPE_BASE ** (-np.arange(n_freq, dtype=np.float64) / n_freq)
        return np.concatenate([row * inv, col * inv], axis=-1)

    ar = angles(RET_DK)
    ad = angles(DIFF_HD)
    cr, sr = np.cos(ar), np.sin(ar)
    cd, sd = np.cos(ad), np.sin(ad)
    zd = np.zeros_like(sd)
    return jnp.asarray(np.concatenate([
        cr, cr, -sr, sr,
        cd, cd, cd, cd,
        -sd, zd, -sd, zd,
        zd, sd, zd, sd], axis=-1), F32)


def _identity_tables(l):
    one = np.ones((l, HEAD_W), np.float32)
    zero = np.zeros((l, HEAD_W), np.float32)
    return jnp.asarray(np.concatenate([one, zero, one, zero, zero], axis=-1))


def kernel(x, c, ctx, c_ctx, w_mod, b_mod, norm1_g, w_in, ret_decay_logit, ret_gn_g, diff_lambda,
           diff_subln_g, w_out, norm2_g, w_up, conv_w, conv_b, w_down, final_g):
    b, l, d = x.shape
    l_ctx = ctx.shape[1]

    rows = -(-(b + 1) // 8) * 8
    cc = jnp.zeros((rows, d), F32).at[:b].set(c).at[b].set(c_ctx)
    m = _modulation(cc, w_mod[0], b_mod[0][None, :])
    sh1, sc1, g1, sh2, sc2, g2 = [m[:, i * d:(i + 1) * d] for i in range(N_MOD)]
    a1 = norm1_g[0][None, :] * (1.0 + sc1)
    a2 = norm2_g[0][None, :] * (1.0 + sc2)
    vec = lambda v: v[:b, None, :]
    ctx_vec = lambda v: jnp.broadcast_to(v[b][None, None, :], (b, 1, d))

    w_in_b = w_in[0].astype(BF16)
    p_lat = _inproj(x, vec(a1), vec(sh1), w_in_b, _rope_tables(l), tm=min(512, l), name="inproj_lat")
    p_ctx = _inproj(ctx, ctx_vec(a1), ctx_vec(sh1), w_in_b, _identity_tables(l_ctx), tm=l_ctx,
                    name="inproj_ctx")

    lg = jax.nn.log_sigmoid(ret_decay_logit[0].astype(F32))
    lgv = jnp.broadcast_to(lg.T[:, :, None], (RET_HEADS, 2, HEAD_W))
    gn_cols = ret_gn_g[0].reshape(2, RET_HEADS, HEAD_W).transpose(1, 0, 2)[..., None]
    y_ret = _retention(p_lat, p_ctx, lgv, gn_cols)
    y_diff = _diff_attention(p_lat, p_ctx, diff_lambda[0], diff_subln_g[0][None, :],
                             tq=min(1024, l), tk=512)

    return _ffn(x, y_ret, y_diff, w_out[0].astype(BF16), vec(g1), vec(a2), vec(sh2),
                w_up[0].astype(BF16), conv_w[0], conv_b[0][None, :], w_down[0].astype(BF16),
                vec(g2), final_g[None, :], tm=min(512, l), tn=256)
```

```python
import functools
import math

import jax
import jax.numpy as jnp
import numpy as np
from jax import lax
from jax.experimental import pallas as pl
from jax.experimental.pallas import tpu as pltpu

F32 = jnp.float32
BF16 = jnp.bfloat16

GRID_W = 64
RET_HEADS = 4
RET_DK = 128
DIFF_HEADS = 4
DIFF_HD = 64
HEAD_W = 128
GROUP_W = 512
N_GROUPS = 7
RET_CHUNK = 256
ROPE_BASE = 10000.0
EPS = 1e-6
GN_EPS = 1e-5
N_MOD = 6
LAMBDA_INIT = 0.8 - 0.6 * math.exp(-0.3 * 0)
LOG2E = math.log2(math.e)
RET_LOOKAHEAD = 2
ATTN_LOOKAHEAD = 1

VMEM_LIMIT = 56 * 1024 * 1024


def _cparams(*sem):
    return pltpu.CompilerParams(dimension_semantics=sem, vmem_limit_bytes=VMEM_LIMIT)


def _silu(v):
    return v * (1.0 / (1.0 + jnp.exp(-v)))


def _split_dot(a, w):
    a_hi = a.astype(BF16)
    a_lo = (a - a_hi.astype(F32)).astype(BF16)
    w_hi = w.astype(BF16)
    w_lo = (w - w_hi.astype(F32)).astype(BF16)
    dot = functools.partial(jnp.dot, preferred_element_type=F32)
    return dot(a_hi, w_hi) + (dot(a_hi, w_lo) + dot(a_lo, w_hi))


def _mod_kernel(c_ref, w_ref, b_ref, o_ref):
    o_ref[...] = _split_dot(_silu(c_ref[...]), w_ref[...]) + b_ref[...]


def _modulation(cc, w_mod, b_mod):
    rows, d = cc.shape
    n = w_mod.shape[1]
    tn = 1024
    return pl.pallas_call(
        _mod_kernel,
        grid=(n // tn,),
        in_specs=[pl.BlockSpec((rows, d), lambda j: (0, 0)),
                  pl.BlockSpec((d, tn), lambda j: (0, j)),
                  pl.BlockSpec((1, tn), lambda j: (0, j))],
        out_specs=pl.BlockSpec((rows, tn), lambda j: (0, j)),
        out_shape=jax.ShapeDtypeStruct((rows, n), F32),
        compiler_params=_cparams("arbitrary"),
        name="mod",
    )(cc, w_mod, b_mod)


def _rope128(v, cos, sin):
    return v * cos + pltpu.roll(v, 64, axis=1) * sin


def _rope64(v, cos, sin_lo, sin_hi):
    return v * cos + pltpu.roll(v, 96, axis=1) * sin_lo + pltpu.roll(v, 32, axis=1) * sin_hi


def _inproj_kernel(x_ref, a_ref, s_ref, w_ref, t_ref, o_ref):
    x = x_ref[...]
    xn = x * lax.rsqrt(jnp.mean(x * x, axis=-1, keepdims=True) + EPS)
    h = (xn * a_ref[...] + s_ref[...]).astype(BF16)
    cos_r = t_ref[:, 0 * HEAD_W:1 * HEAD_W]
    sin_r = t_ref[:, 1 * HEAD_W:2 * HEAD_W]
    cos_d = t_ref[:, 2 * HEAD_W:3 * HEAD_W]
    sin_dl = t_ref[:, 3 * HEAD_W:4 * HEAD_W]
    sin_dh = t_ref[:, 4 * HEAD_W:5 * HEAD_W]
    for g in range(N_GROUPS):
        acc = jnp.dot(h, w_ref[:, g * GROUP_W:(g + 1) * GROUP_W], preferred_element_type=F32)
        for hd in range(GROUP_W // HEAD_W):
            v = acc[:, hd * HEAD_W:(hd + 1) * HEAD_W]
            if g == 0:
                v = _rope128(v, cos_r, sin_r)
            elif g == 1:
                v = _rope128(v, cos_r, sin_r) * (RET_DK ** -0.5)
            elif g == 4:
                v = _rope64(v, cos_d, sin_dl, sin_dh) * (DIFF_HD ** -0.5 * LOG2E)
            elif g == 5:
                v = _rope64(v, cos_d, sin_dl, sin_dh)
            c0 = g * GROUP_W + hd * HEAD_W
            o_ref[:, c0:c0 + HEAD_W] = v.astype(BF16)


def _inproj(x, a, s, w_in, tables, *, tm, name):
    b, lx, d = x.shape
    n_cols = w_in.shape[1]
    return pl.pallas_call(
        _inproj_kernel,
        grid=(lx // tm, b),
        in_specs=[pl.BlockSpec((None, tm, d), lambda t, i: (i, t, 0)),
                  pl.BlockSpec((None, 1, d), lambda t, i: (i, 0, 0)),
                  pl.BlockSpec((None, 1, d), lambda t, i: (i, 0, 0)),
                  pl.BlockSpec((d, n_cols), lambda t, i: (0, 0)),
                  pl.BlockSpec((tm, 5 * HEAD_W), lambda t, i: (t, 0))],
        out_specs=pl.BlockSpec((None, tm, n_cols), lambda t, i: (i, t, 0)),
        out_shape=jax.ShapeDtypeStruct((b, lx, n_cols), BF16),
        compiler_params=_cparams("arbitrary", "arbitrary"),
        name=name,
    )(x, a, s, w_in, tables)


def _ret_kernel(q_ref, k_ref, v_ref, g_ref, kc_ref, vc_ref, lg_ref, gn_ref, o_ref, u_ref, s_ref,
                *, n_lat, n_all):
    c = RET_CHUNK
    dk = HEAD_W
    row = lax.broadcasted_iota(jnp.int32, (c, c), 0).astype(F32)
    col = lax.broadcasted_iota(jnp.int32, (c, c), 1).astype(F32)
    tok = lax.broadcasted_iota(jnp.int32, (c, dk), 0).astype(F32)
    lg_f = lg_ref[0:1, :]
    lg_b = lg_ref[1:2, :]
    lg_f2 = jnp.concatenate([lg_f, lg_f], axis=1)
    lg_b2 = jnp.concatenate([lg_b, lg_b], axis=1)
    mask_f = jnp.where(col >= row, jnp.exp(jnp.maximum(col - row, 0.0) * lg_f2), 0.0)
    mask_b = jnp.where(row >= col, jnp.exp(jnp.maximum(row - col, 0.0) * lg_b2), 0.0)
    qdec_f = jnp.exp((tok + 1.0) * lg_f)
    qdec_b = jnp.exp((c - tok) * lg_b)
    kdec_f = jnp.exp((c - 1.0 - tok) * lg_f)
    kdec_b = jnp.exp(tok * lg_b)
    cdec_f = jnp.exp(c * lg_f)
    cdec_b = jnp.exp(c * lg_b)
    tn_dims = (((0,), (0,)), ((), ()))
    nt_dims = (((1,), (1,)), ((), ()))

    def rows(idx):
        return slice(idx * c, (idx + 1) * c)

    def kv_chunk(idx):
        if idx < n_lat:
            return k_ref[rows(idx), :], v_ref[rows(idx), :]
        return kc_ref[rows(idx - n_lat), :], vc_ref[rows(idx - n_lat), :]

    for idx in range(n_all):
        k, v = kv_chunk(idx)
        kf = k.astype(F32)
        kd = jnp.concatenate([(kf * kdec_f).astype(BF16), (kf * kdec_b).astype(BF16)], axis=1)
        u_ref[idx] = lax.dot_general(v, kd, tn_dims, preferred_element_type=F32)

    state = jnp.zeros((dk, dk), F32)
    for idx in [*range(n_lat, n_all), *range(n_lat)]:
        s_ref[idx, :, 0:dk] = state.astype(BF16)
        state = state * cdec_f + u_ref[idx, :, 0:dk]
    state = jnp.zeros((dk, dk), F32)
    for idx in reversed(range(n_all)):
        s_ref[idx, :, dk:2 * dk] = state.astype(BF16)
        state = state * cdec_b + u_ref[idx, :, dk:2 * dk]

    gn_f = jnp.broadcast_to(gn_ref[0], (dk, c))
    gn_b = jnp.broadcast_to(gn_ref[1], (dk, c))

    def group_norm_t(o, g):
        mu = jnp.mean(o, axis=0, keepdims=True)
        d = o - mu
        var = jnp.mean(d * d, axis=0, keepdims=True)
        return d * lax.rsqrt(var + GN_EPS) * g

    def scores_t(idx):
        return lax.dot_general(k_ref[rows(idx), :], q_ref[rows(idx), :], nt_dims,
                               preferred_element_type=F32)

    ahead = [scores_t(i) for i in range(min(RET_LOOKAHEAD, n_lat))]
    for idx in range(n_lat):
        a_t = ahead.pop(0)
        if idx + RET_LOOKAHEAD < n_lat:
            ahead.append(scores_t(idx + RET_LOOKAHEAD))
        v = v_ref[rows(idx), :]
        qf = q_ref[rows(idx), :].astype(F32)
        o_f = (lax.dot_general(v, (a_t * mask_f).astype(BF16), tn_dims, preferred_element_type=F32)
               + lax.dot_general(s_ref[idx, :, 0:dk], (qf * qdec_f).astype(BF16), nt_dims,
                                 preferred_element_type=F32))
        o_b = (lax.dot_general(v, (a_t * mask_b).astype(BF16), tn_dims, preferred_element_type=F32)
               + lax.dot_general(s_ref[idx, :, dk:2 * dk], (qf * qdec_b).astype(BF16), nt_dims,
                                 preferred_element_type=F32))
        n_t = group_norm_t(o_f, gn_f) + group_norm_t(o_b, gn_b)
        y = _silu(g_ref[rows(idx), :].astype(F32)) * n_t.T
        o_ref[rows(idx), :] = y.astype(BF16)


def _retention(p_lat, p_ctx, lgv, gn_cols):
    b, l_lat, _ = p_lat.shape
    l_ctx = p_ctx.shape[1]
    n_lat = l_lat // RET_CHUNK
    n_all = n_lat + l_ctx // RET_CHUNK
    col = lambda rows, g: pl.BlockSpec((None, rows, HEAD_W), lambda i, h: (i, 0, g * RET_HEADS + h))
    kern = functools.partial(_ret_kernel, n_lat=n_lat, n_all=n_all)
    return pl.pallas_call(
        kern,
        grid=(b, RET_HEADS),
        in_specs=[col(l_lat, 0), col(l_lat, 1), col(l_lat, 2), col(l_lat, 3),
                  col(l_ctx, 1), col(l_ctx, 2),
                  pl.BlockSpec((None, 2, HEAD_W), lambda i, h: (h, 0, 0)),
                  pl.BlockSpec((None, 2, HEAD_W, 1), lambda i, h: (h, 0, 0, 0))],
        out_specs=pl.BlockSpec((None, l_lat, HEAD_W), lambda i, h: (i, 0, h)),
        out_shape=jax.ShapeDtypeStruct((b, l_lat, RET_HEADS * HEAD_W), BF16),
        scratch_shapes=[pltpu.VMEM((n_all, HEAD_W, 2 * HEAD_W), F32),
                        pltpu.VMEM((n_all, HEAD_W, 2 * HEAD_W), BF16)],
        compiler_params=_cparams("arbitrary", "arbitrary"),
        name="ret",
    )(p_lat, p_lat, p_lat, p_lat, p_ctx, p_ctx, lgv, gn_cols)


ONES_ROWS = 16


def _attn_kernel(q_ref, k_ref, v_ref, kc_ref, vc_ref, lam_ref, g_ref, o_ref, vt_ref,
                 *, tq, tk, l_lat, l_ctx):
    dv = HEAD_W
    l_all = l_lat + l_ctx
    spans = ([(False, o, min(tk, l_lat - o)) for o in range(0, l_lat, tk)]
             + [(True, o, min(tk, l_ctx - o)) for o in range(0, l_ctx, tk)])
    n_k = len(spans)

    @pl.when(pl.program_id(2) == 0)
    def _():
        for o in range(0, l_lat, RET_CHUNK):
            vt_ref[0:dv, o:o + RET_CHUNK] = v_ref[o:o + RET_CHUNK, :].astype(F32).T.astype(BF16)
        for o in range(0, l_ctx, RET_CHUNK):
            vt_ref[0:dv, l_lat + o:l_lat + o + RET_CHUNK] = (
                vc_ref[o:o + RET_CHUNK, :].astype(F32).T.astype(BF16))
        vt_ref[dv:dv + ONES_ROWS, :] = jnp.ones((ONES_ROWS, l_all), BF16)

    lane = lax.broadcasted_iota(jnp.int32, (tq, HEAD_W), 1)
    q = q_ref[...]
    zero = jnp.zeros_like(q)
    q1 = jnp.where(lane < DIFF_HD, q, zero)
    q2 = jnp.where(lane >= DIFF_HD, q, zero)
    nt_dims = (((1,), (1,)), ((), ()))

    def scores(i):
        is_ctx, o, size = spans[i]
        kt = (kc_ref if is_ctx else k_ref)[o:o + size, :]
        return (lax.dot_general(kt, q1, nt_dims, preferred_element_type=F32),
                lax.dot_general(kt, q2, nt_dims, preferred_element_type=F32))

    def update(s, vt, state):
        m, acc = state
        m_new = jnp.maximum(m, jnp.max(s, axis=0, keepdims=True))
        alpha = jnp.exp2(m - m_new)
        pr = jnp.exp2(s - m_new).astype(BF16)
        return m_new, alpha * acc + jnp.dot(vt, pr, preferred_element_type=F32)

    init = (jnp.full((1, tq), -jnp.inf, F32), jnp.zeros((dv + ONES_ROWS, tq), F32))
    st1 = st2 = init
    ahead = [scores(i) for i in range(min(ATTN_LOOKAHEAD, n_k))]
    for i in range(n_k):
        s1, s2 = ahead.pop(0)
        if i + ATTN_LOOKAHEAD < n_k:
            ahead.append(scores(i + ATTN_LOOKAHEAD))
        is_ctx, o, size = spans[i]
        start = (l_lat if is_ctx else 0) + o
        vt = vt_ref[:, start:start + size]
        st1 = update(s1, vt, st1)
        st2 = update(s2, vt, st2)
    (_, acc1), (_, acc2) = st1, st2

    lp = lam_ref[...]
    lam = (jnp.exp(jnp.sum(lp[0:1, :] * lp[1:2, :], axis=-1, keepdims=True))
           - jnp.exp(jnp.sum(lp[2:3, :] * lp[3:4, :], axis=-1, keepdims=True)) + LAMBDA_INIT)
    o = acc1[0:dv] / acc1[dv:dv + 1] - lam * (acc2[0:dv] / acc2[dv:dv + 1])
    n = o * lax.rsqrt(jnp.mean(o * o, axis=0, keepdims=True) + EPS)
    o_ref[...] = (n.T * g_ref[...] * (1.0 - LAMBDA_INIT)).astype(BF16)


def _diff_attention(p_lat, p_ctx, lam_par, subln_g, *, tq, tk):
    b, l_lat, _ = p_lat.shape
    l_ctx = p_ctx.shape[1]
    kern = functools.partial(_attn_kernel, tq=tq, tk=tk, l_lat=l_lat, l_ctx=l_ctx)
    col = lambda rows, g: pl.BlockSpec((None, rows, HEAD_W), lambda i, h, j: (i, 0, g * DIFF_HEADS + h))
    return pl.pallas_call(
        kern,
        grid=(b, DIFF_HEADS, l_lat // tq),
        in_specs=[pl.BlockSpec((None, tq, HEAD_W), lambda i, h, j: (i, j, 4 * DIFF_HEADS + h)),
                  col(l_lat, 5), col(l_lat, 6), col(l_ctx, 5), col(l_ctx, 6),
                  pl.BlockSpec((4, DIFF_HD), lambda i, h, j: (0, 0)),
                  pl.BlockSpec((1, HEAD_W), lambda i, h, j: (0, h))],
        out_specs=pl.BlockSpec((None, tq, HEAD_W), lambda i, h, j: (i, j, h)),
        out_shape=jax.ShapeDtypeStruct((b, l_lat, DIFF_HEADS * HEAD_W), BF16),
        scratch_shapes=[pltpu.VMEM((HEAD_W + ONES_ROWS, l_lat + l_ctx), BF16)],
        compiler_params=_cparams("arbitrary", "arbitrary", "arbitrary"),
        name="attn",
    )(p_lat, p_lat, p_lat, p_ctx, p_ctx, lam_par, subln_g)


HALO = 16


def _ffn_kernel(xp_ref, x_ref, xn_ref, rp_ref, r_ref, rn_ref, dp_ref, d_ref, dn_ref,
                wo_ref, g1_ref, a2_ref, s2_ref, wu_ref, cw_ref, cb_ref, wd_ref, g2_ref, fg_ref,
                o_ref, act_ref, *, tm, d_ff, tn):
    t = pl.program_id(1)
    nt = pl.num_programs(1)
    rows = tm + 2 * HALO
    half = r_ref.shape[-1]
    xs = jnp.concatenate([xp_ref[...], x_ref[...], xn_ref[...]], axis=0)
    yr = jnp.concatenate([rp_ref[...], r_ref[...], rn_ref[...]], axis=0)
    yd = jnp.concatenate([dp_ref[...], d_ref[...], dn_ref[...]], axis=0)
    y = (jnp.dot(yr, wo_ref[:half, :], preferred_element_type=F32)
         + jnp.dot(yd, wo_ref[half:, :], preferred_element_type=F32))
    x1s = xs + g1_ref[...] * y
    h2 = (x1s * lax.rsqrt(jnp.mean(x1s * x1s, axis=-1, keepdims=True) + EPS)) * a2_ref[...] + s2_ref[...]
    row = lax.broadcasted_iota(jnp.int32, (rows, 1), 0)
    inside = ((row >= HALO) | (t > 0)) & ((row < tm + HALO) | (t < nt - 1))
    hh = jnp.where(inside, h2, 0.0).astype(BF16)
    for j in range(d_ff // tn):
        for part in range(2):
            c0 = part * d_ff + j * tn
            u = jnp.dot(hh, wu_ref[:, c0:c0 + tn], preferred_element_type=F32)
            w = cw_ref[:, c0:c0 + tn]
            conv = (pltpu.roll(u, 1, axis=0)[HALO:HALO + tm] * w[0:1, :]
                    + u[HALO:HALO + tm] * w[1:2, :]
                    + pltpu.roll(u, rows - 1, axis=0)[HALO:HALO + tm] * w[2:3, :]
                    + cb_ref[:, c0:c0 + tn])
            if part == 0:
                gate = _silu(conv)
            else:
                act_ref[:, j * tn:(j + 1) * tn] = (gate * conv).astype(BF16)
    f = jnp.dot(act_ref[...], wd_ref[...], preferred_element_type=F32)
    x2 = x1s[HALO:HALO + tm, :] + g2_ref[...] * f
    o_ref[...] = x2 * lax.rsqrt(jnp.mean(x2 * x2, axis=-1, keepdims=True) + EPS) * fg_ref[...]


def _ffn(x, y_ret, y_diff, w_out, g1, a2, s2, w_up, conv_w, conv_b, w_down, g2, final_g, *, tm, tn):
    b, l, d = x.shape
    half = y_ret.shape[-1]
    d_ff = w_down.shape[0]
    nh = tm // HALO
    n_halo_blocks = l // HALO
    kern = functools.partial(_ffn_kernel, tm=tm, d_ff=d_ff, tn=tn)

    def halo_specs(w):
        return [pl.BlockSpec((None, HALO, w), lambda i, t: (i, jnp.maximum(t * nh - 1, 0), 0)),
                pl.BlockSpec((None, tm, w), lambda i, t: (i, t, 0)),
                pl.BlockSpec((None, HALO, w), lambda i, t: (i, jnp.minimum((t + 1) * nh, n_halo_blocks - 1), 0))]

    full = lambda a: pl.BlockSpec(a.shape, lambda i, t: (0,) * a.ndim, pipeline_mode=pl.Buffered(1))
    vec = pl.BlockSpec((None, 1, d), lambda i, t: (i, 0, 0))
    return pl.pallas_call(
        kern,
        grid=(b, l // tm),
        in_specs=[*halo_specs(d), *halo_specs(half), *halo_specs(half),
                  full(w_out), vec, vec, vec,
                  full(w_up), full(conv_w), full(conv_b), full(w_down), vec, full(final_g)],
        out_specs=pl.BlockSpec((None, tm, d), lambda i, t: (i, t, 0)),
        out_shape=jax.ShapeDtypeStruct((b, l, d), F32),
        scratch_shapes=[pltpu.VMEM((tm, d_ff), BF16)],
        compiler_params=_cparams("arbitrary", "arbitrary"),
        name="ffn",
    )(x, x, x, y_ret, y_ret, y_ret, y_diff, y_diff, y_diff, w_out, g1, a2, s2,
      w_up, conv_w, conv_b, w_down, g2, final_g)


def _rope_tables(l):
    pos = np.arange(l)
    row = (pos // GRID_W).astype(np.float64)[:, None]
    col = (pos % GRID_W).astype(np.float64)[:, None]

    def angles(head_dim):
        n_freq = head_dim // 4
        inv = ROPE_BASE ** (-np.arange(n_freq, dtype=np.float64) / n_freq)
        return np.concatenate([row * inv, col * inv], axis=-1)

    ar = angles(RET_DK)
    ad = angles(DIFF_HD)
    cr, sr = np.cos(ar), np.sin(ar)
    cd, sd = np.cos(ad), np.sin(ad)
    zd = np.zeros_like(sd)
    return jnp.asarray(np.concatenate([
        cr, cr, -sr, sr,
        cd, cd, cd, cd,
        -sd, zd, -sd, zd,
        zd, sd, zd, sd], axis=-1), F32)


def _identity_tables(l):
    one = np.ones((l, HEAD_W), np.float32)
    zero = np.zeros((l, HEAD_W), np.float32)
    return jnp.asarray(np.concatenate([one, zero, one, zero, zero], axis=-1))


def kernel(x, c, ctx, c_ctx, w_mod, b_mod, norm1_g, w_in, ret_decay_logit, ret_gn_g, diff_lambda,
           diff_subln_g, w_out, norm2_g, w_up, conv_w, conv_b, w_down, final_g):
    b, l, d = x.shape
    l_ctx = ctx.shape[1]

    rows = -(-(b + 1) // 8) * 8
    cc = jnp.zeros((rows, d), F32).at[:b].set(c).at[b].set(c_ctx)
    m = _modulation(cc, w_mod[0], b_mod[0][None, :])
    sh1, sc1, g1, sh2, sc2, g2 = [m[:, i * d:(i + 1) * d] for i in range(N_MOD)]
    a1 = norm1_g[0][None, :] * (1.0 + sc1)
    a2 = norm2_g[0][None, :] * (1.0 + sc2)
    vec = lambda v: v[:b, None, :]
    ctx_vec = lambda v: jnp.broadcast_to(v[b][None, None, :], (b, 1, d))

    w_in_b = w_in[0].astype(BF16)
    p_lat = _inproj(x, vec(a1), vec(sh1), w_in_b, _rope_tables(l), tm=min(512, l), name="inproj_lat")
    p_ctx = _inproj(ctx, ctx_vec(a1), ctx_vec(sh1), w_in_b, _identity_tables(l_ctx), tm=l_ctx,
                    name="inproj_ctx")

    lg = jax.nn.log_sigmoid(ret_decay_logit[0].astype(F32))
    lgv = jnp.broadcast_to(lg.T[:, :, None], (RET_HEADS, 2, HEAD_W))
    gn_cols = ret_gn_g[0].reshape(2, RET_HEADS, HEAD_W).transpose(1, 0, 2)[..., None]
    y_ret = _retention(p_lat, p_ctx, lgv, gn_cols)
    y_diff = _diff_attention(p_lat, p_ctx, diff_lambda[0], diff_subln_g[0][None, :],
                             tq=min(1024, l), tk=512)

    return _ffn(x, y_ret, y_diff, w_out[0].astype(BF16), vec(g1), vec(a2), vec(sh2),
                w_up[0].astype(BF16), conv_w[0], conv_b[0][None, :], w_down[0].astype(BF16),
                vec(g2), final_g[None, :], tm=min(512, l), tn=256)
```

```python
import functools
import math

import jax
import jax.numpy as jnp
import numpy as np
from jax import lax
from jax.experimental import pallas as pl
from jax.experimental.pallas import tpu as pltpu

F32 = jnp.float32
BF16 = jnp.bfloat16

GRID_W = 64
RET_HEADS = 4
RET_DK = 128
DIFF_HEADS = 4
DIFF_HD = 64
HEAD_W = 128
GROUP_W = 512
N_GROUPS = 7
RET_CHUNK = 256
ROPE_BASE = 10000.0
EPS = 1e-6
GN_EPS = 1e-5
N_MOD = 6
LAMBDA_INIT = 0.8 - 0.6 * math.exp(-0.3 * 0)
LOG2E = math.log2(math.e)
RET_LOOKAHEAD = 2
ATTN_LOOKAHEAD = 1

VMEM_LIMIT = 56 * 1024 * 1024


def _cparams(*sem):
    return pltpu.CompilerParams(dimension_semantics=sem, vmem_limit_bytes=VMEM_LIMIT)


def _silu(v):
    return v * (1.0 / (1.0 + jnp.exp(-v)))


def _split_dot(a, w):
    a_hi = a.astype(BF16)
    a_lo = (a - a_hi.astype(F32)).astype(BF16)
    w_hi = w.astype(BF16)
    w_lo = (w - w_hi.astype(F32)).astype(BF16)
    dot = functools.partial(jnp.dot, preferred_element_type=F32)
    return dot(a_hi, w_hi) + (dot(a_hi, w_lo) + dot(a_lo, w_hi))


def _mod_kernel(c_ref, w_ref, b_ref, o_ref):
    o_ref[...] = _split_dot(_silu(c_ref[...]), w_ref[...]) + b_ref[...]


def _modulation(cc, w_mod, b_mod):
    rows, d = cc.shape
    n = w_mod.shape[1]
    tn = 1024
    return pl.pallas_call(
        _mod_kernel,
        grid=(n // tn,),
        in_specs=[pl.BlockSpec((rows, d), lambda j: (0, 0)),
                  pl.BlockSpec((d, tn), lambda j: (0, j)),
                  pl.BlockSpec((1, tn), lambda j: (0, j))],
        out_specs=pl.BlockSpec((rows, tn), lambda j: (0, j)),
        out_shape=jax.ShapeDtypeStruct((rows, n), F32),
        compiler_params=_cparams("arbitrary"),
        name="mod",
    )(cc, w_mod, b_mod)


def _rope128(v, cos, sin):
    return v * cos + pltpu.roll(v, 64, axis=1) * sin


def _rope64(v, cos, sin_lo, sin_hi):
    return v * cos + pltpu.roll(v, 96, axis=1) * sin_lo + pltpu.roll(v, 32, axis=1) * sin_hi


def _inproj_kernel(x_ref, a_ref, s_ref, w_ref, t_ref, o_ref):
    x = x_ref[...]
    xn = x * lax.rsqrt(jnp.mean(x * x, axis=-1, keepdims=True) + EPS)
    h = (xn * a_ref[...] + s_ref[...]).astype(BF16)
    cos_r = t_ref[:, 0 * HEAD_W:1 * HEAD_W]
    sin_r = t_ref[:, 1 * HEAD_W:2 * HEAD_W]
    cos_d = t_ref[:, 2 * HEAD_W:3 * HEAD_W]
    sin_dl = t_ref[:, 3 * HEAD_W:4 * HEAD_W]
    sin_dh = t_ref[:, 4 * HEAD_W:5 * HEAD_W]
    for g in range(N_GROUPS):
        acc = jnp.dot(h, w_ref[:, g * GROUP_W:(g + 1) * GROUP_W], preferred_element_type=F32)
        for hd in range(GROUP_W // HEAD_W):
            v = acc[:, hd * HEAD_W:(hd + 1) * HEAD_W]
            if g == 0:
                v = _rope128(v, cos_r, sin_r)
            elif g == 1:
                v = _rope128(v, cos_r, sin_r) * (RET_DK ** -0.5)
            elif g == 4:
                v = _rope64(v, cos_d, sin_dl, sin_dh) * (DIFF_HD ** -0.5 * LOG2E)
            elif g == 5:
                v = _rope64(v, cos_d, sin_dl, sin_dh)
            c0 = g * GROUP_W + hd * HEAD_W
            o_ref[:, c0:c0 + HEAD_W] = v.astype(BF16)


def _inproj(x, a, s, w_in, tables, *, tm, name):
    b, lx, d = x.shape
    n_cols = w_in.shape[1]
    return pl.pallas_call(
        _inproj_kernel,
        grid=(lx // tm, b),
        in_specs=[pl.BlockSpec((None, tm, d), lambda t, i: (i, t, 0)),
                  pl.BlockSpec((None, 1, d), lambda t, i: (i, 0, 0)),
                  pl.BlockSpec((None, 1, d), lambda t, i: (i, 0, 0)),
                  pl.BlockSpec((d, n_cols), lambda t, i: (0, 0)),
                  pl.BlockSpec((tm, 5 * HEAD_W), lambda t, i: (t, 0))],
        out_specs=pl.BlockSpec((None, tm, n_cols), lambda t, i: (i, t, 0)),
        out_shape=jax.ShapeDtypeStruct((b, lx, n_cols), BF16),
        compiler_params=_cparams("arbitrary", "arbitrary"),
        name=name,
    )(x, a, s, w_in, tables)


def _ret_kernel(q_ref, k_ref, v_ref, g_ref, kc_ref, vc_ref, lg_ref, gn_ref, o_ref, u_ref, s_ref,
                *, n_lat, n_all):
    c = RET_CHUNK
    dk = HEAD_W
    row = lax.broadcasted_iota(jnp.int32, (c, c), 0).astype(F32)
    col = lax.broadcasted_iota(jnp.int32, (c, c), 1).astype(F32)
    tok = lax.broadcasted_iota(jnp.int32, (c, dk), 0).astype(F32)
    lg_f = lg_ref[0:1, :]
    lg_b = lg_ref[1:2, :]
    lg_f2 = jnp.concatenate([lg_f, lg_f], axis=1)
    lg_b2 = jnp.concatenate([lg_b, lg_b], axis=1)
    mask_f = jnp.where(col >= row, jnp.exp(jnp.maximum(col - row, 0.0) * lg_f2), 0.0)
    mask_b = jnp.where(row >= col, jnp.exp(jnp.maximum(row - col, 0.0) * lg_b2), 0.0)
    qdec_f = jnp.exp((tok + 1.0) * lg_f)
    qdec_b = jnp.exp((c - tok) * lg_b)
    kdec_f = jnp.exp((c - 1.0 - tok) * lg_f)
    kdec_b = jnp.exp(tok * lg_b)
    cdec_f = jnp.exp(c * lg_f)
    cdec_b = jnp.exp(c * lg_b)
    tn_dims = (((0,), (0,)), ((), ()))
    nt_dims = (((1,), (1,)), ((), ()))

    def rows(idx):
        return slice(idx * c, (idx + 1) * c)

    def kv_chunk(idx):
        if idx < n_lat:
            return k_ref[rows(idx), :], v_ref[rows(idx), :]
        return kc_ref[rows(idx - n_lat), :], vc_ref[rows(idx - n_lat), :]

    for idx in range(n_all):
        k, v = kv_chunk(idx)
        kf = k.astype(F32)
        kd = jnp.concatenate([(kf * kdec_f).astype(BF16), (kf * kdec_b).astype(BF16)], axis=1)
        u_ref[idx] = lax.dot_general(v, kd, tn_dims, preferred_element_type=F32)

    state = jnp.zeros((dk, dk), F32)
    for idx in [*range(n_lat, n_all), *range(n_lat)]:
        s_ref[idx, :, 0:dk] = state.astype(BF16)
        state = state * cdec_f + u_ref[idx, :, 0:dk]
    state = jnp.zeros((dk, dk), F32)
    for idx in reversed(range(n_all)):
        s_ref[idx, :, dk:2 * dk] = state.astype(BF16)
        state = state * cdec_b + u_ref[idx, :, dk:2 * dk]

    gn_f = jnp.broadcast_to(gn_ref[0], (dk, c))
    gn_b = jnp.broadcast_to(gn_ref[1], (dk, c))

    def group_norm_t(o, g):
        mu = jnp.mean(o, axis=0, keepdims=True)
        d = o - mu
        var = jnp.mean(d * d, axis=0, keepdims=True)
        return d * lax.rsqrt(var + GN_EPS) * g

    def scores_t(idx):
        return lax.dot_general(k_ref[rows(idx), :], q_ref[rows(idx), :], nt_dims,
                               preferred_element_type=F32)

    ahead = [scores_t(i) for i in range(min(RET_LOOKAHEAD, n_lat))]
    for idx in range(n_lat):
        a_t = ahead.pop(0)
        if idx + RET_LOOKAHEAD < n_lat:
            ahead.append(scores_t(idx + RET_LOOKAHEAD))
        v = v_ref[rows(idx), :]
        qf = q_ref[rows(idx), :].astype(F32)
        o_f = (lax.dot_general(v, (a_t * mask_f).astype(BF16), tn_dims, preferred_element_type=F32)
               + lax.dot_general(s_ref[idx, :, 0:dk], (qf * qdec_f).astype(BF16), nt_dims,
                                 preferred_element_type=F32))
        o_b = (lax.dot_general(v, (a_t * mask_b).astype(BF16), tn_dims, preferred_element_type=F32)
               + lax.dot_general(s_ref[idx, :, dk:2 * dk], (qf * qdec_b).astype(BF16), nt_dims,
                                 preferred_element_type=F32))
        n_t = group_norm_t(o_f, gn_f) + group_norm_t(o_b, gn_b)
        y = _silu(g_ref[rows(idx), :].astype(F32)) * n_t.T
        o_ref[rows(idx), :] = y.astype(BF16)


def _retention(p_lat, p_ctx, lgv, gn_cols):
    b, l_lat, _ = p_lat.shape
    l_ctx = p_ctx.shape[1]
    n_lat = l_lat // RET_CHUNK
    n_all = n_lat + l_ctx // RET_CHUNK
    col = lambda rows, g: pl.BlockSpec((None, rows, HEAD_W), lambda i, h: (i, 0, g * RET_HEADS + h))
    kern = functools.partial(_ret_kernel, n_lat=n_lat, n_all=n_all)
    return pl.pallas_call(
        kern,
        grid=(b, RET_HEADS),
        in_specs=[col(l_lat, 0), col(l_lat, 1), col(l_lat, 2), col(l_lat, 3),
                  col(l_ctx, 1), col(l_ctx, 2),
                  pl.BlockSpec((None, 2, HEAD_W), lambda i, h: (h, 0, 0)),
                  pl.BlockSpec((None, 2, HEAD_W, 1), lambda i, h: (h, 0, 0, 0))],
        out_specs=pl.BlockSpec((None, l_lat, HEAD_W), lambda i, h: (i, 0, h)),
        out_shape=jax.ShapeDtypeStruct((b, l_lat, RET_HEADS * HEAD_W), BF16),
        scratch_shapes=[pltpu.VMEM((n_all, HEAD_W, 2 * HEAD_W), F32),
                        pltpu.VMEM((n_all, HEAD_W, 2 * HEAD_W), BF16)],
        compiler_params=_cparams("arbitrary", "arbitrary"),
        name="ret",
    )(p_lat, p_lat, p_lat, p_lat, p_ctx, p_ctx, lgv, gn_cols)


PLAIN_EXP_MAX_LOG2 = 80.0
PLAIN_EXP_MAX_VALUE = 2.0 ** 30
ONES_ROWS = 16


def _attn_kernel(q_ref, k_ref, v_ref, kc_ref, vc_ref, lam_ref, g_ref, o_ref, vt_ref, kv_stat_ref,
                 *, tq, tk, l_lat, l_ctx):
    dv = HEAD_W
    l_all = l_lat + l_ctx
    spans = ([(False, o, min(tk, l_lat - o)) for o in range(0, l_lat, tk)]
             + [(True, o, min(tk, l_ctx - o)) for o in range(0, l_ctx, tk)])
    n_k = len(spans)

    chunks = ([(k_ref, v_ref, o, o) for o in range(0, l_lat, RET_CHUNK)]
              + [(kc_ref, vc_ref, o, l_lat + o) for o in range(0, l_ctx, RET_CHUNK)])

    @pl.when(pl.program_id(2) == 0)
    def _():
        kmax = vmax = jnp.zeros((1, HEAD_W), F32)
        for kr, vr, o, dst in chunks:
            vf = vr[o:o + RET_CHUNK, :].astype(F32)
            vt_ref[0:dv, dst:dst + RET_CHUNK] = vf.T.astype(BF16)
            vmax = jnp.maximum(vmax, jnp.max(jnp.abs(vf), axis=0, keepdims=True))
            kf = kr[o:o + RET_CHUNK, :].astype(F32)
            kmax = jnp.maximum(kmax, jnp.max(jnp.abs(kf), axis=0, keepdims=True))
        vt_ref[dv:dv + ONES_ROWS, :] = jnp.ones((ONES_ROWS, l_all), BF16)
        kv_stat_ref[0:1, :] = kmax
        kv_stat_ref[1:2, :] = vmax

    lane = lax.broadcasted_iota(jnp.int32, (tq, HEAD_W), 1)
    q = q_ref[...]
    zero = jnp.zeros_like(q)
    q1 = jnp.where(lane < DIFF_HD, q, zero)
    q2 = jnp.where(lane >= DIFF_HD, q, zero)
    nt_dims = (((1,), (1,)), ((), ()))

    r = lax.broadcasted_iota(jnp.int32, (HEAD_W, HEAD_W), 0)
    c = lax.broadcasted_iota(jnp.int32, (HEAD_W, HEAD_W), 1)
    same_map = jnp.where((r < DIFF_HD) == (c < DIFF_HD), 1.0, 0.0).astype(BF16)
    q_abs = (jnp.abs(q.astype(F32)) * kv_stat_ref[0:1, :]).astype(BF16)
    bound = jnp.dot(q_abs, same_map, preferred_element_type=F32)
    plain = jnp.logical_and(jnp.max(bound) <= PLAIN_EXP_MAX_LOG2,
                            jnp.max(kv_stat_ref[1:2, :]) <= PLAIN_EXP_MAX_VALUE)

    def scores(i):
        is_ctx, o, size = spans[i]
        kt = (kc_ref if is_ctx else k_ref)[o:o + size, :]
        return (lax.dot_general(kt, q1, nt_dims, preferred_element_type=F32),
                lax.dot_general(kt, q2, nt_dims, preferred_element_type=F32))

    def update_online(s, vt, state):
        m, acc = state
        m_new = jnp.maximum(m, jnp.max(s, axis=0, keepdims=True))
        alpha = jnp.exp2(m - m_new)
        pr = jnp.exp2(s - m_new).astype(BF16)
        return m_new, alpha * acc + jnp.dot(vt, pr, preferred_element_type=F32)

    def update_plain(s, vt, state):
        m, acc = state
        return m, acc + jnp.dot(vt, jnp.exp2(s).astype(BF16), preferred_element_type=F32)

    def attend(update):
        init = (jnp.full((1, tq), -jnp.inf, F32), jnp.zeros((dv + ONES_ROWS, tq), F32))
        st1 = st2 = init
        ahead = [scores(i) for i in range(min(ATTN_LOOKAHEAD, n_k))]
        for i in range(n_k):
            s1, s2 = ahead.pop(0)
            if i + ATTN_LOOKAHEAD < n_k:
                ahead.append(scores(i + ATTN_LOOKAHEAD))
            is_ctx, o, size = spans[i]
            start = (l_lat if is_ctx else 0) + o
            vt = vt_ref[:, start:start + size]
            st1 = update(s1, vt, st1)
            st2 = update(s2, vt, st2)
        (_, acc1), (_, acc2) = st1, st2

        lp = lam_ref[...]
        lam = (jnp.exp(jnp.sum(lp[0:1, :] * lp[1:2, :], axis=-1, keepdims=True))
               - jnp.exp(jnp.sum(lp[2:3, :] * lp[3:4, :], axis=-1, keepdims=True)) + LAMBDA_INIT)
        o = acc1[0:dv] / acc1[dv:dv + 1] - lam * (acc2[0:dv] / acc2[dv:dv + 1])
        n = o * lax.rsqrt(jnp.mean(o * o, axis=0, keepdims=True) + EPS)
        o_ref[...] = (n.T * g_ref[...] * (1.0 - LAMBDA_INIT)).astype(BF16)

    @pl.when(plain)
    def _():
        attend(update_plain)

    @pl.when(jnp.logical_not(plain))
    def _():
        attend(update_online)


def _diff_attention(p_lat, p_ctx, lam_par, subln_g, *, tq, tk):
    b, l_lat, _ = p_lat.shape
    l_ctx = p_ctx.shape[1]
    kern = functools.partial(_attn_kernel, tq=tq, tk=tk, l_lat=l_lat, l_ctx=l_ctx)
    col = lambda rows, g: pl.BlockSpec((None, rows, HEAD_W), lambda i, h, j: (i, 0, g * DIFF_HEADS + h))
    return pl.pallas_call(
        kern,
        grid=(b, DIFF_HEADS, l_lat // tq),
        in_specs=[pl.BlockSpec((None, tq, HEAD_W), lambda i, h, j: (i, j, 4 * DIFF_HEADS + h)),
                  col(l_lat, 5), col(l_lat, 6), col(l_ctx, 5), col(l_ctx, 6),
                  pl.BlockSpec((4, DIFF_HD), lambda i, h, j: (0, 0)),
                  pl.BlockSpec((1, HEAD_W), lambda i, h, j: (0, h))],
        out_specs=pl.BlockSpec((None, tq, HEAD_W), lambda i, h, j: (i, j, h)),
        out_shape=jax.ShapeDtypeStruct((b, l_lat, DIFF_HEADS * HEAD_W), BF16),
        scratch_shapes=[pltpu.VMEM((HEAD_W + ONES_ROWS, l_lat + l_ctx), BF16),
                        pltpu.VMEM((8, HEAD_W), F32)],
        compiler_params=_cparams("arbitrary", "arbitrary", "arbitrary"),
        name="attn",
    )(p_lat, p_lat, p_lat, p_ctx, p_ctx, lam_par, subln_g)


HALO = 16


def _ffn_kernel(xp_ref, x_ref, xn_ref, rp_ref, r_ref, rn_ref, dp_ref, d_ref, dn_ref,
                wo_ref, g1_ref, a2_ref, s2_ref, wu_ref, cw_ref, cb_ref, wd_ref, g2_ref, fg_ref,
                o_ref, act_ref, *, tm, d_ff, tn):
    t = pl.program_id(1)
    nt = pl.num_programs(1)
    rows = tm + 2 * HALO
    half = r_ref.shape[-1]
    xs = jnp.concatenate([xp_ref[...], x_ref[...], xn_ref[...]], axis=0)
    yr = jnp.concatenate([rp_ref[...], r_ref[...], rn_ref[...]], axis=0)
    yd = jnp.concatenate([dp_ref[...], d_ref[...], dn_ref[...]], axis=0)
    y = (jnp.dot(yr, wo_ref[:half, :], preferred_element_type=F32)
         + jnp.dot(yd, wo_ref[half:, :], preferred_element_type=F32))
    x1s = xs + g1_ref[...] * y
    h2 = (x1s * lax.rsqrt(jnp.mean(x1s * x1s, axis=-1, keepdims=True) + EPS)) * a2_ref[...] + s2_ref[...]
    row = lax.broadcasted_iota(jnp.int32, (rows, 1), 0)
    inside = ((row >= HALO) | (t > 0)) & ((row < tm + HALO) | (t < nt - 1))
    hh = jnp.where(inside, h2, 0.0).astype(BF16)
    for j in range(d_ff // tn):
        for part in range(2):
            c0 = part * d_ff + j * tn
            u = jnp.dot(hh, wu_ref[:, c0:c0 + tn], preferred_element_type=F32)
            w = cw_ref[:, c0:c0 + tn]
            conv = (pltpu.roll(u, 1, axis=0)[HALO:HALO + tm] * w[0:1, :]
                    + u[HALO:HALO + tm] * w[1:2, :]
                    + pltpu.roll(u, rows - 1, axis=0)[HALO:HALO + tm] * w[2:3, :]
                    + cb_ref[:, c0:c0 + tn])
            if part == 0:
                gate = _silu(conv)
            else:
                act_ref[:, j * tn:(j + 1) * tn] = (gate * conv).astype(BF16)
    f = jnp.dot(act_ref[...], wd_ref[...], preferred_element_type=F32)
    x2 = x1s[HALO:HALO + tm, :] + g2_ref[...] * f
    o_ref[...] = x2 * lax.rsqrt(jnp.mean(x2 * x2, axis=-1, keepdims=True) + EPS) * fg_ref[...]


def _ffn(x, y_ret, y_diff, w_out, g1, a2, s2, w_up, conv_w, conv_b, w_down, g2, final_g, *, tm, tn):
    b, l, d = x.shape
    half = y_ret.shape[-1]
    d_ff = w_down.shape[0]
    nh = tm // HALO
    n_halo_blocks = l // HALO
    kern = functools.partial(_ffn_kernel, tm=tm, d_ff=d_ff, tn=tn)

    def halo_specs(w):
        return [pl.BlockSpec((None, HALO, w), lambda i, t: (i, jnp.maximum(t * nh - 1, 0), 0)),
                pl.BlockSpec((None, tm, w), lambda i, t: (i, t, 0)),
                pl.BlockSpec((None, HALO, w), lambda i, t: (i, jnp.minimum((t + 1) * nh, n_halo_blocks - 1), 0))]

    full = lambda a: pl.BlockSpec(a.shape, lambda i, t: (0,) * a.ndim, pipeline_mode=pl.Buffered(1))
    vec = pl.BlockSpec((None, 1, d), lambda i, t: (i, 0, 0))
    return pl.pallas_call(
        kern,
        grid=(b, l // tm),
        in_specs=[*halo_specs(d), *halo_specs(half), *halo_specs(half),
                  full(w_out), vec, vec, vec,
                  full(w_up), full(conv_w), full(conv_b), full(w_down), vec, full(final_g)],
        out_specs=pl.BlockSpec((None, tm, d), lambda i, t: (i, t, 0)),
        out_shape=jax.ShapeDtypeStruct((b, l, d), F32),
        scratch_shapes=[pltpu.VMEM((tm, d_ff), BF16)],
        compiler_params=_cparams("arbitrary", "arbitrary"),
        name="ffn",
    )(x, x, x, y_ret, y_ret, y_ret, y_diff, y_diff, y_diff, w_out, g1, a2, s2,
      w_up, conv_w, conv_b, w_down, g2, final_g)


def _rope_tables(l):
    pos = np.arange(l)
    row = (pos // GRID_W).astype(np.float64)[:, None]
    col = (pos % GRID_W).astype(np.float64)[:, None]

    def angles(head_dim):
        n_freq = head_dim // 4
        inv = ROPE_BASE ** (-np.arange(n_freq, dtype=np.float64) / n_freq)
        return np.concatenate([row * inv, col * inv], axis=-1)

    ar = angles(RET_DK)
    ad = angles(DIFF_HD)
    cr, sr = np.cos(ar), np.sin(ar)
    cd, sd = np.cos(ad), np.sin(ad)
    zd = np.zeros_like(sd)
    return jnp.asarray(np.concatenate([
        cr, cr, -sr, sr,
        cd, cd, cd, cd,
        -sd, zd, -sd, zd,
        zd, sd, zd, sd], axis=-1), F32)


def _identity_tables(l):
    one = np.ones((l, HEAD_W), np.float32)
    zero = np.zeros((l, HEAD_W), np.float32)
    return jnp.asarray(np.concatenate([one, zero, one, zero, zero], axis=-1))


def kernel(x, c, ctx, c_ctx, w_mod, b_mod, norm1_g, w_in, ret_decay_logit, ret_gn_g, diff_lambda,
           diff_subln_g, w_out, norm2_g, w_up, conv_w, conv_b, w_down, final_g):
    b, l, d = x.shape
    l_ctx = ctx.shape[1]

    rows = -(-(b + 1) // 8) * 8
    cc = jnp.zeros((rows, d), F32).at[:b].set(c).at[b].set(c_ctx)
    m = _modulation(cc, w_mod[0], b_mod[0][None, :])
    sh1, sc1, g1, sh2, sc2, g2 = [m[:, i * d:(i + 1) * d] for i in range(N_MOD)]
    a1 = norm1_g[0][None, :] * (1.0 + sc1)
    a2 = norm2_g[0][None, :] * (1.0 + sc2)
    vec = lambda v: v[:b, None, :]
    ctx_vec = lambda v: jnp.broadcast_to(v[b][None, None, :], (b, 1, d))

    w_in_b = w_in[0].astype(BF16)
    p_lat = _inproj(x, vec(a1), vec(sh1), w_in_b, _rope_tables(l), tm=min(512, l), name="inproj_lat")
    p_ctx = _inproj(ctx, ctx_vec(a1), ctx_vec(sh1), w_in_b, _identity_tables(l_ctx), tm=l_ctx,
                    name="inproj_ctx")

    lg = jax.nn.log_sigmoid(ret_decay_logit[0].astype(F32))
    lgv = jnp.broadcast_to(lg.T[:, :, None], (RET_HEADS, 2, HEAD_W))
    gn_cols = ret_gn_g[0].reshape(2, RET_HEADS, HEAD_W).transpose(1, 0, 2)[..., None]
    y_ret = _retention(p_lat, p_ctx, lgv, gn_cols)
    y_diff = _diff_attention(p_lat, p_ctx, diff_lambda[0], diff_subln_g[0][None, :],
                             tq=min(1024, l), tk=512)

    return _ffn(x, y_ret, y_diff, w_out[0].astype(BF16), vec(g1), vec(a2), vec(sh2),
                w_up[0].astype(BF16), conv_w[0], conv_b[0][None, :], w_down[0].astype(BF16),
                vec(g2), final_g[None, :], tm=min(512, l), tn=256)
```

```python
import functools
import math

import jax
import jax.numpy as jnp
import numpy as np
from jax import lax
from jax.experimental import pallas as pl
from jax.experimental.pallas import tpu as pltpu

F32 = jnp.float32
BF16 = jnp.bfloat16

GRID_W = 64
RET_HEADS = 4
RET_DK = 128
DIFF_HEADS = 4
DIFF_HD = 64
HEAD_W = 128
GROUP_W = 512
N_GROUPS = 7
RET_CHUNK = 256
ROPE_BASE = 10000.0
EPS = 1e-6
GN_EPS = 1e-5
N_MOD = 6
LAMBDA_INIT = 0.8 - 0.6 * math.exp(-0.3 * 0)
LOG2E = math.log2(math.e)
RET_LOOKAHEAD = 2
ATTN_LOOKAHEAD = 1

VMEM_LIMIT = 56 * 1024 * 1024


def _cparams(*sem):
    return pltpu.CompilerParams(dimension_semantics=sem, vmem_limit_bytes=VMEM_LIMIT)


def _silu(v):
    return v * (1.0 / (1.0 + jnp.exp(-v)))


def _split_dot(a, w):
    a_hi = a.astype(BF16)
    a_lo = (a - a_hi.astype(F32)).astype(BF16)
    w_hi = w.astype(BF16)
    w_lo = (w - w_hi.astype(F32)).astype(BF16)
    dot = functools.partial(jnp.dot, preferred_element_type=F32)
    return dot(a_hi, w_hi) + (dot(a_hi, w_lo) + dot(a_lo, w_hi))


def _mod_kernel(c_ref, w_ref, b_ref, o_ref):
    o_ref[...] = _split_dot(_silu(c_ref[...]), w_ref[...]) + b_ref[...]


def _modulation(cc, w_mod, b_mod):
    rows, d = cc.shape
    n = w_mod.shape[1]
    tn = 1024
    return pl.pallas_call(
        _mod_kernel,
        grid=(n // tn,),
        in_specs=[pl.BlockSpec((rows, d), lambda j: (0, 0)),
                  pl.BlockSpec((d, tn), lambda j: (0, j)),
                  pl.BlockSpec((1, tn), lambda j: (0, j))],
        out_specs=pl.BlockSpec((rows, tn), lambda j: (0, j)),
        out_shape=jax.ShapeDtypeStruct((rows, n), F32),
        compiler_params=_cparams("arbitrary"),
        name="mod",
    )(cc, w_mod, b_mod)


def _rope128(v, cos, sin):
    return v * cos + pltpu.roll(v, 64, axis=1) * sin


def _rope64(v, cos, sin_lo, sin_hi):
    return v * cos + pltpu.roll(v, 96, axis=1) * sin_lo + pltpu.roll(v, 32, axis=1) * sin_hi


def _inproj_kernel(x_ref, a_ref, s_ref, w_ref, t_ref, o_ref):
    x = x_ref[...]
    xn = x * lax.rsqrt(jnp.mean(x * x, axis=-1, keepdims=True) + EPS)
    h = (xn * a_ref[...] + s_ref[...]).astype(BF16)
    cos_r = t_ref[:, 0 * HEAD_W:1 * HEAD_W]
    sin_r = t_ref[:, 1 * HEAD_W:2 * HEAD_W]
    cos_d = t_ref[:, 2 * HEAD_W:3 * HEAD_W]
    sin_dl = t_ref[:, 3 * HEAD_W:4 * HEAD_W]
    sin_dh = t_ref[:, 4 * HEAD_W:5 * HEAD_W]
    for g in range(N_GROUPS):
        acc = jnp.dot(h, w_ref[:, g * GROUP_W:(g + 1) * GROUP_W], preferred_element_type=F32)
        for hd in range(GROUP_W // HEAD_W):
            v = acc[:, hd * HEAD_W:(hd + 1) * HEAD_W]
            if g == 0:
                v = _rope128(v, cos_r, sin_r)
            elif g == 1:
                v = _rope128(v, cos_r, sin_r) * (RET_DK ** -0.5)
            elif g == 4:
                v = _rope64(v, cos_d, sin_dl, sin_dh) * (DIFF_HD ** -0.5 * LOG2E)
            elif g == 5:
                v = _rope64(v, cos_d, sin_dl, sin_dh)
            c0 = g * GROUP_W + hd * HEAD_W
            o_ref[:, c0:c0 + HEAD_W] = v.astype(BF16)


def _inproj(x, a, s, w_in, tables, *, tm, name):
    b, lx, d = x.shape
    n_cols = w_in.shape[1]
    return pl.pallas_call(
        _inproj_kernel,
        grid=(lx // tm, b),
        in_specs=[pl.BlockSpec((None, tm, d), lambda t, i: (i, t, 0)),
                  pl.BlockSpec((None, 1, d), lambda t, i: (i, 0, 0)),
                  pl.BlockSpec((None, 1, d), lambda t, i: (i, 0, 0)),
                  pl.BlockSpec((d, n_cols), lambda t, i: (0, 0)),
                  pl.BlockSpec((tm, 5 * HEAD_W), lambda t, i: (t, 0))],
        out_specs=pl.BlockSpec((None, tm, n_cols), lambda t, i: (i, t, 0)),
        out_shape=jax.ShapeDtypeStruct((b, lx, n_cols), BF16),
        compiler_params=_cparams("arbitrary", "arbitrary"),
        name=name,
    )(x, a, s, w_in, tables)


def _ret_kernel(q_ref, k_ref, v_ref, g_ref, kc_ref, vc_ref, lg_ref, gn_ref, o_ref, u_ref, s_ref,
                *, n_lat, n_all):
    c = RET_CHUNK
    dk = HEAD_W
    row = lax.broadcasted_iota(jnp.int32, (c, c), 0).astype(F32)
    col = lax.broadcasted_iota(jnp.int32, (c, c), 1).astype(F32)
    tok = lax.broadcasted_iota(jnp.int32, (c, dk), 0).astype(F32)
    lg_f = lg_ref[0:1, :]
    lg_b = lg_ref[1:2, :]
    lg_f2 = jnp.concatenate([lg_f, lg_f], axis=1)
    lg_b2 = jnp.concatenate([lg_b, lg_b], axis=1)
    mask_f = jnp.where(col >= row, jnp.exp(jnp.maximum(col - row, 0.0) * lg_f2), 0.0)
    mask_b = jnp.where(row >= col, jnp.exp(jnp.maximum(row - col, 0.0) * lg_b2), 0.0)
    qdec_f = jnp.exp((tok + 1.0) * lg_f)
    qdec_b = jnp.exp((c - tok) * lg_b)
    kdec_f = jnp.exp((c - 1.0 - tok) * lg_f)
    kdec_b = jnp.exp(tok * lg_b)
    cdec_f = jnp.exp(c * lg_f)
    cdec_b = jnp.exp(c * lg_b)
    tn_dims = (((0,), (0,)), ((), ()))
    nt_dims = (((1,), (1,)), ((), ()))

    def rows(idx):
        return slice(idx * c, (idx + 1) * c)

    def kv_chunk(idx):
        if idx < n_lat:
            return k_ref[rows(idx), :], v_ref[rows(idx), :]
        return kc_ref[rows(idx - n_lat), :], vc_ref[rows(idx - n_lat), :]

    for idx in range(n_all):
        k, v = kv_chunk(idx)
        kf = k.astype(F32)
        kd = jnp.concatenate([(kf * kdec_f).astype(BF16), (kf * kdec_b).astype(BF16)], axis=1)
        u_ref[idx] = lax.dot_general(v, kd, tn_dims, preferred_element_type=F32)

    state = jnp.zeros((dk, dk), F32)
    for idx in [*range(n_lat, n_all), *range(n_lat)]:
        s_ref[idx, :, 0:dk] = state.astype(BF16)
        state = state * cdec_f + u_ref[idx, :, 0:dk]
    state = jnp.zeros((dk, dk), F32)
    for idx in reversed(range(n_all)):
        s_ref[idx, :, dk:2 * dk] = state.astype(BF16)
        state = state * cdec_b + u_ref[idx, :, dk:2 * dk]

    gn_f = jnp.broadcast_to(gn_ref[0], (dk, c))
    gn_b = jnp.broadcast_to(gn_ref[1], (dk, c))

    def group_norm_t(o, g):
        mu = jnp.mean(o, axis=0, keepdims=True)
        d = o - mu
        var = jnp.mean(d * d, axis=0, keepdims=True)
        return d * lax.rsqrt(var + GN_EPS) * g

    def scores_t(idx):
        return lax.dot_general(k_ref[rows(idx), :], q_ref[rows(idx), :], nt_dims,
                               preferred_element_type=F32)

    ahead = [scores_t(i) for i in range(min(RET_LOOKAHEAD, n_lat))]
    for idx in range(n_lat):
        a_t = ahead.pop(0)
        if idx + RET_LOOKAHEAD < n_lat:
            ahead.append(scores_t(idx + RET_LOOKAHEAD))
        v = v_ref[rows(idx), :]
        qf = q_ref[rows(idx), :].astype(F32)
        o_f = (lax.dot_general(v, (a_t * mask_f).astype(BF16), tn_dims, preferred_element_type=F32)
               + lax.dot_general(s_ref[idx, :, 0:dk], (qf * qdec_f).astype(BF16), nt_dims,
                                 preferred_element_type=F32))
        o_b = (lax.dot_general(v, (a_t * mask_b).astype(BF16), tn_dims, preferred_element_type=F32)
               + lax.dot_general(s_ref[idx, :, dk:2 * dk], (qf * qdec_b).astype(BF16), nt_dims,
                                 preferred_element_type=F32))
        n_t = group_norm_t(o_f, gn_f) + group_norm_t(o_b, gn_b)
        y = _silu(g_ref[rows(idx), :].astype(F32)) * n_t.T
        o_ref[rows(idx), :] = y.astype(BF16)


def _retention(p_lat, p_ctx, lgv, gn_cols):
    b, l_lat, _ = p_lat.shape
    l_ctx = p_ctx.shape[1]
    n_lat = l_lat // RET_CHUNK
    n_all = n_lat + l_ctx // RET_CHUNK
    col = lambda rows, g: pl.BlockSpec((None, rows, HEAD_W), lambda i, h: (i, 0, g * RET_HEADS + h))
    kern = functools.partial(_ret_kernel, n_lat=n_lat, n_all=n_all)
    return pl.pallas_call(
        kern,
        grid=(b, RET_HEADS),
        in_specs=[col(l_lat, 0), col(l_lat, 1), col(l_lat, 2), col(l_lat, 3),
                  col(l_ctx, 1), col(l_ctx, 2),
                  pl.BlockSpec((None, 2, HEAD_W), lambda i, h: (h, 0, 0)),
                  pl.BlockSpec((None, 2, HEAD_W, 1), lambda i, h: (h, 0, 0, 0))],
        out_specs=pl.BlockSpec((None, l_lat, HEAD_W), lambda i, h: (i, 0, h)),
        out_shape=jax.ShapeDtypeStruct((b, l_lat, RET_HEADS * HEAD_W), BF16),
        scratch_shapes=[pltpu.VMEM((n_all, HEAD_W, 2 * HEAD_W), F32),
                        pltpu.VMEM((n_all, HEAD_W, 2 * HEAD_W), BF16)],
        compiler_params=_cparams("arbitrary", "arbitrary"),
        name="ret",
    )(p_lat, p_lat, p_lat, p_lat, p_ctx, p_ctx, lgv, gn_cols)


PLAIN_EXP_MAX_LOG2 = 80.0
PLAIN_EXP_MAX_VALUE = 2.0 ** 30
ONES_ROWS = 16


def _attn_kernel(q_ref, k_ref, v_ref, kc_ref, vc_ref, lam_ref, g_ref, o_ref, vt_ref, kv_stat_ref,
                 *, tq, tk, l_lat, l_ctx):
    dv = HEAD_W
    l_all = l_lat + l_ctx
    spans = ([(False, o, min(tk, l_lat - o)) for o in range(0, l_lat, tk)]
             + [(True, o, min(tk, l_ctx - o)) for o in range(0, l_ctx, tk)])
    n_k = len(spans)

    chunks = ([(k_ref, v_ref, o, o) for o in range(0, l_lat, RET_CHUNK)]
              + [(kc_ref, vc_ref, o, l_lat + o) for o in range(0, l_ctx, RET_CHUNK)])

    @pl.when(pl.program_id(2) == 0)
    def _():
        kmax = vmax = jnp.zeros((1, HEAD_W), F32)
        for kr, vr, o, dst in chunks:
            vf = vr[o:o + RET_CHUNK, :].astype(F32)
            vt_ref[0:dv, dst:dst + RET_CHUNK] = vf.T.astype(BF16)
            vmax = jnp.maximum(vmax, jnp.max(jnp.abs(vf), axis=0, keepdims=True))
            kf = kr[o:o + RET_CHUNK, :].astype(F32)
            kmax = jnp.maximum(kmax, jnp.max(jnp.abs(kf), axis=0, keepdims=True))
        vt_ref[dv:dv + ONES_ROWS, :] = jnp.ones((ONES_ROWS, l_all), BF16)
        kv_stat_ref[0:1, :] = kmax
        kv_stat_ref[1:2, :] = vmax

    lane = lax.broadcasted_iota(jnp.int32, (tq, HEAD_W), 1)
    q = q_ref[...]
    zero = jnp.zeros_like(q)
    q1 = jnp.where(lane < DIFF_HD, q, zero)
    q2 = jnp.where(lane >= DIFF_HD, q, zero)
    nt_dims = (((1,), (1,)), ((), ()))

    r = lax.broadcasted_iota(jnp.int32, (HEAD_W, HEAD_W), 0)
    c = lax.broadcasted_iota(jnp.int32, (HEAD_W, HEAD_W), 1)
    same_map = jnp.where((r < DIFF_HD) == (c < DIFF_HD), 1.0, 0.0).astype(BF16)
    q_abs = (jnp.abs(q.astype(F32)) * kv_stat_ref[0:1, :]).astype(BF16)
    bound = jnp.dot(q_abs, same_map, preferred_element_type=F32)
    plain = jnp.logical_and(jnp.max(bound) <= PLAIN_EXP_MAX_LOG2,
                            jnp.max(kv_stat_ref[1:2, :]) <= PLAIN_EXP_MAX_VALUE)

    def scores(i):
        is_ctx, o, size = spans[i]
        kt = (kc_ref if is_ctx else k_ref)[o:o + size, :]
        return (lax.dot_general(kt, q1, nt_dims, preferred_element_type=F32),
                lax.dot_general(kt, q2, nt_dims, preferred_element_type=F32))

    def update_online(s, vt, state):
        m, acc = state
        m_new = jnp.maximum(m, jnp.max(s, axis=0, keepdims=True))
        alpha = jnp.exp2(m - m_new)
        pr = jnp.exp2(s - m_new).astype(BF16)
        return m_new, alpha * acc + jnp.dot(vt, pr, preferred_element_type=F32)

    def update_plain(s, vt, state):
        m, acc = state
        return m, acc + jnp.dot(vt, jnp.exp2(s).astype(BF16), preferred_element_type=F32)

    def attend(update):
        init = (jnp.full((1, tq), -jnp.inf, F32), jnp.zeros((dv + ONES_ROWS, tq), F32))
        st1 = st2 = init
        ahead = [scores(i) for i in range(min(ATTN_LOOKAHEAD, n_k))]
        for i in range(n_k):
            s1, s2 = ahead.pop(0)
            if i + ATTN_LOOKAHEAD < n_k:
                ahead.append(scores(i + ATTN_LOOKAHEAD))
            is_ctx, o, size = spans[i]
            start = (l_lat if is_ctx else 0) + o
            vt = vt_ref[:, start:start + size]
            st1 = update(s1, vt, st1)
            st2 = update(s2, vt, st2)
        (_, acc1), (_, acc2) = st1, st2

        lp = lam_ref[...]
        lam = (jnp.exp(jnp.sum(lp[0:1, :] * lp[1:2, :], axis=-1, keepdims=True))
               - jnp.exp(jnp.sum(lp[2:3, :] * lp[3:4, :], axis=-1, keepdims=True)) + LAMBDA_INIT)
        o = acc1[0:dv] / acc1[dv:dv + 1] - lam * (acc2[0:dv] / acc2[dv:dv + 1])
        n = o * lax.rsqrt(jnp.mean(o * o, axis=0, keepdims=True) + EPS)
        o_ref[...] = (n.T * g_ref[...] * (1.0 - LAMBDA_INIT)).astype(BF16)

    @pl.when(plain)
    def _():
        attend(update_plain)

    @pl.when(jnp.logical_not(plain))
    def _():
        attend(update_online)


def _diff_attention(p_lat, p_ctx, lam_par, subln_g, *, tq, tk):
    b, l_lat, _ = p_lat.shape
    l_ctx = p_ctx.shape[1]
    kern = functools.partial(_attn_kernel, tq=tq, tk=tk, l_lat=l_lat, l_ctx=l_ctx)
    col = lambda rows, g: pl.BlockSpec((None, rows, HEAD_W), lambda i, h, j: (i, 0, g * DIFF_HEADS + h))
    return pl.pallas_call(
        kern,
        grid=(b, DIFF_HEADS, l_lat // tq),
        in_specs=[pl.BlockSpec((None, tq, HEAD_W), lambda i, h, j: (i, j, 4 * DIFF_HEADS + h)),
                  col(l_lat, 5), col(l_lat, 6), col(l_ctx, 5), col(l_ctx, 6),
                  pl.BlockSpec((4, DIFF_HD), lambda i, h, j: (0, 0)),
                  pl.BlockSpec((1, HEAD_W), lambda i, h, j: (0, h))],
        out_specs=pl.BlockSpec((None, tq, HEAD_W), lambda i, h, j: (i, j, h)),
        out_shape=jax.ShapeDtypeStruct((b, l_lat, DIFF_HEADS * HEAD_W), BF16),
        scratch_shapes=[pltpu.VMEM((HEAD_W + ONES_ROWS, l_lat + l_ctx), BF16),
                        pltpu.VMEM((8, HEAD_W), F32)],
        compiler_params=_cparams("arbitrary", "arbitrary", "arbitrary"),
        name="attn",
    )(p_lat, p_lat, p_lat, p_ctx, p_ctx, lam_par, subln_g)


HALO = 16


def _ffn_kernel(xp_ref, x_ref, xn_ref, rp_ref, r_ref, rn_ref, dp_ref, d_ref, dn_ref,
                wo_ref, g1_ref, a2_ref, s2_ref, wu_ref, cw_ref, cb_ref, wd_ref, g2_ref, fg_ref,
                o_ref, act_ref, *, tm, d_ff, tn):
    t = pl.program_id(1)
    nt = pl.num_programs(1)
    rows = tm + 2 * HALO
    half = r_ref.shape[-1]
    xs = jnp.concatenate([xp_ref[...], x_ref[...], xn_ref[...]], axis=0)
    yr = jnp.concatenate([rp_ref[...], r_ref[...], rn_ref[...]], axis=0)
    yd = jnp.concatenate([dp_ref[...], d_ref[...], dn_ref[...]], axis=0)
    y = (jnp.dot(yr, wo_ref[:half, :], preferred_element_type=F32)
         + jnp.dot(yd, wo_ref[half:, :], preferred_element_type=F32))
    x1s = xs + g1_ref[...] * y
    h2 = (x1s * lax.rsqrt(jnp.mean(x1s * x1s, axis=-1, keepdims=True) + EPS)) * a2_ref[...] + s2_ref[...]
    row = lax.broadcasted_iota(jnp.int32, (rows, 1), 0)
    inside = ((row >= HALO) | (t > 0)) & ((row < tm + HALO) | (t < nt - 1))
    hh = jnp.where(inside, h2, 0.0).astype(BF16)
    for j in range(d_ff // tn):
        for part in range(2):
            c0 = part * d_ff + j * tn
            u = jnp.dot(hh, wu_ref[:, c0:c0 + tn], preferred_element_type=F32)
            w = cw_ref[:, c0:c0 + tn]
            conv = (pltpu.roll(u, 1, axis=0)[HALO:HALO + tm] * w[0:1, :]
                    + u[HALO:HALO + tm] * w[1:2, :]
                    + pltpu.roll(u, rows - 1, axis=0)[HALO:HALO + tm] * w[2:3, :]
                    + cb_ref[:, c0:c0 + tn])
            if part == 0:
                gate = _silu(conv)
            else:
                act_ref[:, j * tn:(j + 1) * tn] = (gate * conv).astype(BF16)
    f = jnp.dot(act_ref[...], wd_ref[...], preferred_element_type=F32)
    x2 = x1s[HALO:HALO + tm, :] + g2_ref[...] * f
    o_ref[...] = x2 * lax.rsqrt(jnp.mean(x2 * x2, axis=-1, keepdims=True) + EPS) * fg_ref[...]


def _ffn(x, y_ret, y_diff, w_out, g1, a2, s2, w_up, conv_w, conv_b, w_down, g2, final_g, *, tm, tn):
    b, l, d = x.shape
    half = y_ret.shape[-1]
    d_ff = w_down.shape[0]
    nh = tm // HALO
    n_halo_blocks = l // HALO
    kern = functools.partial(_ffn_kernel, tm=tm, d_ff=d_ff, tn=tn)

    def halo_specs(w):
        return [pl.BlockSpec((None, HALO, w), lambda i, t: (i, jnp.maximum(t * nh - 1, 0), 0)),
                pl.BlockSpec((None, tm, w), lambda i, t: (i, t, 0)),
                pl.BlockSpec((None, HALO, w), lambda i, t: (i, jnp.minimum((t + 1) * nh, n_halo_blocks - 1), 0))]

    full = lambda a: pl.BlockSpec(a.shape, lambda i, t: (0,) * a.ndim, pipeline_mode=pl.Buffered(1))
    vec = pl.BlockSpec((None, 1, d), lambda i, t: (i, 0, 0))
    return pl.pallas_call(
        kern,
        grid=(b, l // tm),
        in_specs=[*halo_specs(d), *halo_specs(half), *halo_specs(half),
                  full(w_out), vec, vec, vec,
                  full(w_up), full(conv_w), full(conv_b), full(w_down), vec, full(final_g)],
        out_specs=pl.BlockSpec((None, tm, d), lambda i, t: (i, t, 0)),
        out_shape=jax.ShapeDtypeStruct((b, l, d), F32),
        scratch_shapes=[pltpu.VMEM((tm, d_ff), BF16)],
        compiler_params=_cparams("arbitrary", "arbitrary"),
        name="ffn",
    )(x, x, x, y_ret, y_ret, y_ret, y_diff, y_diff, y_diff, w_out, g1, a2, s2,
      w_up, conv_w, conv_b, w_down, g2, final_g)


def _rope_tables(l):
    pos = np.arange(l)
    row = (pos // GRID_W).astype(np.float64)[:, None]
    col = (pos % GRID_W).astype(np.float64)[:, None]

    def angles(head_dim):
        n_freq = head_dim // 4
        inv = ROPE_BASE ** (-np.arange(n_freq, dtype=np.float64) / n_freq)
        return np.concatenate([row * inv, col * inv], axis=-1)

    ar = angles(RET_DK)
    ad = angles(DIFF_HD)
    cr, sr = np.cos(ar), np.sin(ar)
    cd, sd = np.cos(ad), np.sin(ad)
    zd = np.zeros_like(sd)
    return jnp.asarray(np.concatenate([
        cr, cr, -sr, sr,
        cd, cd, cd, cd,
        -sd, zd, -sd, zd,
        zd, sd, zd, sd], axis=-1), F32)


def _identity_tables(l):
    one = np.ones((l, HEAD_W), np.float32)
    zero = np.zeros((l, HEAD_W), np.float32)
    return jnp.asarray(np.concatenate([one, zero, one, zero, zero], axis=-1))


def kernel(x, c, ctx, c_ctx, w_mod, b_mod, norm1_g, w_in, ret_decay_logit, ret_gn_g, diff_lambda,
           diff_subln_g, w_out, norm2_g, w_up, conv_w, conv_b, w_down, final_g):
    b, l, d = x.shape
    l_ctx = ctx.shape[1]

    rows = -(-(b + 1) // 8) * 8
    cc = jnp.zeros((rows, d), F32).at[:b].set(c).at[b].set(c_ctx)
    m = _modulation(cc, w_mod[0], b_mod[0][None, :])
    sh1, sc1, g1, sh2, sc2, g2 = [m[:, i * d:(i + 1) * d] for i in range(N_MOD)]
    a1 = norm1_g[0][None, :] * (1.0 + sc1)
    a2 = norm2_g[0][None, :] * (1.0 + sc2)
    vec = lambda v: v[:b, None, :]
    ctx_vec = lambda v: jnp.broadcast_to(v[b][None, None, :], (b, 1, d))

    w_in_b = w_in[0].astype(BF16)
    p_lat = _inproj(x, vec(a1), vec(sh1), w_in_b, _rope_tables(l), tm=min(512, l), name="inproj_lat")
    p_ctx = _inproj(ctx, ctx_vec(a1), ctx_vec(sh1), w_in_b, _identity_tables(l_ctx), tm=l_ctx,
                    name="inproj_ctx")

    lg = jax.nn.log_sigmoid(ret_decay_logit[0].astype(F32))
    lgv = jnp.broadcast_to(lg.T[:, :, None], (RET_HEADS, 2, HEAD_W))
    gn_cols = ret_gn_g[0].reshape(2, RET_HEADS, HEAD_W).transpose(1, 0, 2)[..., None]
    y_ret = _retention(p_lat, p_ctx, lgv, gn_cols)
    y_diff = _diff_attention(p_lat, p_ctx, diff_lambda[0], diff_subln_g[0][None, :],
                             tq=min(2048, l), tk=512)

    return _ffn(x, y_ret, y_diff, w_out[0].astype(BF16), vec(g1), vec(a2), vec(sh2),
                w_up[0].astype(BF16), conv_w[0], conv_b[0][None, :], w_down[0].astype(BF16),
                vec(g2), final_g[None, :], tm=min(512, l), tn=256)
```

```python
import functools
import math

import jax
import jax.numpy as jnp
import numpy as np
from jax import lax
from jax.experimental import pallas as pl
from jax.experimental.pallas import tpu as pltpu

F32 = jnp.float32
BF16 = jnp.bfloat16

GRID_W = 64
RET_HEADS = 4
RET_DK = 128
DIFF_HEADS = 4
DIFF_HD = 64
HEAD_W = 128
GROUP_W = 512
N_GROUPS = 7
RET_CHUNK = 256
ROPE_BASE = 10000.0
EPS = 1e-6
GN_EPS = 1e-5
N_MOD = 6
LAMBDA_INIT = 0.8 - 0.6 * math.exp(-0.3 * 0)
LOG2E = math.log2(math.e)
RET_LOOKAHEAD = 2
ATTN_LOOKAHEAD = 1

VMEM_LIMIT = 56 * 1024 * 1024


def _cparams(*sem):
    return pltpu.CompilerParams(dimension_semantics=sem, vmem_limit_bytes=VMEM_LIMIT)


def _silu(v):
    return v * (1.0 / (1.0 + jnp.exp(-v)))


def _split_dot(a, w):
    a_hi = a.astype(BF16)
    a_lo = (a - a_hi.astype(F32)).astype(BF16)
    w_hi = w.astype(BF16)
    w_lo = (w - w_hi.astype(F32)).astype(BF16)
    dot = functools.partial(jnp.dot, preferred_element_type=F32)
    return dot(a_hi, w_hi) + (dot(a_hi, w_lo) + dot(a_lo, w_hi))


def _mod_kernel(c_ref, w_ref, b_ref, o_ref):
    o_ref[...] = _split_dot(_silu(c_ref[...]), w_ref[...]) + b_ref[...]


def _modulation(cc, w_mod, b_mod):
    rows, d = cc.shape
    n = w_mod.shape[1]
    tn = 1024
    return pl.pallas_call(
        _mod_kernel,
        grid=(n // tn,),
        in_specs=[pl.BlockSpec((rows, d), lambda j: (0, 0)),
                  pl.BlockSpec((d, tn), lambda j: (0, j)),
                  pl.BlockSpec((1, tn), lambda j: (0, j))],
        out_specs=pl.BlockSpec((rows, tn), lambda j: (0, j)),
        out_shape=jax.ShapeDtypeStruct((rows, n), F32),
        compiler_params=_cparams("arbitrary"),
        name="mod",
    )(cc, w_mod, b_mod)


def _rope128(v, cos, sin):
    return v * cos + pltpu.roll(v, 64, axis=1) * sin


def _rope64(v, cos, sin_lo, sin_hi):
    return v * cos + pltpu.roll(v, 96, axis=1) * sin_lo + pltpu.roll(v, 32, axis=1) * sin_hi


def _inproj_kernel(x_ref, a_ref, s_ref, w_ref, t_ref, o_ref):
    x = x_ref[...]
    xn = x * lax.rsqrt(jnp.mean(x * x, axis=-1, keepdims=True) + EPS)
    h = (xn * a_ref[...] + s_ref[...]).astype(BF16)
    cos_r = t_ref[:, 0 * HEAD_W:1 * HEAD_W]
    sin_r = t_ref[:, 1 * HEAD_W:2 * HEAD_W]
    cos_d = t_ref[:, 2 * HEAD_W:3 * HEAD_W]
    sin_dl = t_ref[:, 3 * HEAD_W:4 * HEAD_W]
    sin_dh = t_ref[:, 4 * HEAD_W:5 * HEAD_W]
    for g in range(N_GROUPS):
        acc = jnp.dot(h, w_ref[:, g * GROUP_W:(g + 1) * GROUP_W], preferred_element_type=F32)
        for hd in range(GROUP_W // HEAD_W):
            v = acc[:, hd * HEAD_W:(hd + 1) * HEAD_W]
            if g == 0:
                v = _rope128(v, cos_r, sin_r)
            elif g == 1:
                v = _rope128(v, cos_r, sin_r) * (RET_DK ** -0.5)
            elif g == 4:
                v = _rope64(v, cos_d, sin_dl, sin_dh) * (DIFF_HD ** -0.5 * LOG2E)
            elif g == 5:
                v = _rope64(v, cos_d, sin_dl, sin_dh)
            c0 = g * GROUP_W + hd * HEAD_W
            o_ref[:, c0:c0 + HEAD_W] = v.astype(BF16)


def _inproj(x, a, s, w_in, tables, *, tm, name):
    b, lx, d = x.shape
    n_cols = w_in.shape[1]
    return pl.pallas_call(
        _inproj_kernel,
        grid=(lx // tm, b),
        in_specs=[pl.BlockSpec((None, tm, d), lambda t, i: (i, t, 0)),
                  pl.BlockSpec((None, 1, d), lambda t, i: (i, 0, 0)),
                  pl.BlockSpec((None, 1, d), lambda t, i: (i, 0, 0)),
                  pl.BlockSpec((d, n_cols), lambda t, i: (0, 0)),
                  pl.BlockSpec((tm, 5 * HEAD_W), lambda t, i: (t, 0))],
        out_specs=pl.BlockSpec((None, tm, n_cols), lambda t, i: (i, t, 0)),
        out_shape=jax.ShapeDtypeStruct((b, lx, n_cols), BF16),
        compiler_params=_cparams("arbitrary", "arbitrary"),
        name=name,
    )(x, a, s, w_in, tables)


def _ret_kernel(q_ref, k_ref, v_ref, g_ref, kc_ref, vc_ref, lg_ref, gn_ref, o_ref, u_ref, s_ref,
                *, n_lat, n_all):
    c = RET_CHUNK
    dk = HEAD_W
    row = lax.broadcasted_iota(jnp.int32, (c, c), 0).astype(F32)
    col = lax.broadcasted_iota(jnp.int32, (c, c), 1).astype(F32)
    tok = lax.broadcasted_iota(jnp.int32, (c, dk), 0).astype(F32)
    lg_f = lg_ref[0:1, :]
    lg_b = lg_ref[1:2, :]
    lg_f2 = jnp.concatenate([lg_f, lg_f], axis=1)
    lg_b2 = jnp.concatenate([lg_b, lg_b], axis=1)
    mask_f = jnp.where(col >= row, jnp.exp(jnp.maximum(col - row, 0.0) * lg_f2), 0.0)
    mask_b = jnp.where(row >= col, jnp.exp(jnp.maximum(row - col, 0.0) * lg_b2), 0.0)
    qdec_f = jnp.exp((tok + 1.0) * lg_f)
    qdec_b = jnp.exp((c - tok) * lg_b)
    kdec_f = jnp.exp((c - 1.0 - tok) * lg_f)
    kdec_b = jnp.exp(tok * lg_b)
    cdec_f = jnp.exp(c * lg_f)
    cdec_b = jnp.exp(c * lg_b)
    tn_dims = (((0,), (0,)), ((), ()))
    nt_dims = (((1,), (1,)), ((), ()))

    def rows(idx):
        return slice(idx * c, (idx + 1) * c)

    def kv_chunk(idx):
        if idx < n_lat:
            return k_ref[rows(idx), :], v_ref[rows(idx), :]
        return kc_ref[rows(idx - n_lat), :], vc_ref[rows(idx - n_lat), :]

    for idx in range(n_all):
        k, v = kv_chunk(idx)
        kf = k.astype(F32)
        kd = jnp.concatenate([(kf * kdec_f).astype(BF16), (kf * kdec_b).astype(BF16)], axis=1)
        u_ref[idx] = lax.dot_general(v, kd, tn_dims, preferred_element_type=F32)

    state = jnp.zeros((dk, dk), F32)
    for idx in [*range(n_lat, n_all), *range(n_lat)]:
        s_ref[idx, :, 0:dk] = state.astype(BF16)
        state = state * cdec_f + u_ref[idx, :, 0:dk]
    state = jnp.zeros((dk, dk), F32)
    for idx in reversed(range(n_all)):
        s_ref[idx, :, dk:2 * dk] = state.astype(BF16)
        state = state * cdec_b + u_ref[idx, :, dk:2 * dk]

    gn_f = jnp.broadcast_to(gn_ref[0], (dk, c))
    gn_b = jnp.broadcast_to(gn_ref[1], (dk, c))

    def group_norm_t(o, g):
        mu = jnp.mean(o, axis=0, keepdims=True)
        d = o - mu
        var = jnp.mean(d * d, axis=0, keepdims=True)
        return d * lax.rsqrt(var + GN_EPS) * g

    def scores_t(idx):
        return lax.dot_general(k_ref[rows(idx), :], q_ref[rows(idx), :], nt_dims,
                               preferred_element_type=F32)

    ahead = [scores_t(i) for i in range(min(RET_LOOKAHEAD, n_lat))]
    for idx in range(n_lat):
        a_t = ahead.pop(0)
        if idx + RET_LOOKAHEAD < n_lat:
            ahead.append(scores_t(idx + RET_LOOKAHEAD))
        v = v_ref[rows(idx), :]
        qf = q_ref[rows(idx), :].astype(F32)
        o_f = (lax.dot_general(v, (a_t * mask_f).astype(BF16), tn_dims, preferred_element_type=F32)
               + lax.dot_general(s_ref[idx, :, 0:dk], (qf * qdec_f).astype(BF16), nt_dims,
                                 preferred_element_type=F32))
        o_b = (lax.dot_general(v, (a_t * mask_b).astype(BF16), tn_dims, preferred_element_type=F32)
               + lax.dot_general(s_ref[idx, :, dk:2 * dk], (qf * qdec_b).astype(BF16), nt_dims,
                                 preferred_element_type=F32))
        n_t = group_norm_t(o_f, gn_f) + group_norm_t(o_b, gn_b)
        y = _silu(g_ref[rows(idx), :].astype(F32)) * n_t.T
        o_ref[rows(idx), :] = y.astype(BF16)


def _retention(p_lat, p_ctx, lgv, gn_cols):
    b, l_lat, _ = p_lat.shape
    l_ctx = p_ctx.shape[1]
    n_lat = l_lat // RET_CHUNK
    n_all = n_lat + l_ctx // RET_CHUNK
    col = lambda rows, g: pl.BlockSpec((None, rows, HEAD_W), lambda i, h: (i, 0, g * RET_HEADS + h))
    kern = functools.partial(_ret_kernel, n_lat=n_lat, n_all=n_all)
    return pl.pallas_call(
        kern,
        grid=(b, RET_HEADS),
        in_specs=[col(l_lat, 0), col(l_lat, 1), col(l_lat, 2), col(l_lat, 3),
                  col(l_ctx, 1), col(l_ctx, 2),
                  pl.BlockSpec((None, 2, HEAD_W), lambda i, h: (h, 0, 0)),
                  pl.BlockSpec((None, 2, HEAD_W, 1), lambda i, h: (h, 0, 0, 0))],
        out_specs=pl.BlockSpec((None, l_lat, HEAD_W), lambda i, h: (i, 0, h)),
        out_shape=jax.ShapeDtypeStruct((b, l_lat, RET_HEADS * HEAD_W), BF16),
        scratch_shapes=[pltpu.VMEM((n_all, HEAD_W, 2 * HEAD_W), F32),
                        pltpu.VMEM((n_all, HEAD_W, 2 * HEAD_W), BF16)],
        compiler_params=_cparams("arbitrary", "arbitrary"),
        name="ret",
    )(p_lat, p_lat, p_lat, p_lat, p_ctx, p_ctx, lgv, gn_cols)


PLAIN_EXP_MAX_LOG2 = 80.0
PLAIN_EXP_MAX_VALUE = 2.0 ** 30
ONES_ROWS = 16


def _attn_kernel(q_ref, k_ref, v_ref, kc_ref, vc_ref, lam_ref, g_ref, o_ref, vt_ref, kv_stat_ref,
                 *, tq, tk, l_lat, l_ctx):
    dv = HEAD_W
    l_all = l_lat + l_ctx
    spans = ([(False, o, min(tk, l_lat - o)) for o in range(0, l_lat, tk)]
             + [(True, o, min(tk, l_ctx - o)) for o in range(0, l_ctx, tk)])
    n_k = len(spans)

    chunks = ([(k_ref, v_ref, o, o) for o in range(0, l_lat, RET_CHUNK)]
              + [(kc_ref, vc_ref, o, l_lat + o) for o in range(0, l_ctx, RET_CHUNK)])

    @pl.when(pl.program_id(2) == 0)
    def _():
        kmax = vmax = jnp.zeros((1, HEAD_W), F32)
        for kr, vr, o, dst in chunks:
            vf = vr[o:o + RET_CHUNK, :].astype(F32)
            vt_ref[0:dv, dst:dst + RET_CHUNK] = vf.T.astype(BF16)
            vmax = jnp.maximum(vmax, jnp.max(jnp.abs(vf), axis=0, keepdims=True))
            kf = kr[o:o + RET_CHUNK, :].astype(F32)
            kmax = jnp.maximum(kmax, jnp.max(jnp.abs(kf), axis=0, keepdims=True))
        vt_ref[dv:dv + ONES_ROWS, :] = jnp.ones((ONES_ROWS, l_all), BF16)
        kv_stat_ref[0:1, :] = kmax
        kv_stat_ref[1:2, :] = vmax

    lane = lax.broadcasted_iota(jnp.int32, (tq, HEAD_W), 1)
    q = q_ref[...]
    zero = jnp.zeros_like(q)
    q1 = jnp.where(lane < DIFF_HD, q, zero)
    q2 = jnp.where(lane >= DIFF_HD, q, zero)
    nt_dims = (((1,), (1,)), ((), ()))

    r = lax.broadcasted_iota(jnp.int32, (HEAD_W, HEAD_W), 0)
    c = lax.broadcasted_iota(jnp.int32, (HEAD_W, HEAD_W), 1)
    same_map = jnp.where((r < DIFF_HD) == (c < DIFF_HD), 1.0, 0.0).astype(BF16)
    q_abs = (jnp.abs(q.astype(F32)) * kv_stat_ref[0:1, :]).astype(BF16)
    bound = jnp.dot(q_abs, same_map, preferred_element_type=F32)
    plain = jnp.logical_and(jnp.max(bound) <= PLAIN_EXP_MAX_LOG2,
                            jnp.max(kv_stat_ref[1:2, :]) <= PLAIN_EXP_MAX_VALUE)

    def scores(i):
        is_ctx, o, size = spans[i]
        kt = (kc_ref if is_ctx else k_ref)[o:o + size, :]
        return (lax.dot_general(kt, q1, nt_dims, preferred_element_type=F32),
                lax.dot_general(kt, q2, nt_dims, preferred_element_type=F32))

    def update_online(s, vt, state):
        m, acc = state
        m_new = jnp.maximum(m, jnp.max(s, axis=0, keepdims=True))
        alpha = jnp.exp2(m - m_new)
        pr = jnp.exp2(s - m_new).astype(BF16)
        return m_new, alpha * acc + jnp.dot(vt, pr, preferred_element_type=F32)

    def update_plain(s, vt, state):
        l, acc = state
        pr = jnp.exp2(s)
        return (l + jnp.sum(pr, axis=0, keepdims=True),
                acc + jnp.dot(vt[0:dv], pr.astype(BF16), preferred_element_type=F32))

    def attend(update, init):
        st1 = st2 = init
        ahead = [scores(i) for i in range(min(ATTN_LOOKAHEAD, n_k))]
        for i in range(n_k):
            s1, s2 = ahead.pop(0)
            if i + ATTN_LOOKAHEAD < n_k:
                ahead.append(scores(i + ATTN_LOOKAHEAD))
            is_ctx, o, size = spans[i]
            start = (l_lat if is_ctx else 0) + o
            vt = vt_ref[:, start:start + size]
            st1 = update(s1, vt, st1)
            st2 = update(s2, vt, st2)
        return st1, st2

    def finish(o1, l1, o2, l2):
        lp = lam_ref[...]
        lam = (jnp.exp(jnp.sum(lp[0:1, :] * lp[1:2, :], axis=-1, keepdims=True))
               - jnp.exp(jnp.sum(lp[2:3, :] * lp[3:4, :], axis=-1, keepdims=True)) + LAMBDA_INIT)
        o = o1 / l1 - lam * (o2 / l2)
        n = o * lax.rsqrt(jnp.mean(o * o, axis=0, keepdims=True) + EPS)
        o_ref[...] = (n.T * g_ref[...] * (1.0 - LAMBDA_INIT)).astype(BF16)

    @pl.when(plain)
    def _():
        (l1, acc1), (l2, acc2) = attend(
            update_plain, (jnp.zeros((1, tq), F32), jnp.zeros((dv, tq), F32)))
        finish(acc1, l1, acc2, l2)

    @pl.when(jnp.logical_not(plain))
    def _():
        (_, acc1), (_, acc2) = attend(
            update_online, (jnp.full((1, tq), -jnp.inf, F32), jnp.zeros((dv + ONES_ROWS, tq), F32)))
        finish(acc1[0:dv], acc1[dv:dv + 1], acc2[0:dv], acc2[dv:dv + 1])


def _diff_attention(p_lat, p_ctx, lam_par, subln_g, *, tq, tk):
    b, l_lat, _ = p_lat.shape
    l_ctx = p_ctx.shape[1]
    kern = functools.partial(_attn_kernel, tq=tq, tk=tk, l_lat=l_lat, l_ctx=l_ctx)
    col = lambda rows, g: pl.BlockSpec((None, rows, HEAD_W), lambda i, h, j: (i, 0, g * DIFF_HEADS + h))
    return pl.pallas_call(
        kern,
        grid=(b, DIFF_HEADS, l_lat // tq),
        in_specs=[pl.BlockSpec((None, tq, HEAD_W), lambda i, h, j: (i, j, 4 * DIFF_HEADS + h)),
                  col(l_lat, 5), col(l_lat, 6), col(l_ctx, 5), col(l_ctx, 6),
                  pl.BlockSpec((4, DIFF_HD), lambda i, h, j: (0, 0)),
                  pl.BlockSpec((1, HEAD_W), lambda i, h, j: (0, h))],
        out_specs=pl.BlockSpec((None, tq, HEAD_W), lambda i, h, j: (i, j, h)),
        out_shape=jax.ShapeDtypeStruct((b, l_lat, DIFF_HEADS * HEAD_W), BF16),
        scratch_shapes=[pltpu.VMEM((HEAD_W + ONES_ROWS, l_lat + l_ctx), BF16),
                        pltpu.VMEM((8, HEAD_W), F32)],
        compiler_params=_cparams("arbitrary", "arbitrary", "arbitrary"),
        name="attn",
    )(p_lat, p_lat, p_lat, p_ctx, p_ctx, lam_par, subln_g)


HALO = 16


def _ffn_kernel(xp_ref, x_ref, xn_ref, rp_ref, r_ref, rn_ref, dp_ref, d_ref, dn_ref,
                wo_ref, g1_ref, a2_ref, s2_ref, wu_ref, cw_ref, cb_ref, wd_ref, g2_ref, fg_ref,
                o_ref, act_ref, *, tm, d_ff, tn):
    t = pl.program_id(1)
    nt = pl.num_programs(1)
    rows = tm + 2 * HALO
    half = r_ref.shape[-1]
    xs = jnp.concatenate([xp_ref[...], x_ref[...], xn_ref[...]], axis=0)
    yr = jnp.concatenate([rp_ref[...], r_ref[...], rn_ref[...]], axis=0)
    yd = jnp.concatenate([dp_ref[...], d_ref[...], dn_ref[...]], axis=0)
    y = (jnp.dot(yr, wo_ref[:half, :], preferred_element_type=F32)
         + jnp.dot(yd, wo_ref[half:, :], preferred_element_type=F32))
    x1s = xs + g1_ref[...] * y
    h2 = (x1s * lax.rsqrt(jnp.mean(x1s * x1s, axis=-1, keepdims=True) + EPS)) * a2_ref[...] + s2_ref[...]
    row = lax.broadcasted_iota(jnp.int32, (rows, 1), 0)
    inside = ((row >= HALO) | (t > 0)) & ((row < tm + HALO) | (t < nt - 1))
    hh = jnp.where(inside, h2, 0.0).astype(BF16)
    for j in range(d_ff // tn):
        for part in range(2):
            c0 = part * d_ff + j * tn
            u = jnp.dot(hh, wu_ref[:, c0:c0 + tn], preferred_element_type=F32)
            w = cw_ref[:, c0:c0 + tn]
            conv = (pltpu.roll(u, 1, axis=0)[HALO:HALO + tm] * w[0:1, :]
                    + u[HALO:HALO + tm] * w[1:2, :]
                    + pltpu.roll(u, rows - 1, axis=0)[HALO:HALO + tm] * w[2:3, :]
                    + cb_ref[:, c0:c0 + tn])
            if part == 0:
                gate = _silu(conv)
            else:
                act_ref[:, j * tn:(j + 1) * tn] = (gate * conv).astype(BF16)
    f = jnp.dot(act_ref[...], wd_ref[...], preferred_element_type=F32)
    x2 = x1s[HALO:HALO + tm, :] + g2_ref[...] * f
    o_ref[...] = x2 * lax.rsqrt(jnp.mean(x2 * x2, axis=-1, keepdims=True) + EPS) * fg_ref[...]


def _ffn(x, y_ret, y_diff, w_out, g1, a2, s2, w_up, conv_w, conv_b, w_down, g2, final_g, *, tm, tn):
    b, l, d = x.shape
    half = y_ret.shape[-1]
    d_ff = w_down.shape[0]
    nh = tm // HALO
    n_halo_blocks = l // HALO
    kern = functools.partial(_ffn_kernel, tm=tm, d_ff=d_ff, tn=tn)

    def halo_specs(w):
        return [pl.BlockSpec((None, HALO, w), lambda i, t: (i, jnp.maximum(t * nh - 1, 0), 0)),
                pl.BlockSpec((None, tm, w), lambda i, t: (i, t, 0)),
                pl.BlockSpec((None, HALO, w), lambda i, t: (i, jnp.minimum((t + 1) * nh, n_halo_blocks - 1), 0))]

    full = lambda a: pl.BlockSpec(a.shape, lambda i, t: (0,) * a.ndim, pipeline_mode=pl.Buffered(1))
    vec = pl.BlockSpec((None, 1, d), lambda i, t: (i, 0, 0))
    return pl.pallas_call(
        kern,
        grid=(b, l // tm),
        in_specs=[*halo_specs(d), *halo_specs(half), *halo_specs(half),
                  full(w_out), vec, vec, vec,
                  full(w_up), full(conv_w), full(conv_b), full(w_down), vec, full(final_g)],
        out_specs=pl.BlockSpec((None, tm, d), lambda i, t: (i, t, 0)),
        out_shape=jax.ShapeDtypeStruct((b, l, d), F32),
        scratch_shapes=[pltpu.VMEM((tm, d_ff), BF16)],
        compiler_params=_cparams("arbitrary", "arbitrary"),
        name="ffn",
    )(x, x, x, y_ret, y_ret, y_ret, y_diff, y_diff, y_diff, w_out, g1, a2, s2,
      w_up, conv_w, conv_b, w_down, g2, final_g)


def _rope_tables(l):
    pos = np.arange(l)
    row = (pos // GRID_W).astype(np.float64)[:, None]
    col = (pos % GRID_W).astype(np.float64)[:, None]

    def angles(head_dim):
        n_freq = head_dim // 4
        inv = ROPE_BASE ** (-np.arange(n_freq, dtype=np.float64) / n_freq)
        return np.concatenate([row * inv, col * inv], axis=-1)

    ar = angles(RET_DK)
    ad = angles(DIFF_HD)
    cr, sr = np.cos(ar), np.sin(ar)
    cd, sd = np.cos(ad), np.sin(ad)
    zd = np.zeros_like(sd)
    return jnp.asarray(np.concatenate([
        cr, cr, -sr, sr,
        cd, cd, cd, cd,
        -sd, zd, -sd, zd,
        zd, sd, zd, sd], axis=-1), F32)


def _identity_tables(l):
    one = np.ones((l, HEAD_W), np.float32)
    zero = np.zeros((l, HEAD_W), np.float32)
    return jnp.asarray(np.concatenate([one, zero, one, zero, zero], axis=-1))


def kernel(x, c, ctx, c_ctx, w_mod, b_mod, norm1_g, w_in, ret_decay_logit, ret_gn_g, diff_lambda,
           diff_subln_g, w_out, norm2_g, w_up, conv_w, conv_b, w_down, final_g):
    b, l, d = x.shape
    l_ctx = ctx.shape[1]

    rows = -(-(b + 1) // 8) * 8
    cc = jnp.zeros((rows, d), F32).at[:b].set(c).at[b].set(c_ctx)
    m = _modulation(cc, w_mod[0], b_mod[0][None, :])
    sh1, sc1, g1, sh2, sc2, g2 = [m[:, i * d:(i + 1) * d] for i in range(N_MOD)]
    a1 = norm1_g[0][None, :] * (1.0 + sc1)
    a2 = norm2_g[0][None, :] * (1.0 + sc2)
    vec = lambda v: v[:b, None, :]
    ctx_vec = lambda v: jnp.broadcast_to(v[b][None, None, :], (b, 1, d))

    w_in_b = w_in[0].astype(BF16)
    p_lat = _inproj(x, vec(a1), vec(sh1), w_in_b, _rope_tables(l), tm=min(512, l), name="inproj_lat")
    p_ctx = _inproj(ctx, ctx_vec(a1), ctx_vec(sh1), w_in_b, _identity_tables(l_ctx), tm=l_ctx,
                    name="inproj_ctx")

    lg = jax.nn.log_sigmoid(ret_decay_logit[0].astype(F32))
    lgv = jnp.broadcast_to(lg.T[:, :, None], (RET_HEADS, 2, HEAD_W))
    gn_cols = ret_gn_g[0].reshape(2, RET_HEADS, HEAD_W).transpose(1, 0, 2)[..., None]
    y_ret = _retention(p_lat, p_ctx, lgv, gn_cols)
    y_diff = _diff_attention(p_lat, p_ctx, diff_lambda[0], diff_subln_g[0][None, :],
                             tq=min(2048, l), tk=512)

    return _ffn(x, y_ret, y_diff, w_out[0].astype(BF16), vec(g1), vec(a2), vec(sh2),
                w_up[0].astype(BF16), conv_w[0], conv_b[0][None, :], w_down[0].astype(BF16),
                vec(g2), final_g[None, :], tm=min(512, l), tn=256)
```

```python
import functools
import math

import jax
import jax.numpy as jnp
import numpy as np
from jax import lax
from jax.experimental import pallas as pl
from jax.experimental.pallas import tpu as pltpu

F32 = jnp.float32
BF16 = jnp.bfloat16

GRID_W = 64
RET_HEADS = 4
RET_DK = 128
DIFF_HEADS = 4
DIFF_HD = 64
HEAD_W = 128
GROUP_W = 512
N_GROUPS = 7
RET_CHUNK = 256
ROPE_BASE = 10000.0
EPS = 1e-6
GN_EPS = 1e-5
N_MOD = 6
LAMBDA_INIT = 0.8 - 0.6 * math.exp(-0.3 * 0)
LOG2E = math.log2(math.e)
RET_LOOKAHEAD = 2
ATTN_LOOKAHEAD = 1

VMEM_LIMIT = 56 * 1024 * 1024


def _cparams(*sem):
    return pltpu.CompilerParams(dimension_semantics=sem, vmem_limit_bytes=VMEM_LIMIT)


def _silu(v):
    return v * (1.0 / (1.0 + jnp.exp(-v)))


def _split_dot(a, w):
    a_hi = a.astype(BF16)
    a_lo = (a - a_hi.astype(F32)).astype(BF16)
    w_hi = w.astype(BF16)
    w_lo = (w - w_hi.astype(F32)).astype(BF16)
    dot = functools.partial(jnp.dot, preferred_element_type=F32)
    return dot(a_hi, w_hi) + (dot(a_hi, w_lo) + dot(a_lo, w_hi))


def _mod_kernel(c_ref, w_ref, b_ref, o_ref):
    o_ref[...] = _split_dot(_silu(c_ref[...]), w_ref[...]) + b_ref[...]


def _modulation(cc, w_mod, b_mod):
    rows, d = cc.shape
    n = w_mod.shape[1]
    tn = 1024
    return pl.pallas_call(
        _mod_kernel,
        grid=(n // tn,),
        in_specs=[pl.BlockSpec((rows, d), lambda j: (0, 0)),
                  pl.BlockSpec((d, tn), lambda j: (0, j)),
                  pl.BlockSpec((1, tn), lambda j: (0, j))],
        out_specs=pl.BlockSpec((rows, tn), lambda j: (0, j)),
        out_shape=jax.ShapeDtypeStruct((rows, n), F32),
        compiler_params=_cparams("arbitrary"),
        name="mod",
    )(cc, w_mod, b_mod)


def _rope128(v, cos, sin):
    return v * cos + pltpu.roll(v, 64, axis=1) * sin


def _rope64(v, cos, sin_lo, sin_hi):
    return v * cos + pltpu.roll(v, 96, axis=1) * sin_lo + pltpu.roll(v, 32, axis=1) * sin_hi


def _inproj_kernel(x_ref, a_ref, s_ref, w_ref, t_ref, o_ref):
    x = x_ref[...]
    xn = x * lax.rsqrt(jnp.mean(x * x, axis=-1, keepdims=True) + EPS)
    h = (xn * a_ref[...] + s_ref[...]).astype(BF16)
    cos_r = t_ref[:, 0 * HEAD_W:1 * HEAD_W]
    sin_r = t_ref[:, 1 * HEAD_W:2 * HEAD_W]
    cos_d = t_ref[:, 2 * HEAD_W:3 * HEAD_W]
    sin_dl = t_ref[:, 3 * HEAD_W:4 * HEAD_W]
    sin_dh = t_ref[:, 4 * HEAD_W:5 * HEAD_W]
    for g in range(N_GROUPS):
        acc = jnp.dot(h, w_ref[:, g * GROUP_W:(g + 1) * GROUP_W], preferred_element_type=F32)
        for hd in range(GROUP_W // HEAD_W):
            v = acc[:, hd * HEAD_W:(hd + 1) * HEAD_W]
            if g == 0:
                v = _rope128(v, cos_r, sin_r)
            elif g == 1:
                v = _rope128(v, cos_r, sin_r) * (RET_DK ** -0.5)
            elif g == 4:
                v = _rope64(v, cos_d, sin_dl, sin_dh) * (DIFF_HD ** -0.5 * LOG2E)
            elif g == 5:
                v = _rope64(v, cos_d, sin_dl, sin_dh)
            c0 = g * GROUP_W + hd * HEAD_W
            o_ref[:, c0:c0 + HEAD_W] = v.astype(BF16)


def _inproj(x, a, s, w_in, tables, *, tm, name):
    b, lx, d = x.shape
    n_cols = w_in.shape[1]
    return pl.pallas_call(
        _inproj_kernel,
        grid=(lx // tm, b),
        in_specs=[pl.BlockSpec((None, tm, d), lambda t, i: (i, t, 0)),
                  pl.BlockSpec((None, 1, d), lambda t, i: (i, 0, 0)),
                  pl.BlockSpec((None, 1, d), lambda t, i: (i, 0, 0)),
                  pl.BlockSpec((d, n_cols), lambda t, i: (0, 0), pipeline_mode=pl.Buffered(1)),
                  pl.BlockSpec((tm, 5 * HEAD_W), lambda t, i: (t, 0))],
        out_specs=pl.BlockSpec((None, tm, n_cols), lambda t, i: (i, t, 0)),
        out_shape=jax.ShapeDtypeStruct((b, lx, n_cols), BF16),
        compiler_params=_cparams("arbitrary", "arbitrary"),
        name=name,
    )(x, a, s, w_in, tables)


def _ret_kernel(q_ref, k_ref, v_ref, g_ref, kc_ref, vc_ref, lg_ref, gn_ref, o_ref, u_ref, s_ref,
                *, n_lat, n_all):
    c = RET_CHUNK
    dk = HEAD_W
    row = lax.broadcasted_iota(jnp.int32, (c, c), 0).astype(F32)
    col = lax.broadcasted_iota(jnp.int32, (c, c), 1).astype(F32)
    tok = lax.broadcasted_iota(jnp.int32, (c, dk), 0).astype(F32)
    lg_f = lg_ref[0:1, :]
    lg_b = lg_ref[1:2, :]
    lg_f2 = jnp.concatenate([lg_f, lg_f], axis=1)
    lg_b2 = jnp.concatenate([lg_b, lg_b], axis=1)
    mask_f = jnp.where(col >= row, jnp.exp(jnp.maximum(col - row, 0.0) * lg_f2), 0.0)
    mask_b = jnp.where(row >= col, jnp.exp(jnp.maximum(row - col, 0.0) * lg_b2), 0.0)
    qdec_f = jnp.exp((tok + 1.0) * lg_f)
    qdec_b = jnp.exp((c - tok) * lg_b)
    kdec_f = jnp.exp((c - 1.0 - tok) * lg_f)
    kdec_b = jnp.exp(tok * lg_b)
    cdec_f = jnp.exp(c * lg_f)
    cdec_b = jnp.exp(c * lg_b)
    tn_dims = (((0,), (0,)), ((), ()))
    nt_dims = (((1,), (1,)), ((), ()))

    def rows(idx):
        return slice(idx * c, (idx + 1) * c)

    def kv_chunk(idx):
        if idx < n_lat:
            return k_ref[rows(idx), :], v_ref[rows(idx), :]
        return kc_ref[rows(idx - n_lat), :], vc_ref[rows(idx - n_lat), :]

    for idx in range(n_all):
        k, v = kv_chunk(idx)
        kf = k.astype(F32)
        kd = jnp.concatenate([(kf * kdec_f).astype(BF16), (kf * kdec_b).astype(BF16)], axis=1)
        u_ref[idx] = lax.dot_general(v, kd, tn_dims, preferred_element_type=F32)

    state = jnp.zeros((dk, dk), F32)
    for idx in [*range(n_lat, n_all), *range(n_lat)]:
        s_ref[idx, :, 0:dk] = state.astype(BF16)
        state = state * cdec_f + u_ref[idx, :, 0:dk]
    state = jnp.zeros((dk, dk), F32)
    for idx in reversed(range(n_all)):
        s_ref[idx, :, dk:2 * dk] = state.astype(BF16)
        state = state * cdec_b + u_ref[idx, :, dk:2 * dk]

    gn_f = jnp.broadcast_to(gn_ref[0], (dk, c))
    gn_b = jnp.broadcast_to(gn_ref[1], (dk, c))

    def group_norm_t(o, g):
        mu = jnp.mean(o, axis=0, keepdims=True)
        d = o - mu
        var = jnp.mean(d * d, axis=0, keepdims=True)
        return d * lax.rsqrt(var + GN_EPS) * g

    def scores_t(idx):
        return lax.dot_general(k_ref[rows(idx), :], q_ref[rows(idx), :], nt_dims,
                               preferred_element_type=F32)

    ahead = [scores_t(i) for i in range(min(RET_LOOKAHEAD, n_lat))]
    for idx in range(n_lat):
        a_t = ahead.pop(0)
        if idx + RET_LOOKAHEAD < n_lat:
            ahead.append(scores_t(idx + RET_LOOKAHEAD))
        v = v_ref[rows(idx), :]
        qf = q_ref[rows(idx), :].astype(F32)
        o_f = (lax.dot_general(v, (a_t * mask_f).astype(BF16), tn_dims, preferred_element_type=F32)
               + lax.dot_general(s_ref[idx, :, 0:dk], (qf * qdec_f).astype(BF16), nt_dims,
                                 preferred_element_type=F32))
        o_b = (lax.dot_general(v, (a_t * mask_b).astype(BF16), tn_dims, preferred_element_type=F32)
               + lax.dot_general(s_ref[idx, :, dk:2 * dk], (qf * qdec_b).astype(BF16), nt_dims,
                                 preferred_element_type=F32))
        n_t = group_norm_t(o_f, gn_f) + group_norm_t(o_b, gn_b)
        y = _silu(g_ref[rows(idx), :].astype(F32)) * n_t.T
        o_ref[rows(idx), :] = y.astype(BF16)


def _retention(p_lat, p_ctx, lgv, gn_cols):
    b, l_lat, _ = p_lat.shape
    l_ctx = p_ctx.shape[1]
    n_lat = l_lat // RET_CHUNK
    n_all = n_lat + l_ctx // RET_CHUNK
    col = lambda rows, g: pl.BlockSpec((None, rows, HEAD_W), lambda i, h: (i, 0, g * RET_HEADS + h))
    kern = functools.partial(_ret_kernel, n_lat=n_lat, n_all=n_all)
    return pl.pallas_call(
        kern,
        grid=(b, RET_HEADS),
        in_specs=[col(l_lat, 0), col(l_lat, 1), col(l_lat, 2), col(l_lat, 3),
                  col(l_ctx, 1), col(l_ctx, 2),
                  pl.BlockSpec((None, 2, HEAD_W), lambda i, h: (h, 0, 0)),
                  pl.BlockSpec((None, 2, HEAD_W, 1), lambda i, h: (h, 0, 0, 0))],
        out_specs=pl.BlockSpec((None, l_lat, HEAD_W), lambda i, h: (i, 0, h)),
        out_shape=jax.ShapeDtypeStruct((b, l_lat, RET_HEADS * HEAD_W), BF16),
        scratch_shapes=[pltpu.VMEM((n_all, HEAD_W, 2 * HEAD_W), F32),
                        pltpu.VMEM((n_all, HEAD_W, 2 * HEAD_W), BF16)],
        compiler_params=_cparams("arbitrary", "arbitrary"),
        name="ret",
    )(p_lat, p_lat, p_lat, p_lat, p_ctx, p_ctx, lgv, gn_cols)


PLAIN_EXP_MAX_LOG2 = 80.0
PLAIN_EXP_MAX_VALUE = 2.0 ** 30
ONES_ROWS = 16


def _attn_kernel(q_ref, k_ref, v_ref, kc_ref, vc_ref, lam_ref, g_ref, o_ref, vt_ref, kv_stat_ref,
                 *, tq, tk, l_lat, l_ctx):
    dv = HEAD_W
    l_all = l_lat + l_ctx
    spans = ([(False, o, min(tk, l_lat - o)) for o in range(0, l_lat, tk)]
             + [(True, o, min(tk, l_ctx - o)) for o in range(0, l_ctx, tk)])
    n_k = len(spans)

    chunks = ([(k_ref, v_ref, o, o) for o in range(0, l_lat, RET_CHUNK)]
              + [(kc_ref, vc_ref, o, l_lat + o) for o in range(0, l_ctx, RET_CHUNK)])

    @pl.when(pl.program_id(2) == 0)
    def _():
        kmax = vmax = jnp.zeros((1, HEAD_W), F32)
        for kr, vr, o, dst in chunks:
            vf = vr[o:o + RET_CHUNK, :].astype(F32)
            vt_ref[0:dv, dst:dst + RET_CHUNK] = vf.T.astype(BF16)
            vmax = jnp.maximum(vmax, jnp.max(jnp.abs(vf), axis=0, keepdims=True))
            kf = kr[o:o + RET_CHUNK, :].astype(F32)
            kmax = jnp.maximum(kmax, jnp.max(jnp.abs(kf), axis=0, keepdims=True))
        vt_ref[dv:dv + ONES_ROWS, :] = jnp.ones((ONES_ROWS, l_all), BF16)
        kv_stat_ref[0:1, :] = kmax
        kv_stat_ref[1:2, :] = vmax

    lane = lax.broadcasted_iota(jnp.int32, (tq, HEAD_W), 1)
    q = q_ref[...]
    zero = jnp.zeros_like(q)
    q1 = jnp.where(lane < DIFF_HD, q, zero)
    q2 = jnp.where(lane >= DIFF_HD, q, zero)
    nt_dims = (((1,), (1,)), ((), ()))

    r = lax.broadcasted_iota(jnp.int32, (HEAD_W, HEAD_W), 0)
    c = lax.broadcasted_iota(jnp.int32, (HEAD_W, HEAD_W), 1)
    same_map = jnp.where((r < DIFF_HD) == (c < DIFF_HD), 1.0, 0.0).astype(BF16)
    q_abs = (jnp.abs(q.astype(F32)) * kv_stat_ref[0:1, :]).astype(BF16)
    bound = jnp.dot(q_abs, same_map, preferred_element_type=F32)
    plain = jnp.logical_and(jnp.max(bound) <= PLAIN_EXP_MAX_LOG2,
                            jnp.max(kv_stat_ref[1:2, :]) <= PLAIN_EXP_MAX_VALUE)

    def scores(i):
        is_ctx, o, size = spans[i]
        kt = (kc_ref if is_ctx else k_ref)[o:o + size, :]
        return (lax.dot_general(kt, q1, nt_dims, preferred_element_type=F32),
                lax.dot_general(kt, q2, nt_dims, preferred_element_type=F32))

    def update_online(s, vt, state):
        m, acc = state
        m_new = jnp.maximum(m, jnp.max(s, axis=0, keepdims=True))
        alpha = jnp.exp2(m - m_new)
        pr = jnp.exp2(s - m_new).astype(BF16)
        return m_new, alpha * acc + jnp.dot(vt, pr, preferred_element_type=F32)

    def update_plain(s, vt, state):
        l, acc = state
        pr = jnp.exp2(s)
        return (l + jnp.sum(pr, axis=0, keepdims=True),
                acc + jnp.dot(vt[0:dv], pr.astype(BF16), preferred_element_type=F32))

    def attend(update, init):
        st1 = st2 = init
        ahead = [scores(i) for i in range(min(ATTN_LOOKAHEAD, n_k))]
        for i in range(n_k):
            s1, s2 = ahead.pop(0)
            if i + ATTN_LOOKAHEAD < n_k:
                ahead.append(scores(i + ATTN_LOOKAHEAD))
            is_ctx, o, size = spans[i]
            start = (l_lat if is_ctx else 0) + o
            vt = vt_ref[:, start:start + size]
            st1 = update(s1, vt, st1)
            st2 = update(s2, vt, st2)
        return st1, st2

    def finish(o1, l1, o2, l2):
        lp = lam_ref[...]
        lam = (jnp.exp(jnp.sum(lp[0:1, :] * lp[1:2, :], axis=-1, keepdims=True))
               - jnp.exp(jnp.sum(lp[2:3, :] * lp[3:4, :], axis=-1, keepdims=True)) + LAMBDA_INIT)
        o = o1 / l1 - lam * (o2 / l2)
        n = o * lax.rsqrt(jnp.mean(o * o, axis=0, keepdims=True) + EPS)
        o_ref[...] = (n.T * g_ref[...] * (1.0 - LAMBDA_INIT)).astype(BF16)

    @pl.when(plain)
    def _():
        (l1, acc1), (l2, acc2) = attend(
            update_plain, (jnp.zeros((1, tq), F32), jnp.zeros((dv, tq), F32)))
        finish(acc1, l1, acc2, l2)

    @pl.when(jnp.logical_not(plain))
    def _():
        (_, acc1), (_, acc2) = attend(
            update_online, (jnp.full((1, tq), -jnp.inf, F32), jnp.zeros((dv + ONES_ROWS, tq), F32)))
        finish(acc1[0:dv], acc1[dv:dv + 1], acc2[0:dv], acc2[dv:dv + 1])


def _diff_attention(p_lat, p_ctx, lam_par, subln_g, *, tq, tk):
    b, l_lat, _ = p_lat.shape
    l_ctx = p_ctx.shape[1]
    kern = functools.partial(_attn_kernel, tq=tq, tk=tk, l_lat=l_lat, l_ctx=l_ctx)
    col = lambda rows, g: pl.BlockSpec((None, rows, HEAD_W), lambda i, h, j: (i, 0, g * DIFF_HEADS + h))
    return pl.pallas_call(
        kern,
        grid=(b, DIFF_HEADS, l_lat // tq),
        in_specs=[pl.BlockSpec((None, tq, HEAD_W), lambda i, h, j: (i, j, 4 * DIFF_HEADS + h)),
                  col(l_lat, 5), col(l_lat, 6), col(l_ctx, 5), col(l_ctx, 6),
                  pl.BlockSpec((4, DIFF_HD), lambda i, h, j: (0, 0)),
                  pl.BlockSpec((1, HEAD_W), lambda i, h, j: (0, h))],
        out_specs=pl.BlockSpec((None, tq, HEAD_W), lambda i, h, j: (i, j, h)),
        out_shape=jax.ShapeDtypeStruct((b, l_lat, DIFF_HEADS * HEAD_W), BF16),
        scratch_shapes=[pltpu.VMEM((HEAD_W + ONES_ROWS, l_lat + l_ctx), BF16),
                        pltpu.VMEM((8, HEAD_W), F32)],
        compiler_params=_cparams("arbitrary", "arbitrary", "arbitrary"),
        name="attn",
    )(p_lat, p_lat, p_lat, p_ctx, p_ctx, lam_par, subln_g)


HALO = 16


def _ffn_kernel(xp_ref, x_ref, xn_ref, rp_ref, r_ref, rn_ref, dp_ref, d_ref, dn_ref,
                wo_ref, g1_ref, a2_ref, s2_ref, wu_ref, cw_ref, cb_ref, wd_ref, g2_ref, fg_ref,
                o_ref, act_ref, *, tm, d_ff, tn):
    t = pl.program_id(1)
    nt = pl.num_programs(1)
    rows = tm + 2 * HALO
    half = r_ref.shape[-1]
    xs = jnp.concatenate([xp_ref[...], x_ref[...], xn_ref[...]], axis=0)
    yr = jnp.concatenate([rp_ref[...], r_ref[...], rn_ref[...]], axis=0)
    yd = jnp.concatenate([dp_ref[...], d_ref[...], dn_ref[...]], axis=0)
    y = (jnp.dot(yr, wo_ref[:half, :], preferred_element_type=F32)
         + jnp.dot(yd, wo_ref[half:, :], preferred_element_type=F32))
    x1s = xs + g1_ref[...] * y
    h2 = (x1s * lax.rsqrt(jnp.mean(x1s * x1s, axis=-1, keepdims=True) + EPS)) * a2_ref[...] + s2_ref[...]
    row = lax.broadcasted_iota(jnp.int32, (rows, 1), 0)
    inside = ((row >= HALO) | (t > 0)) & ((row < tm + HALO) | (t < nt - 1))
    hh = jnp.where(inside, h2, 0.0).astype(BF16)
    for j in range(d_ff // tn):
        for part in range(2):
            c0 = part * d_ff + j * tn
            u = jnp.dot(hh, wu_ref[:, c0:c0 + tn], preferred_element_type=F32)
            w = cw_ref[:, c0:c0 + tn]
            conv = (pltpu.roll(u, 1, axis=0)[HALO:HALO + tm] * w[0:1, :]
                    + u[HALO:HALO + tm] * w[1:2, :]
                    + pltpu.roll(u, rows - 1, axis=0)[HALO:HALO + tm] * w[2:3, :]
                    + cb_ref[:, c0:c0 + tn])
            if part == 0:
                gate = _silu(conv)
            else:
                act_ref[:, j * tn:(j + 1) * tn] = (gate * conv).astype(BF16)
    f = jnp.dot(act_ref[...], wd_ref[...], preferred_element_type=F32)
    x2 = x1s[HALO:HALO + tm, :] + g2_ref[...] * f
    o_ref[...] = x2 * lax.rsqrt(jnp.mean(x2 * x2, axis=-1, keepdims=True) + EPS) * fg_ref[...]


def _ffn(x, y_ret, y_diff, w_out, g1, a2, s2, w_up, conv_w, conv_b, w_down, g2, final_g, *, tm, tn):
    b, l, d = x.shape
    half = y_ret.shape[-1]
    d_ff = w_down.shape[0]
    nh = tm // HALO
    n_halo_blocks = l // HALO
    kern = functools.partial(_ffn_kernel, tm=tm, d_ff=d_ff, tn=tn)

    def halo_specs(w):
        return [pl.BlockSpec((None, HALO, w), lambda i, t: (i, jnp.maximum(t * nh - 1, 0), 0)),
                pl.BlockSpec((None, tm, w), lambda i, t: (i, t, 0)),
                pl.BlockSpec((None, HALO, w), lambda i, t: (i, jnp.minimum((t + 1) * nh, n_halo_blocks - 1), 0))]

    full = lambda a: pl.BlockSpec(a.shape, lambda i, t: (0,) * a.ndim, pipeline_mode=pl.Buffered(1))
    vec = pl.BlockSpec((None, 1, d), lambda i, t: (i, 0, 0))
    return pl.pallas_call(
        kern,
        grid=(b, l // tm),
        in_specs=[*halo_specs(d), *halo_specs(half), *halo_specs(half),
                  full(w_out), vec, vec, vec,
                  full(w_up), full(conv_w), full(conv_b), full(w_down), vec, full(final_g)],
        out_specs=pl.BlockSpec((None, tm, d), lambda i, t: (i, t, 0)),
        out_shape=jax.ShapeDtypeStruct((b, l, d), F32),
        scratch_shapes=[pltpu.VMEM((tm, d_ff), BF16)],
        compiler_params=_cparams("arbitrary", "arbitrary"),
        name="ffn",
    )(x, x, x, y_ret, y_ret, y_ret, y_diff, y_diff, y_diff, w_out, g1, a2, s2,
      w_up, conv_w, conv_b, w_down, g2, final_g)


def _rope_tables(l):
    pos = np.arange(l)
    row = (pos // GRID_W).astype(np.float64)[:, None]
    col = (pos % GRID_W).astype(np.float64)[:, None]

    def angles(head_dim):
        n_freq = head_dim // 4
        inv = ROPE_BASE ** (-np.arange(n_freq, dtype=np.float64) / n_freq)
        return np.concatenate([row * inv, col * inv], axis=-1)

    ar = angles(RET_DK)
    ad = angles(DIFF_HD)
    cr, sr = np.cos(ar), np.sin(ar)
    cd, sd = np.cos(ad), np.sin(ad)
    zd = np.zeros_like(sd)
    return jnp.asarray(np.concatenate([
        cr, cr, -sr, sr,
        cd, cd, cd, cd,
        -sd, zd, -sd, zd,
        zd, sd, zd, sd], axis=-1), F32)


def _identity_tables(l):
    one = np.ones((l, HEAD_W), np.float32)
    zero = np.zeros((l, HEAD_W), np.float32)
    return jnp.asarray(np.concatenate([one, zero, one, zero, zero], axis=-1))


def kernel(x, c, ctx, c_ctx, w_mod, b_mod, norm1_g, w_in, ret_decay_logit, ret_gn_g, diff_lambda,
           diff_subln_g, w_out, norm2_g, w_up, conv_w, conv_b, w_down, final_g):
    b, l, d = x.shape
    l_ctx = ctx.shape[1]

    rows = -(-(b + 1) // 8) * 8
    cc = jnp.zeros((rows, d), F32).at[:b].set(c).at[b].set(c_ctx)
    m = _modulation(cc, w_mod[0], b_mod[0][None, :])
    sh1, sc1, g1, sh2, sc2, g2 = [m[:, i * d:(i + 1) * d] for i in range(N_MOD)]
    a1 = norm1_g[0][None, :] * (1.0 + sc1)
    a2 = norm2_g[0][None, :] * (1.0 + sc2)
    vec = lambda v: v[:b, None, :]
    ctx_vec = lambda v: jnp.broadcast_to(v[b][None, None, :], (b, 1, d))

    w_in_b = w_in[0].astype(BF16)
    p_lat = _inproj(x, vec(a1), vec(sh1), w_in_b, _rope_tables(l), tm=min(1024, l), name="inproj_lat")
    p_ctx = _inproj(ctx, ctx_vec(a1), ctx_vec(sh1), w_in_b, _identity_tables(l_ctx), tm=l_ctx,
                    name="inproj_ctx")

    lg = jax.nn.log_sigmoid(ret_decay_logit[0].astype(F32))
    lgv = jnp.broadcast_to(lg.T[:, :, None], (RET_HEADS, 2, HEAD_W))
    gn_cols = ret_gn_g[0].reshape(2, RET_HEADS, HEAD_W).transpose(1, 0, 2)[..., None]
    y_ret = _retention(p_lat, p_ctx, lgv, gn_cols)
    y_diff = _diff_attention(p_lat, p_ctx, diff_lambda[0], diff_subln_g[0][None, :],
                             tq=min(2048, l), tk=512)

    return _ffn(x, y_ret, y_diff, w_out[0].astype(BF16), vec(g1), vec(a2), vec(sh2),
                w_up[0].astype(BF16), conv_w[0], conv_b[0][None, :], w_down[0].astype(BF16),
                vec(g2), final_g[None, :], tm=min(512, l), tn=256)
```

```python
import functools
import math

import jax
import jax.numpy as jnp
import numpy as np
from jax import lax
from jax.experimental import pallas as pl
from jax.experimental.pallas import tpu as pltpu

F32 = jnp.float32
BF16 = jnp.bfloat16

GRID_W = 64
RET_HEADS = 4
RET_DK = 128
DIFF_HEADS = 4
DIFF_HD = 64
HEAD_W = 128
GROUP_W = 512
LAT_GROUPS = (0, 1, 2, 3, 4, 5, 6)
CTX_GROUPS = (1, 2, 5, 6)
RET_CHUNK = 256
ROPE_BASE = 10000.0
EPS = 1e-6
GN_EPS = 1e-5
N_MOD = 6
LAMBDA_INIT = 0.8 - 0.6 * math.exp(-0.3 * 0)
LOG2E = math.log2(math.e)
RET_LOOKAHEAD = 2
ATTN_LOOKAHEAD = 1

VMEM_LIMIT = 56 * 1024 * 1024


def _cparams(*sem):
    return pltpu.CompilerParams(dimension_semantics=sem, vmem_limit_bytes=VMEM_LIMIT)


def _silu(v):
    return v * (1.0 / (1.0 + jnp.exp(-v)))


def _split_dot(a, w):
    a_hi = a.astype(BF16)
    a_lo = (a - a_hi.astype(F32)).astype(BF16)
    w_hi = w.astype(BF16)
    w_lo = (w - w_hi.astype(F32)).astype(BF16)
    dot = functools.partial(jnp.dot, preferred_element_type=F32)
    return dot(a_hi, w_hi) + (dot(a_hi, w_lo) + dot(a_lo, w_hi))


def _mod_kernel(c_ref, w_ref, b_ref, o_ref):
    o_ref[...] = _split_dot(_silu(c_ref[...]), w_ref[...]) + b_ref[...]


def _modulation(cc, w_mod, b_mod):
    rows, d = cc.shape
    n = w_mod.shape[1]
    tn = 1024
    return pl.pallas_call(
        _mod_kernel,
        grid=(n // tn,),
        in_specs=[pl.BlockSpec((rows, d), lambda j: (0, 0)),
                  pl.BlockSpec((d, tn), lambda j: (0, j)),
                  pl.BlockSpec((1, tn), lambda j: (0, j))],
        out_specs=pl.BlockSpec((rows, tn), lambda j: (0, j)),
        out_shape=jax.ShapeDtypeStruct((rows, n), F32),
        compiler_params=_cparams("arbitrary"),
        name="mod",
    )(cc, w_mod, b_mod)


def _rope128(v, cos, sin):
    return v * cos + pltpu.roll(v, 64, axis=1) * sin


def _rope64(v, cos, sin_lo, sin_hi):
    return v * cos + pltpu.roll(v, 96, axis=1) * sin_lo + pltpu.roll(v, 32, axis=1) * sin_hi


def _inproj_kernel(x_ref, a_ref, s_ref, w_ref, t_ref, o_ref, *, groups):
    x = x_ref[...]
    xn = x * lax.rsqrt(jnp.mean(x * x, axis=-1, keepdims=True) + EPS)
    h = (xn * a_ref[...] + s_ref[...]).astype(BF16)
    cos_r = t_ref[:, 0 * HEAD_W:1 * HEAD_W]
    sin_r = t_ref[:, 1 * HEAD_W:2 * HEAD_W]
    cos_d = t_ref[:, 2 * HEAD_W:3 * HEAD_W]
    sin_dl = t_ref[:, 3 * HEAD_W:4 * HEAD_W]
    sin_dh = t_ref[:, 4 * HEAD_W:5 * HEAD_W]
    for slot, g in enumerate(groups):
        acc = jnp.dot(h, w_ref[:, g * GROUP_W:(g + 1) * GROUP_W], preferred_element_type=F32)
        for hd in range(GROUP_W // HEAD_W):
            v = acc[:, hd * HEAD_W:(hd + 1) * HEAD_W]
            if g == 0:
                v = _rope128(v, cos_r, sin_r)
            elif g == 1:
                v = _rope128(v, cos_r, sin_r) * (RET_DK ** -0.5)
            elif g == 4:
                v = _rope64(v, cos_d, sin_dl, sin_dh) * (DIFF_HD ** -0.5 * LOG2E)
            elif g == 5:
                v = _rope64(v, cos_d, sin_dl, sin_dh)
            c0 = slot * GROUP_W + hd * HEAD_W
            o_ref[:, c0:c0 + HEAD_W] = v.astype(BF16)


def _inproj(x, a, s, w_in, tables, *, tm, groups, name):
    b, lx, d = x.shape
    n_in = w_in.shape[1]
    n_cols = len(groups) * GROUP_W
    return pl.pallas_call(
        functools.partial(_inproj_kernel, groups=groups),
        grid=(lx // tm, b),
        in_specs=[pl.BlockSpec((None, tm, d), lambda t, i: (i, t, 0)),
                  pl.BlockSpec((None, 1, d), lambda t, i: (i, 0, 0)),
                  pl.BlockSpec((None, 1, d), lambda t, i: (i, 0, 0)),
                  pl.BlockSpec((d, n_in), lambda t, i: (0, 0), pipeline_mode=pl.Buffered(1)),
                  pl.BlockSpec((tm, 5 * HEAD_W), lambda t, i: (t, 0))],
        out_specs=pl.BlockSpec((None, tm, n_cols), lambda t, i: (i, t, 0)),
        out_shape=jax.ShapeDtypeStruct((b, lx, n_cols), BF16),
        compiler_params=_cparams("arbitrary", "arbitrary"),
        name=name,
    )(x, a, s, w_in, tables)


def _ret_kernel(q_ref, k_ref, v_ref, g_ref, kc_ref, vc_ref, lg_ref, gn_ref, o_ref, u_ref, s_ref,
                *, n_lat, n_all):
    c = RET_CHUNK
    dk = HEAD_W
    row = lax.broadcasted_iota(jnp.int32, (c, c), 0).astype(F32)
    col = lax.broadcasted_iota(jnp.int32, (c, c), 1).astype(F32)
    tok = lax.broadcasted_iota(jnp.int32, (c, dk), 0).astype(F32)
    lg_f = lg_ref[0:1, :]
    lg_b = lg_ref[1:2, :]
    lg_f2 = jnp.concatenate([lg_f, lg_f], axis=1)
    lg_b2 = jnp.concatenate([lg_b, lg_b], axis=1)
    mask_f = jnp.where(col >= row, jnp.exp(jnp.maximum(col - row, 0.0) * lg_f2), 0.0).astype(BF16)
    mask_b = jnp.where(row >= col, jnp.exp(jnp.maximum(row - col, 0.0) * lg_b2), 0.0).astype(BF16)
    qdec_f = jnp.exp((tok + 1.0) * lg_f)
    qdec_b = jnp.exp((c - tok) * lg_b)
    kdec_f = jnp.exp((c - 1.0 - tok) * lg_f)
    kdec_b = jnp.exp(tok * lg_b)
    cdec_f = jnp.exp(c * lg_f)
    cdec_b = jnp.exp(c * lg_b)
    tn_dims = (((0,), (0,)), ((), ()))
    nt_dims = (((1,), (1,)), ((), ()))

    def rows(idx):
        return slice(idx * c, (idx + 1) * c)

    def kv_chunk(idx):
        if idx < n_lat:
            return k_ref[rows(idx), :], v_ref[rows(idx), :]
        return kc_ref[rows(idx - n_lat), :], vc_ref[rows(idx - n_lat), :]

    for idx in range(n_all):
        k, v = kv_chunk(idx)
        kf = k.astype(F32)
        kd = jnp.concatenate([(kf * kdec_f).astype(BF16), (kf * kdec_b).astype(BF16)], axis=1)
        u_ref[idx] = lax.dot_general(v, kd, tn_dims, preferred_element_type=F32)

    state = jnp.zeros((dk, dk), F32)
    for idx in [*range(n_lat, n_all), *range(n_lat)]:
        s_ref[idx, :, 0:dk] = state.astype(BF16)
        state = state * cdec_f + u_ref[idx, :, 0:dk]
    state = jnp.zeros((dk, dk), F32)
    for idx in reversed(range(n_all)):
        s_ref[idx, :, dk:2 * dk] = state.astype(BF16)
        state = state * cdec_b + u_ref[idx, :, dk:2 * dk]

    gn_f = jnp.broadcast_to(gn_ref[0], (dk, c))
    gn_b = jnp.broadcast_to(gn_ref[1], (dk, c))

    def group_norm_t(o, g):
        mu = jnp.mean(o, axis=0, keepdims=True)
        d = o - mu
        var = jnp.mean(d * d, axis=0, keepdims=True)
        return d * lax.rsqrt(var + GN_EPS) * g

    def scores_t(idx):
        return lax.dot_general(k_ref[rows(idx), :], q_ref[rows(idx), :], nt_dims,
                               preferred_element_type=F32)

    ahead = [scores_t(i) for i in range(min(RET_LOOKAHEAD, n_lat))]
    for idx in range(n_lat):
        a_t = ahead.pop(0)
        if idx + RET_LOOKAHEAD < n_lat:
            ahead.append(scores_t(idx + RET_LOOKAHEAD))
        v = v_ref[rows(idx), :]
        qf = q_ref[rows(idx), :].astype(F32)
        a_b = a_t.astype(BF16)
        o_f = (lax.dot_general(v, a_b * mask_f, tn_dims, preferred_element_type=F32)
               + lax.dot_general(s_ref[idx, :, 0:dk], (qf * qdec_f).astype(BF16), nt_dims,
                                 preferred_element_type=F32))
        o_b = (lax.dot_general(v, a_b * mask_b, tn_dims, preferred_element_type=F32)
               + lax.dot_general(s_ref[idx, :, dk:2 * dk], (qf * qdec_b).astype(BF16), nt_dims,
                                 preferred_element_type=F32))
        n_t = group_norm_t(o_f, gn_f) + group_norm_t(o_b, gn_b)
        y = _silu(g_ref[rows(idx), :].astype(F32)) * n_t.T
        o_ref[rows(idx), :] = y.astype(BF16)


def _retention(p_lat, p_ctx, lgv, gn_cols):
    b, l_lat, _ = p_lat.shape
    l_ctx = p_ctx.shape[1]
    n_lat = l_lat // RET_CHUNK
    n_all = n_lat + l_ctx // RET_CHUNK
    col = lambda rows, g: pl.BlockSpec((None, rows, HEAD_W), lambda i, h: (i, 0, g * RET_HEADS + h))
    kern = functools.partial(_ret_kernel, n_lat=n_lat, n_all=n_all)
    return pl.pallas_call(
        kern,
        grid=(b, RET_HEADS),
        in_specs=[col(l_lat, 0), col(l_lat, 1), col(l_lat, 2), col(l_lat, 3),
                  col(l_ctx, CTX_GROUPS.index(1)), col(l_ctx, CTX_GROUPS.index(2)),
                  pl.BlockSpec((None, 2, HEAD_W), lambda i, h: (h, 0, 0)),
                  pl.BlockSpec((None, 2, HEAD_W, 1), lambda i, h: (h, 0, 0, 0))],
        out_specs=pl.BlockSpec((None, l_lat, HEAD_W), lambda i, h: (i, 0, h)),
        out_shape=jax.ShapeDtypeStruct((b, l_lat, RET_HEADS * HEAD_W), BF16),
        scratch_shapes=[pltpu.VMEM((n_all, HEAD_W, 2 * HEAD_W), F32),
                        pltpu.VMEM((n_all, HEAD_W, 2 * HEAD_W), BF16)],
        compiler_params=_cparams("arbitrary", "arbitrary"),
        name="ret",
    )(p_lat, p_lat, p_lat, p_lat, p_ctx, p_ctx, lgv, gn_cols)


PLAIN_EXP_MAX_LOG2 = 80.0
PLAIN_EXP_MAX_VALUE = 2.0 ** 30
ONES_ROWS = 16


def _attn_kernel(q_ref, k_ref, v_ref, kc_ref, vc_ref, lam_ref, g_ref, o_ref, vt_ref, kv_stat_ref,
                 *, tq, tk, l_lat, l_ctx):
    dv = HEAD_W
    l_all = l_lat + l_ctx
    spans = ([(False, o, min(tk, l_lat - o)) for o in range(0, l_lat, tk)]
             + [(True, o, min(tk, l_ctx - o)) for o in range(0, l_ctx, tk)])
    n_k = len(spans)

    chunks = ([(k_ref, v_ref, o, o) for o in range(0, l_lat, RET_CHUNK)]
              + [(kc_ref, vc_ref, o, l_lat + o) for o in range(0, l_ctx, RET_CHUNK)])

    @pl.when(pl.program_id(2) == 0)
    def _():
        kmax = vmax = jnp.zeros((1, HEAD_W), F32)
        for kr, vr, o, dst in chunks:
            vf = vr[o:o + RET_CHUNK, :].astype(F32)
            vt_ref[0:dv, dst:dst + RET_CHUNK] = vf.T.astype(BF16)
            vmax = jnp.maximum(vmax, jnp.max(jnp.abs(vf), axis=0, keepdims=True))
            kf = kr[o:o + RET_CHUNK, :].astype(F32)
            kmax = jnp.maximum(kmax, jnp.max(jnp.abs(kf), axis=0, keepdims=True))
        vt_ref[dv:dv + ONES_ROWS, :] = jnp.ones((ONES_ROWS, l_all), BF16)
        kv_stat_ref[0:1, :] = kmax
        kv_stat_ref[1:2, :] = vmax

    lane = lax.broadcasted_iota(jnp.int32, (tq, HEAD_W), 1)
    q = q_ref[...]
    zero = jnp.zeros_like(q)
    q1 = jnp.where(lane < DIFF_HD, q, zero)
    q2 = jnp.where(lane >= DIFF_HD, q, zero)
    nt_dims = (((1,), (1,)), ((), ()))

    r = lax.broadcasted_iota(jnp.int32, (HEAD_W, HEAD_W), 0)
    c = lax.broadcasted_iota(jnp.int32, (HEAD_W, HEAD_W), 1)
    same_map = jnp.where((r < DIFF_HD) == (c < DIFF_HD), 1.0, 0.0).astype(BF16)
    q_abs = (jnp.abs(q.astype(F32)) * kv_stat_ref[0:1, :]).astype(BF16)
    bound = jnp.dot(q_abs, same_map, preferred_element_type=F32)
    plain = jnp.logical_and(jnp.max(bound) <= PLAIN_EXP_MAX_LOG2,
                            jnp.max(kv_stat_ref[1:2, :]) <= PLAIN_EXP_MAX_VALUE)

    def scores(i):
        is_ctx, o, size = spans[i]
        kt = (kc_ref if is_ctx else k_ref)[o:o + size, :]
        return (lax.dot_general(kt, q1, nt_dims, preferred_element_type=F32),
                lax.dot_general(kt, q2, nt_dims, preferred_element_type=F32))

    def update_online(s, vt, state):
        m, acc = state
        m_new = jnp.maximum(m, jnp.max(s, axis=0, keepdims=True))
        alpha = jnp.exp2(m - m_new)
        pr = jnp.exp2(s - m_new).astype(BF16)
        return m_new, alpha * acc + jnp.dot(vt, pr, preferred_element_type=F32)

    def update_plain(s, vt, state):
        l, acc = state
        pr = jnp.exp2(s)
        return (l + jnp.sum(pr, axis=0, keepdims=True),
                acc + jnp.dot(vt[0:dv], pr.astype(BF16), preferred_element_type=F32))

    def attend(update, init):
        st1 = st2 = init
        ahead = [scores(i) for i in range(min(ATTN_LOOKAHEAD, n_k))]
        for i in range(n_k):
            s1, s2 = ahead.pop(0)
            if i + ATTN_LOOKAHEAD < n_k:
                ahead.append(scores(i + ATTN_LOOKAHEAD))
            is_ctx, o, size = spans[i]
            start = (l_lat if is_ctx else 0) + o
            vt = vt_ref[:, start:start + size]
            st1 = update(s1, vt, st1)
            st2 = update(s2, vt, st2)
        return st1, st2

    def finish(o1, l1, o2, l2):
        lp = lam_ref[...]
        lam = (jnp.exp(jnp.sum(lp[0:1, :] * lp[1:2, :], axis=-1, keepdims=True))
               - jnp.exp(jnp.sum(lp[2:3, :] * lp[3:4, :], axis=-1, keepdims=True)) + LAMBDA_INIT)
        o = o1 / l1 - lam * (o2 / l2)
        n = o * lax.rsqrt(jnp.mean(o * o, axis=0, keepdims=True) + EPS)
        o_ref[...] = (n.T * g_ref[...] * (1.0 - LAMBDA_INIT)).astype(BF16)

    @pl.when(plain)
    def _():
        (l1, acc1), (l2, acc2) = attend(
            update_plain, (jnp.zeros((1, tq), F32), jnp.zeros((dv, tq), F32)))
        finish(acc1, l1, acc2, l2)

    @pl.when(jnp.logical_not(plain))
    def _():
        (_, acc1), (_, acc2) = attend(
            update_online, (jnp.full((1, tq), -jnp.inf, F32), jnp.zeros((dv + ONES_ROWS, tq), F32)))
        finish(acc1[0:dv], acc1[dv:dv + 1], acc2[0:dv], acc2[dv:dv + 1])


def _diff_attention(p_lat, p_ctx, lam_par, subln_g, *, tq, tk):
    b, l_lat, _ = p_lat.shape
    l_ctx = p_ctx.shape[1]
    kern = functools.partial(_attn_kernel, tq=tq, tk=tk, l_lat=l_lat, l_ctx=l_ctx)
    col = lambda rows, g: pl.BlockSpec((None, rows, HEAD_W), lambda i, h, j: (i, 0, g * DIFF_HEADS + h))
    return pl.pallas_call(
        kern,
        grid=(b, DIFF_HEADS, l_lat // tq),
        in_specs=[pl.BlockSpec((None, tq, HEAD_W), lambda i, h, j: (i, j, 4 * DIFF_HEADS + h)),
                  col(l_lat, 5), col(l_lat, 6),
                  col(l_ctx, CTX_GROUPS.index(5)), col(l_ctx, CTX_GROUPS.index(6)),
                  pl.BlockSpec((4, DIFF_HD), lambda i, h, j: (0, 0)),
                  pl.BlockSpec((1, HEAD_W), lambda i, h, j: (0, h))],
        out_specs=pl.BlockSpec((None, tq, HEAD_W), lambda i, h, j: (i, j, h)),
        out_shape=jax.ShapeDtypeStruct((b, l_lat, DIFF_HEADS * HEAD_W), BF16),
        scratch_shapes=[pltpu.VMEM((HEAD_W + ONES_ROWS, l_lat + l_ctx), BF16),
                        pltpu.VMEM((8, HEAD_W), F32)],
        compiler_params=_cparams("arbitrary", "arbitrary", "arbitrary"),
        name="attn",
    )(p_lat, p_lat, p_lat, p_ctx, p_ctx, lam_par, subln_g)


HALO = 16


def _ffn_kernel(xp_ref, x_ref, xn_ref, rp_ref, r_ref, rn_ref, dp_ref, d_ref, dn_ref,
                wo_ref, g1_ref, a2_ref, s2_ref, wu_ref, cw_ref, cb_ref, wd_ref, g2_ref, fg_ref,
                o_ref, act_ref, *, tm, d_ff, tn):
    t = pl.program_id(1)
    nt = pl.num_programs(1)
    rows = tm + 2 * HALO
    half = r_ref.shape[-1]
    xs = jnp.concatenate([xp_ref[...], x_ref[...], xn_ref[...]], axis=0)
    yr = jnp.concatenate([rp_ref[...], r_ref[...], rn_ref[...]], axis=0)
    yd = jnp.concatenate([dp_ref[...], d_ref[...], dn_ref[...]], axis=0)
    y = (jnp.dot(yr, wo_ref[:half, :], preferred_element_type=F32)
         + jnp.dot(yd, wo_ref[half:, :], preferred_element_type=F32))
    x1s = xs + g1_ref[...] * y
    h2 = (x1s * lax.rsqrt(jnp.mean(x1s * x1s, axis=-1, keepdims=True) + EPS)) * a2_ref[...] + s2_ref[...]
    row = lax.broadcasted_iota(jnp.int32, (rows, 1), 0)
    inside = ((row >= HALO) | (t > 0)) & ((row < tm + HALO) | (t < nt - 1))
    hh = jnp.where(inside, h2, 0.0).astype(BF16)
    for j in range(d_ff // tn):
        for part in range(2):
            c0 = part * d_ff + j * tn
            u = jnp.dot(hh, wu_ref[:, c0:c0 + tn], preferred_element_type=F32)
            w = cw_ref[:, c0:c0 + tn]
            conv = (pltpu.roll(u, 1, axis=0)[HALO:HALO + tm] * w[0:1, :]
                    + u[HALO:HALO + tm] * w[1:2, :]
                    + pltpu.roll(u, rows - 1, axis=0)[HALO:HALO + tm] * w[2:3, :]
                    + cb_ref[:, c0:c0 + tn])
            if part == 0:
                gate = _silu(conv)
            else:
                act_ref[:, j * tn:(j + 1) * tn] = (gate * conv).astype(BF16)
    rb = tm // 2
    for r in range(2):
        f = jnp.dot(act_ref[r * rb:(r + 1) * rb, :], wd_ref[...], preferred_element_type=F32)
        x2 = x1s[HALO + r * rb:HALO + (r + 1) * rb, :] + g2_ref[...] * f
        o_ref[r * rb:(r + 1) * rb, :] = (
            x2 * lax.rsqrt(jnp.mean(x2 * x2, axis=-1, keepdims=True) + EPS) * fg_ref[...])


def _ffn(x, y_ret, y_diff, w_out, g1, a2, s2, w_up, conv_w, conv_b, w_down, g2, final_g, *, tm, tn):
    b, l, d = x.shape
    half = y_ret.shape[-1]
    d_ff = w_down.shape[0]
    nh = tm // HALO
    n_halo_blocks = l // HALO
    kern = functools.partial(_ffn_kernel, tm=tm, d_ff=d_ff, tn=tn)

    def halo_specs(w):
        return [pl.BlockSpec((None, HALO, w), lambda i, t: (i, jnp.maximum(t * nh - 1, 0), 0)),
                pl.BlockSpec((None, tm, w), lambda i, t: (i, t, 0)),
                pl.BlockSpec((None, HALO, w), lambda i, t: (i, jnp.minimum((t + 1) * nh, n_halo_blocks - 1), 0))]

    full = lambda a: pl.BlockSpec(a.shape, lambda i, t: (0,) * a.ndim, pipeline_mode=pl.Buffered(1))
    vec = pl.BlockSpec((None, 1, d), lambda i, t: (i, 0, 0))
    return pl.pallas_call(
        kern,
        grid=(b, l // tm),
        in_specs=[*halo_specs(d), *halo_specs(half), *halo_specs(half),
                  full(w_out), vec, vec, vec,
                  full(w_up), full(conv_w), full(conv_b), full(w_down), vec, full(final_g)],
        out_specs=pl.BlockSpec((None, tm, d), lambda i, t: (i, t, 0)),
        out_shape=jax.ShapeDtypeStruct((b, l, d), F32),
        scratch_shapes=[pltpu.VMEM((tm, d_ff), BF16)],
        compiler_params=_cparams("arbitrary", "arbitrary"),
        name="ffn",
    )(x, x, x, y_ret, y_ret, y_ret, y_diff, y_diff, y_diff, w_out, g1, a2, s2,
      w_up, conv_w, conv_b, w_down, g2, final_g)


def _rope_tables(l):
    pos = np.arange(l)
    row = (pos // GRID_W).astype(np.float64)[:, None]
    col = (pos % GRID_W).astype(np.float64)[:, None]

    def angles(head_dim):
        n_freq = head_dim // 4
        inv = ROPE_BASE ** (-np.arange(n_freq, dtype=np.float64) / n_freq)
        return np.concatenate([row * inv, col * inv], axis=-1)

    ar = angles(RET_DK)
    ad = angles(DIFF_HD)
    cr, sr = np.cos(ar), np.sin(ar)
    cd, sd = np.cos(ad), np.sin(ad)
    zd = np.zeros_like(sd)
    return jnp.asarray(np.concatenate([
        cr, cr, -sr, sr,
        cd, cd, cd, cd,
        -sd, zd, -sd, zd,
        zd, sd, zd, sd], axis=-1), F32)


def _identity_tables(l):
    one = np.ones((l, HEAD_W), np.float32)
    zero = np.zeros((l, HEAD_W), np.float32)
    return jnp.asarray(np.concatenate([one, zero, one, zero, zero], axis=-1))


def kernel(x, c, ctx, c_ctx, w_mod, b_mod, norm1_g, w_in, ret_decay_logit, ret_gn_g, diff_lambda,
           diff_subln_g, w_out, norm2_g, w_up, conv_w, conv_b, w_down, final_g):
    b, l, d = x.shape
    l_ctx = ctx.shape[1]

    rows = -(-(b + 1) // 8) * 8
    cc = jnp.zeros((rows, d), F32).at[:b].set(c).at[b].set(c_ctx)
    m = _modulation(cc, w_mod[0], b_mod[0][None, :])
    sh1, sc1, g1, sh2, sc2, g2 = [m[:, i * d:(i + 1) * d] for i in range(N_MOD)]
    a1 = norm1_g[0][None, :] * (1.0 + sc1)
    a2 = norm2_g[0][None, :] * (1.0 + sc2)
    vec = lambda v: v[:b, None, :]
    ctx_vec = lambda v: jnp.broadcast_to(v[b][None, None, :], (b, 1, d))

    w_in_b = w_in[0].astype(BF16)
    p_lat = _inproj(x, vec(a1), vec(sh1), w_in_b, _rope_tables(l), tm=min(1024, l),
                    groups=LAT_GROUPS, name="inproj_lat")
    p_ctx = _inproj(ctx, ctx_vec(a1), ctx_vec(sh1), w_in_b, _identity_tables(l_ctx), tm=l_ctx,
                    groups=CTX_GROUPS, name="inproj_ctx")

    lg = jax.nn.log_sigmoid(ret_decay_logit[0].astype(F32))
    lgv = jnp.broadcast_to(lg.T[:, :, None], (RET_HEADS, 2, HEAD_W))
    gn_cols = ret_gn_g[0].reshape(2, RET_HEADS, HEAD_W).transpose(1, 0, 2)[..., None]
    y_ret = _retention(p_lat, p_ctx, lgv, gn_cols)
    y_diff = _diff_attention(p_lat, p_ctx, diff_lambda[0], diff_subln_g[0][None, :],
                             tq=min(2048, l), tk=512)

    return _ffn(x, y_ret, y_diff, w_out[0].astype(BF16), vec(g1), vec(a2), vec(sh2),
                w_up[0].astype(BF16), conv_w[0], conv_b[0][None, :], w_down[0].astype(BF16),
                vec(g2), final_g[None, :], tm=min(512, l), tn=256)
```

```python
import functools
import math

import jax
import jax.numpy as jnp
import numpy as np
from jax import lax
from jax.experimental import pallas as pl
from jax.experimental.pallas import tpu as pltpu

F32 = jnp.float32
BF16 = jnp.bfloat16

GRID_W = 64
RET_HEADS = 4
RET_DK = 128
DIFF_HEADS = 4
DIFF_HD = 64
HEAD_W = 128
GROUP_W = 512
LAT_GROUPS = (0, 1, 2, 3, 4, 5, 6)
CTX_GROUPS = LAT_GROUPS
RET_CHUNK = 256
ROPE_BASE = 10000.0
EPS = 1e-6
GN_EPS = 1e-5
N_MOD = 6
LAMBDA_INIT = 0.8 - 0.6 * math.exp(-0.3 * 0)
LOG2E = math.log2(math.e)
RET_LOOKAHEAD = 2
ATTN_LOOKAHEAD = 1

VMEM_LIMIT = 56 * 1024 * 1024


def _cparams(*sem):
    return pltpu.CompilerParams(dimension_semantics=sem, vmem_limit_bytes=VMEM_LIMIT)


def _silu(v):
    return v * (1.0 / (1.0 + jnp.exp(-v)))


def _split_dot(a, w):
    a_hi = a.astype(BF16)
    a_lo = (a - a_hi.astype(F32)).astype(BF16)
    w_hi = w.astype(BF16)
    w_lo = (w - w_hi.astype(F32)).astype(BF16)
    dot = functools.partial(jnp.dot, preferred_element_type=F32)
    return dot(a_hi, w_hi) + (dot(a_hi, w_lo) + dot(a_lo, w_hi))


def _mod_kernel(c_ref, w_ref, b_ref, o_ref):
    o_ref[...] = _split_dot(_silu(c_ref[...]), w_ref[...]) + b_ref[...]


def _modulation(cc, w_mod, b_mod):
    rows, d = cc.shape
    n = w_mod.shape[1]
    tn = 1024
    return pl.pallas_call(
        _mod_kernel,
        grid=(n // tn,),
        in_specs=[pl.BlockSpec((rows, d), lambda j: (0, 0)),
                  pl.BlockSpec((d, tn), lambda j: (0, j)),
                  pl.BlockSpec((1, tn), lambda j: (0, j))],
        out_specs=pl.BlockSpec((rows, tn), lambda j: (0, j)),
        out_shape=jax.ShapeDtypeStruct((rows, n), F32),
        compiler_params=_cparams("arbitrary"),
        name="mod",
    )(cc, w_mod, b_mod)


def _rope128(v, cos, sin):
    return v * cos + pltpu.roll(v, 64, axis=1) * sin


def _rope64(v, cos, sin_lo, sin_hi):
    return v * cos + pltpu.roll(v, 96, axis=1) * sin_lo + pltpu.roll(v, 32, axis=1) * sin_hi


def _inproj_kernel(x_ref, a_ref, s_ref, w_ref, t_ref, o_ref, *, groups):
    x = x_ref[...]
    xn = x * lax.rsqrt(jnp.mean(x * x, axis=-1, keepdims=True) + EPS)
    h = (xn * a_ref[...] + s_ref[...]).astype(BF16)
    cos_r = t_ref[:, 0 * HEAD_W:1 * HEAD_W]
    sin_r = t_ref[:, 1 * HEAD_W:2 * HEAD_W]
    cos_d = t_ref[:, 2 * HEAD_W:3 * HEAD_W]
    sin_dl = t_ref[:, 3 * HEAD_W:4 * HEAD_W]
    sin_dh = t_ref[:, 4 * HEAD_W:5 * HEAD_W]
    for slot, g in enumerate(groups):
        acc = jnp.dot(h, w_ref[:, g * GROUP_W:(g + 1) * GROUP_W], preferred_element_type=F32)
        for hd in range(GROUP_W // HEAD_W):
            v = acc[:, hd * HEAD_W:(hd + 1) * HEAD_W]
            if g == 0:
                v = _rope128(v, cos_r, sin_r)
            elif g == 1:
                v = _rope128(v, cos_r, sin_r) * (RET_DK ** -0.5)
            elif g == 4:
                v = _rope64(v, cos_d, sin_dl, sin_dh) * (DIFF_HD ** -0.5 * LOG2E)
            elif g == 5:
                v = _rope64(v, cos_d, sin_dl, sin_dh)
            c0 = slot * GROUP_W + hd * HEAD_W
            o_ref[:, c0:c0 + HEAD_W] = v.astype(BF16)


def _inproj(x, a, s, w_in, tables, *, tm, groups, name):
    b, lx, d = x.shape
    n_in = w_in.shape[1]
    n_cols = len(groups) * GROUP_W
    return pl.pallas_call(
        functools.partial(_inproj_kernel, groups=groups),
        grid=(lx // tm, b),
        in_specs=[pl.BlockSpec((None, tm, d), lambda t, i: (i, t, 0)),
                  pl.BlockSpec((None, 1, d), lambda t, i: (i, 0, 0)),
                  pl.BlockSpec((None, 1, d), lambda t, i: (i, 0, 0)),
                  pl.BlockSpec((d, n_in), lambda t, i: (0, 0), pipeline_mode=pl.Buffered(1)),
                  pl.BlockSpec((tm, 5 * HEAD_W), lambda t, i: (t, 0))],
        out_specs=pl.BlockSpec((None, tm, n_cols), lambda t, i: (i, t, 0)),
        out_shape=jax.ShapeDtypeStruct((b, lx, n_cols), BF16),
        compiler_params=_cparams("arbitrary", "arbitrary"),
        name=name,
    )(x, a, s, w_in, tables)


def _ret_kernel(q_ref, k_ref, v_ref, g_ref, kc_ref, vc_ref, lg_ref, gn_ref, o_ref, u_ref, s_ref,
                *, n_lat, n_all):
    c = RET_CHUNK
    dk = HEAD_W
    row = lax.broadcasted_iota(jnp.int32, (c, c), 0).astype(F32)
    col = lax.broadcasted_iota(jnp.int32, (c, c), 1).astype(F32)
    tok = lax.broadcasted_iota(jnp.int32, (c, dk), 0).astype(F32)
    lg_f = lg_ref[0:1, :]
    lg_b = lg_ref[1:2, :]
    lg_f2 = jnp.concatenate([lg_f, lg_f], axis=1)
    lg_b2 = jnp.concatenate([lg_b, lg_b], axis=1)
    mask_f = jnp.where(col >= row, jnp.exp(jnp.maximum(col - row, 0.0) * lg_f2), 0.0).astype(BF16)
    mask_b = jnp.where(row >= col, jnp.exp(jnp.maximum(row - col, 0.0) * lg_b2), 0.0).astype(BF16)
    qdec_f = jnp.exp((tok + 1.0) * lg_f)
    qdec_b = jnp.exp((c - tok) * lg_b)
    kdec_f = jnp.exp((c - 1.0 - tok) * lg_f)
    kdec_b = jnp.exp(tok * lg_b)
    cdec_f = jnp.exp(c * lg_f)
    cdec_b = jnp.exp(c * lg_b)
    tn_dims = (((0,), (0,)), ((), ()))
    nt_dims = (((1,), (1,)), ((), ()))

    def rows(idx):
        return slice(idx * c, (idx + 1) * c)

    def kv_chunk(idx):
        if idx < n_lat:
            return k_ref[rows(idx), :], v_ref[rows(idx), :]
        return kc_ref[rows(idx - n_lat), :], vc_ref[rows(idx - n_lat), :]

    for idx in range(n_all):
        k, v = kv_chunk(idx)
        kf = k.astype(F32)
        kd = jnp.concatenate([(kf * kdec_f).astype(BF16), (kf * kdec_b).astype(BF16)], axis=1)
        u_ref[idx] = lax.dot_general(v, kd, tn_dims, preferred_element_type=F32)

    state = jnp.zeros((dk, dk), F32)
    for idx in [*range(n_lat, n_all), *range(n_lat)]:
        s_ref[idx, :, 0:dk] = state.astype(BF16)
        state = state * cdec_f + u_ref[idx, :, 0:dk]
    state = jnp.zeros((dk, dk), F32)
    for idx in reversed(range(n_all)):
        s_ref[idx, :, dk:2 * dk] = state.astype(BF16)
        state = state * cdec_b + u_ref[idx, :, dk:2 * dk]

    gn_f = jnp.broadcast_to(gn_ref[0], (dk, c))
    gn_b = jnp.broadcast_to(gn_ref[1], (dk, c))

    def group_norm_t(o, g):
        mu = jnp.mean(o, axis=0, keepdims=True)
        d = o - mu
        var = jnp.mean(d * d, axis=0, keepdims=True)
        return d * lax.rsqrt(var + GN_EPS) * g

    def scores_t(idx):
        return lax.dot_general(k_ref[rows(idx), :], q_ref[rows(idx), :], nt_dims,
                               preferred_element_type=F32)

    ahead = [scores_t(i) for i in range(min(RET_LOOKAHEAD, n_lat))]
    for idx in range(n_lat):
        a_t = ahead.pop(0)
        if idx + RET_LOOKAHEAD < n_lat:
            ahead.append(scores_t(idx + RET_LOOKAHEAD))
        v = v_ref[rows(idx), :]
        qf = q_ref[rows(idx), :].astype(F32)
        a_b = a_t.astype(BF16)
        o_f = (lax.dot_general(v, a_b * mask_f, tn_dims, preferred_element_type=F32)
               + lax.dot_general(s_ref[idx, :, 0:dk], (qf * qdec_f).astype(BF16), nt_dims,
                                 preferred_element_type=F32))
        o_b = (lax.dot_general(v, a_b * mask_b, tn_dims, preferred_element_type=F32)
               + lax.dot_general(s_ref[idx, :, dk:2 * dk], (qf * qdec_b).astype(BF16), nt_dims,
                                 preferred_element_type=F32))
        n_t = group_norm_t(o_f, gn_f) + group_norm_t(o_b, gn_b)
        y = _silu(g_ref[rows(idx), :].astype(F32)) * n_t.T
        o_ref[rows(idx), :] = y.astype(BF16)


def _retention(p_lat, p_ctx, lgv, gn_cols):
    b, l_lat, _ = p_lat.shape
    l_ctx = p_ctx.shape[1]
    n_lat = l_lat // RET_CHUNK
    n_all = n_lat + l_ctx // RET_CHUNK
    col = lambda rows, g: pl.BlockSpec((None, rows, HEAD_W), lambda i, h: (i, 0, g * RET_HEADS + h))
    kern = functools.partial(_ret_kernel, n_lat=n_lat, n_all=n_all)
    return pl.pallas_call(
        kern,
        grid=(b, RET_HEADS),
        in_specs=[col(l_lat, 0), col(l_lat, 1), col(l_lat, 2), col(l_lat, 3),
                  col(l_ctx, CTX_GROUPS.index(1)), col(l_ctx, CTX_GROUPS.index(2)),
                  pl.BlockSpec((None, 2, HEAD_W), lambda i, h: (h, 0, 0)),
                  pl.BlockSpec((None, 2, HEAD_W, 1), lambda i, h: (h, 0, 0, 0))],
        out_specs=pl.BlockSpec((None, l_lat, HEAD_W), lambda i, h: (i, 0, h)),
        out_shape=jax.ShapeDtypeStruct((b, l_lat, RET_HEADS * HEAD_W), BF16),
        scratch_shapes=[pltpu.VMEM((n_all, HEAD_W, 2 * HEAD_W), F32),
                        pltpu.VMEM((n_all, HEAD_W, 2 * HEAD_W), BF16)],
        compiler_params=_cparams("arbitrary", "arbitrary"),
        name="ret",
    )(p_lat, p_lat, p_lat, p_lat, p_ctx, p_ctx, lgv, gn_cols)


PLAIN_EXP_MAX_LOG2 = 80.0
PLAIN_EXP_MAX_VALUE = 2.0 ** 30
ONES_ROWS = 16


def _attn_kernel(q_ref, k_ref, v_ref, kc_ref, vc_ref, lam_ref, g_ref, o_ref, vt_ref, kv_stat_ref,
                 *, tq, tk, l_lat, l_ctx):
    dv = HEAD_W
    l_all = l_lat + l_ctx
    spans = ([(False, o, min(tk, l_lat - o)) for o in range(0, l_lat, tk)]
             + [(True, o, min(tk, l_ctx - o)) for o in range(0, l_ctx, tk)])
    n_k = len(spans)

    chunks = ([(k_ref, v_ref, o, o) for o in range(0, l_lat, RET_CHUNK)]
              + [(kc_ref, vc_ref, o, l_lat + o) for o in range(0, l_ctx, RET_CHUNK)])

    @pl.when(pl.program_id(2) == 0)
    def _():
        kmax = vmax = jnp.zeros((1, HEAD_W), F32)
        for kr, vr, o, dst in chunks:
            vf = vr[o:o + RET_CHUNK, :].astype(F32)
            vt_ref[0:dv, dst:dst + RET_CHUNK] = vf.T.astype(BF16)
            vmax = jnp.maximum(vmax, jnp.max(jnp.abs(vf), axis=0, keepdims=True))
            kf = kr[o:o + RET_CHUNK, :].astype(F32)
            kmax = jnp.maximum(kmax, jnp.max(jnp.abs(kf), axis=0, keepdims=True))
        vt_ref[dv:dv + ONES_ROWS, :] = jnp.ones((ONES_ROWS, l_all), BF16)
        kv_stat_ref[0:1, :] = kmax
        kv_stat_ref[1:2, :] = vmax

    lane = lax.broadcasted_iota(jnp.int32, (tq, HEAD_W), 1)
    q = q_ref[...]
    zero = jnp.zeros_like(q)
    q1 = jnp.where(lane < DIFF_HD, q, zero)
    q2 = jnp.where(lane >= DIFF_HD, q, zero)
    nt_dims = (((1,), (1,)), ((), ()))

    r = lax.broadcasted_iota(jnp.int32, (HEAD_W, HEAD_W), 0)
    c = lax.broadcasted_iota(jnp.int32, (HEAD_W, HEAD_W), 1)
    same_map = jnp.where((r < DIFF_HD) == (c < DIFF_HD), 1.0, 0.0).astype(BF16)
    q_abs = (jnp.abs(q.astype(F32)) * kv_stat_ref[0:1, :]).astype(BF16)
    bound = jnp.dot(q_abs, same_map, preferred_element_type=F32)
    plain = jnp.logical_and(jnp.max(bound) <= PLAIN_EXP_MAX_LOG2,
                            jnp.max(kv_stat_ref[1:2, :]) <= PLAIN_EXP_MAX_VALUE)

    def scores(i):
        is_ctx, o, size = spans[i]
        kt = (kc_ref if is_ctx else k_ref)[o:o + size, :]
        return (lax.dot_general(kt, q1, nt_dims, preferred_element_type=F32),
                lax.dot_general(kt, q2, nt_dims, preferred_element_type=F32))

    def update_online(s, vt, state):
        m, acc = state
        m_new = jnp.maximum(m, jnp.max(s, axis=0, keepdims=True))
        alpha = jnp.exp2(m - m_new)
        pr = jnp.exp2(s - m_new).astype(BF16)
        return m_new, alpha * acc + jnp.dot(vt, pr, preferred_element_type=F32)

    def update_plain(s, vt, state):
        l, acc = state
        pr = jnp.exp2(s)
        return (l + jnp.sum(pr, axis=0, keepdims=True),
                acc + jnp.dot(vt[0:dv], pr.astype(BF16), preferred_element_type=F32))

    def attend(update, init):
        st1 = st2 = init
        ahead = [scores(i) for i in range(min(ATTN_LOOKAHEAD, n_k))]
        for i in range(n_k):
            s1, s2 = ahead.pop(0)
            if i + ATTN_LOOKAHEAD < n_k:
                ahead.append(scores(i + ATTN_LOOKAHEAD))
            is_ctx, o, size = spans[i]
            start = (l_lat if is_ctx else 0) + o
            vt = vt_ref[:, start:start + size]
            st1 = update(s1, vt, st1)
            st2 = update(s2, vt, st2)
        return st1, st2

    def finish(o1, l1, o2, l2):
        lp = lam_ref[...]
        lam = (jnp.exp(jnp.sum(lp[0:1, :] * lp[1:2, :], axis=-1, keepdims=True))
               - jnp.exp(jnp.sum(lp[2:3, :] * lp[3:4, :], axis=-1, keepdims=True)) + LAMBDA_INIT)
        o = o1 / l1 - lam * (o2 / l2)
        n = o * lax.rsqrt(jnp.mean(o * o, axis=0, keepdims=True) + EPS)
        o_ref[...] = (n.T * g_ref[...] * (1.0 - LAMBDA_INIT)).astype(BF16)

    @pl.when(plain)
    def _():
        (l1, acc1), (l2, acc2) = attend(
            update_plain, (jnp.zeros((1, tq), F32), jnp.zeros((dv, tq), F32)))
        finish(acc1, l1, acc2, l2)

    @pl.when(jnp.logical_not(plain))
    def _():
        (_, acc1), (_, acc2) = attend(
            update_online, (jnp.full((1, tq), -jnp.inf, F32), jnp.zeros((dv + ONES_ROWS, tq), F32)))
        finish(acc1[0:dv], acc1[dv:dv + 1], acc2[0:dv], acc2[dv:dv + 1])


def _diff_attention(p_lat, p_ctx, lam_par, subln_g, *, tq, tk):
    b, l_lat, _ = p_lat.shape
    l_ctx = p_ctx.shape[1]
    kern = functools.partial(_attn_kernel, tq=tq, tk=tk, l_lat=l_lat, l_ctx=l_ctx)
    col = lambda rows, g: pl.BlockSpec((None, rows, HEAD_W), lambda i, h, j: (i, 0, g * DIFF_HEADS + h))
    return pl.pallas_call(
        kern,
        grid=(b, DIFF_HEADS, l_lat // tq),
        in_specs=[pl.BlockSpec((None, tq, HEAD_W), lambda i, h, j: (i, j, 4 * DIFF_HEADS + h)),
                  col(l_lat, 5), col(l_lat, 6),
                  col(l_ctx, CTX_GROUPS.index(5)), col(l_ctx, CTX_GROUPS.index(6)),
                  pl.BlockSpec((4, DIFF_HD), lambda i, h, j: (0, 0)),
                  pl.BlockSpec((1, HEAD_W), lambda i, h, j: (0, h))],
        out_specs=pl.BlockSpec((None, tq, HEAD_W), lambda i, h, j: (i, j, h)),
        out_shape=jax.ShapeDtypeStruct((b, l_lat, DIFF_HEADS * HEAD_W), BF16),
        scratch_shapes=[pltpu.VMEM((HEAD_W + ONES_ROWS, l_lat + l_ctx), BF16),
                        pltpu.VMEM((8, HEAD_W), F32)],
        compiler_params=_cparams("arbitrary", "arbitrary", "arbitrary"),
        name="attn",
    )(p_lat, p_lat, p_lat, p_ctx, p_ctx, lam_par, subln_g)


HALO = 16


def _ffn_kernel(xp_ref, x_ref, xn_ref, rp_ref, r_ref, rn_ref, dp_ref, d_ref, dn_ref,
                wo_ref, g1_ref, a2_ref, s2_ref, wu_ref, cw_ref, cb_ref, wd_ref, g2_ref, fg_ref,
                o_ref, act_ref, *, tm, d_ff, tn):
    t = pl.program_id(1)
    nt = pl.num_programs(1)
    rows = tm + 2 * HALO
    half = r_ref.shape[-1]
    xs = jnp.concatenate([xp_ref[...], x_ref[...], xn_ref[...]], axis=0)
    yr = jnp.concatenate([rp_ref[...], r_ref[...], rn_ref[...]], axis=0)
    yd = jnp.concatenate([dp_ref[...], d_ref[...], dn_ref[...]], axis=0)
    y = (jnp.dot(yr, wo_ref[:half, :], preferred_element_type=F32)
         + jnp.dot(yd, wo_ref[half:, :], preferred_element_type=F32))
    x1s = xs + g1_ref[...] * y
    h2 = (x1s * lax.rsqrt(jnp.mean(x1s * x1s, axis=-1, keepdims=True) + EPS)) * a2_ref[...] + s2_ref[...]
    row = lax.broadcasted_iota(jnp.int32, (rows, 1), 0)
    inside = ((row >= HALO) | (t > 0)) & ((row < tm + HALO) | (t < nt - 1))
    hh = jnp.where(inside, h2, 0.0).astype(BF16)
    for j in range(d_ff // tn):
        for part in range(2):
            c0 = part * d_ff + j * tn
            u = jnp.dot(hh, wu_ref[:, c0:c0 + tn], preferred_element_type=F32)
            w = cw_ref[:, c0:c0 + tn]
            conv = (pltpu.roll(u, 1, axis=0)[HALO:HALO + tm] * w[0:1, :]
                    + u[HALO:HALO + tm] * w[1:2, :]
                    + pltpu.roll(u, rows - 1, axis=0)[HALO:HALO + tm] * w[2:3, :]
                    + cb_ref[:, c0:c0 + tn])
            if part == 0:
                gate = _silu(conv)
            else:
                act_ref[:, j * tn:(j + 1) * tn] = (gate * conv).astype(BF16)
    rb = tm // 2
    for r in range(2):
        f = jnp.dot(act_ref[r * rb:(r + 1) * rb, :], wd_ref[...], preferred_element_type=F32)
        x2 = x1s[HALO + r * rb:HALO + (r + 1) * rb, :] + g2_ref[...] * f
        o_ref[r * rb:(r + 1) * rb, :] = (
            x2 * lax.rsqrt(jnp.mean(x2 * x2, axis=-1, keepdims=True) + EPS) * fg_ref[...])


def _ffn(x, y_ret, y_diff, w_out, g1, a2, s2, w_up, conv_w, conv_b, w_down, g2, final_g, *, tm, tn):
    b, l, d = x.shape
    half = y_ret.shape[-1]
    d_ff = w_down.shape[0]
    nh = tm // HALO
    n_halo_blocks = l // HALO
    kern = functools.partial(_ffn_kernel, tm=tm, d_ff=d_ff, tn=tn)

    def halo_specs(w):
        return [pl.BlockSpec((None, HALO, w), lambda i, t: (i, jnp.maximum(t * nh - 1, 0), 0)),
                pl.BlockSpec((None, tm, w), lambda i, t: (i, t, 0)),
                pl.BlockSpec((None, HALO, w), lambda i, t: (i, jnp.minimum((t + 1) * nh, n_halo_blocks - 1), 0))]

    full = lambda a: pl.BlockSpec(a.shape, lambda i, t: (0,) * a.ndim, pipeline_mode=pl.Buffered(1))
    vec = pl.BlockSpec((None, 1, d), lambda i, t: (i, 0, 0))
    return pl.pallas_call(
        kern,
        grid=(b, l // tm),
        in_specs=[*halo_specs(d), *halo_specs(half), *halo_specs(half),
                  full(w_out), vec, vec, vec,
                  full(w_up), full(conv_w), full(conv_b), full(w_down), vec, full(final_g)],
        out_specs=pl.BlockSpec((None, tm, d), lambda i, t: (i, t, 0)),
        out_shape=jax.ShapeDtypeStruct((b, l, d), F32),
        scratch_shapes=[pltpu.VMEM((tm, d_ff), BF16)],
        compiler_params=_cparams("arbitrary", "arbitrary"),
        name="ffn",
    )(x, x, x, y_ret, y_ret, y_ret, y_diff, y_diff, y_diff, w_out, g1, a2, s2,
      w_up, conv_w, conv_b, w_down, g2, final_g)


def _rope_tables(l):
    pos = np.arange(l)
    row = (pos // GRID_W).astype(np.float64)[:, None]
    col = (pos % GRID_W).astype(np.float64)[:, None]

    def angles(head_dim):
        n_freq = head_dim // 4
        inv = ROPE_BASE ** (-np.arange(n_freq, dtype=np.float64) / n_freq)
        return np.concatenate([row * inv, col * inv], axis=-1)

    ar = angles(RET_DK)
    ad = angles(DIFF_HD)
    cr, sr = np.cos(ar), np.sin(ar)
    cd, sd = np.cos(ad), np.sin(ad)
    zd = np.zeros_like(sd)
    return jnp.asarray(np.concatenate([
        cr, cr, -sr, sr,
        cd, cd, cd, cd,
        -sd, zd, -sd, zd,
        zd, sd, zd, sd], axis=-1), F32)


def _identity_tables(l):
    one = np.ones((l, HEAD_W), np.float32)
    zero = np.zeros((l, HEAD_W), np.float32)
    return jnp.asarray(np.concatenate([one, zero, one, zero, zero], axis=-1))


def kernel(x, c, ctx, c_ctx, w_mod, b_mod, norm1_g, w_in, ret_decay_logit, ret_gn_g, diff_lambda,
           diff_subln_g, w_out, norm2_g, w_up, conv_w, conv_b, w_down, final_g):
    b, l, d = x.shape
    l_ctx = ctx.shape[1]

    rows = -(-(b + 1) // 8) * 8
    cc = jnp.zeros((rows, d), F32).at[:b].set(c).at[b].set(c_ctx)
    m = _modulation(cc, w_mod[0], b_mod[0][None, :])
    sh1, sc1, g1, sh2, sc2, g2 = [m[:, i * d:(i + 1) * d] for i in range(N_MOD)]
    a1 = norm1_g[0][None, :] * (1.0 + sc1)
    a2 = norm2_g[0][None, :] * (1.0 + sc2)
    vec = lambda v: v[:b, None, :]
    ctx_vec = lambda v: jnp.broadcast_to(v[b][None, None, :], (b, 1, d))

    w_in_b = w_in[0].astype(BF16)
    p_lat = _inproj(x, vec(a1), vec(sh1), w_in_b, _rope_tables(l), tm=min(1024, l),
                    groups=LAT_GROUPS, name="inproj_lat")
    p_ctx = _inproj(ctx, ctx_vec(a1), ctx_vec(sh1), w_in_b, _identity_tables(l_ctx), tm=l_ctx,
                    groups=CTX_GROUPS, name="inproj_ctx")

    lg = jax.nn.log_sigmoid(ret_decay_logit[0].astype(F32))
    lgv = jnp.broadcast_to(lg.T[:, :, None], (RET_HEADS, 2, HEAD_W))
    gn_cols = ret_gn_g[0].reshape(2, RET_HEADS, HEAD_W).transpose(1, 0, 2)[..., None]
    y_ret = _retention(p_lat, p_ctx, lgv, gn_cols)
    y_diff = _diff_attention(p_lat, p_ctx, diff_lambda[0], diff_subln_g[0][None, :],
                             tq=min(2048, l), tk=512)

    return _ffn(x, y_ret, y_diff, w_out[0].astype(BF16), vec(g1), vec(a2), vec(sh2),
                w_up[0].astype(BF16), conv_w[0], conv_b[0][None, :], w_down[0].astype(BF16),
                vec(g2), final_g[None, :], tm=min(512, l), tn=256)
```

```python
import functools
import math

import jax
import jax.numpy as jnp
import numpy as np
from jax import lax
from jax.experimental import pallas as pl
from jax.experimental.pallas import tpu as pltpu

F32 = jnp.float32
BF16 = jnp.bfloat16

GRID_W = 64
RET_HEADS = 4
RET_DK = 128
DIFF_HEADS = 4
DIFF_HD = 64
HEAD_W = 128
GROUP_W = 512
LAT_GROUPS = (0, 1, 2, 3, 4, 5, 6)
CTX_GROUPS = LAT_GROUPS
RET_CHUNK = 256
ROPE_BASE = 10000.0
EPS = 1e-6
GN_EPS = 1e-5
N_MOD = 6
LAMBDA_INIT = 0.8 - 0.6 * math.exp(-0.3 * 0)
LOG2E = math.log2(math.e)
RET_LOOKAHEAD = 2
ATTN_LOOKAHEAD = 1

VMEM_LIMIT = 56 * 1024 * 1024


def _cparams(*sem):
    return pltpu.CompilerParams(dimension_semantics=sem, vmem_limit_bytes=VMEM_LIMIT)


def _silu(v):
    return v * (1.0 / (1.0 + jnp.exp(-v)))


def _split_dot(a, w):
    a_hi = a.astype(BF16)
    a_lo = (a - a_hi.astype(F32)).astype(BF16)
    w_hi = w.astype(BF16)
    w_lo = (w - w_hi.astype(F32)).astype(BF16)
    dot = functools.partial(jnp.dot, preferred_element_type=F32)
    return dot(a_hi, w_hi) + (dot(a_hi, w_lo) + dot(a_lo, w_hi))


def _mod_kernel(c_ref, w_ref, b_ref, o_ref):
    o_ref[...] = _split_dot(_silu(c_ref[...]), w_ref[...]) + b_ref[...]


def _modulation(cc, w_mod, b_mod):
    rows, d = cc.shape
    n = w_mod.shape[1]
    tn = 1024
    return pl.pallas_call(
        _mod_kernel,
        grid=(n // tn,),
        in_specs=[pl.BlockSpec((rows, d), lambda j: (0, 0)),
                  pl.BlockSpec((d, tn), lambda j: (0, j)),
                  pl.BlockSpec((1, tn), lambda j: (0, j))],
        out_specs=pl.BlockSpec((rows, tn), lambda j: (0, j)),
        out_shape=jax.ShapeDtypeStruct((rows, n), F32),
        compiler_params=_cparams("arbitrary"),
        name="mod",
    )(cc, w_mod, b_mod)


def _rope128(v, cos, sin):
    return v * cos + pltpu.roll(v, 64, axis=1) * sin


def _rope64(v, cos, sin_lo, sin_hi):
    return v * cos + pltpu.roll(v, 96, axis=1) * sin_lo + pltpu.roll(v, 32, axis=1) * sin_hi


def _inproj_kernel(x_ref, a_ref, s_ref, w_ref, t_ref, o_ref, *, groups):
    x = x_ref[...]
    xn = x * lax.rsqrt(jnp.mean(x * x, axis=-1, keepdims=True) + EPS)
    h = (xn * a_ref[...] + s_ref[...]).astype(BF16)
    cos_r = t_ref[:, 0 * HEAD_W:1 * HEAD_W]
    sin_r = t_ref[:, 1 * HEAD_W:2 * HEAD_W]
    cos_d = t_ref[:, 2 * HEAD_W:3 * HEAD_W]
    sin_dl = t_ref[:, 3 * HEAD_W:4 * HEAD_W]
    sin_dh = t_ref[:, 4 * HEAD_W:5 * HEAD_W]
    for slot, g in enumerate(groups):
        acc = jnp.dot(h, w_ref[:, g * GROUP_W:(g + 1) * GROUP_W], preferred_element_type=F32)
        for hd in range(GROUP_W // HEAD_W):
            v = acc[:, hd * HEAD_W:(hd + 1) * HEAD_W]
            if g == 0:
                v = _rope128(v, cos_r, sin_r)
            elif g == 1:
                v = _rope128(v, cos_r, sin_r) * (RET_DK ** -0.5)
            elif g == 4:
                v = _rope64(v, cos_d, sin_dl, sin_dh) * (DIFF_HD ** -0.5 * LOG2E)
            elif g == 5:
                v = _rope64(v, cos_d, sin_dl, sin_dh)
            c0 = slot * GROUP_W + hd * HEAD_W
            o_ref[:, c0:c0 + HEAD_W] = v.astype(BF16)


def _inproj(x, a, s, w_in, tables, *, tm, groups, name):
    b, lx, d = x.shape
    n_in = w_in.shape[1]
    n_cols = len(groups) * GROUP_W
    return pl.pallas_call(
        functools.partial(_inproj_kernel, groups=groups),
        grid=(lx // tm, b),
        in_specs=[pl.BlockSpec((None, tm, d), lambda t, i: (i, t, 0)),
                  pl.BlockSpec((None, 1, d), lambda t, i: (i, 0, 0)),
                  pl.BlockSpec((None, 1, d), lambda t, i: (i, 0, 0)),
                  pl.BlockSpec((d, n_in), lambda t, i: (0, 0), pipeline_mode=pl.Buffered(1)),
                  pl.BlockSpec((tm, 5 * HEAD_W), lambda t, i: (t, 0))],
        out_specs=pl.BlockSpec((None, tm, n_cols), lambda t, i: (i, t, 0)),
        out_shape=jax.ShapeDtypeStruct((b, lx, n_cols), BF16),
        compiler_params=_cparams("arbitrary", "arbitrary"),
        name=name,
    )(x, a, s, w_in, tables)


def _ret_kernel(q_ref, k_ref, v_ref, g_ref, kc_ref, vc_ref, lg_ref, gn_ref, o_ref, u_ref, s_ref,
                *, n_lat, n_all):
    c = RET_CHUNK
    dk = HEAD_W
    row = lax.broadcasted_iota(jnp.int32, (c, c), 0).astype(F32)
    col = lax.broadcasted_iota(jnp.int32, (c, c), 1).astype(F32)
    tok = lax.broadcasted_iota(jnp.int32, (c, dk), 0).astype(F32)
    lg_f = lg_ref[0:1, :]
    lg_b = lg_ref[1:2, :]
    lg_f2 = jnp.concatenate([lg_f, lg_f], axis=1)
    lg_b2 = jnp.concatenate([lg_b, lg_b], axis=1)
    mask_f = jnp.where(col >= row, jnp.exp(jnp.maximum(col - row, 0.0) * lg_f2), 0.0).astype(BF16)
    mask_b = jnp.where(row >= col, jnp.exp(jnp.maximum(row - col, 0.0) * lg_b2), 0.0).astype(BF16)
    qdec_f = jnp.exp((tok + 1.0) * lg_f)
    qdec_b = jnp.exp((c - tok) * lg_b)
    kdec_f = jnp.exp((c - 1.0 - tok) * lg_f)
    kdec_b = jnp.exp(tok * lg_b)
    cdec_f = jnp.exp(c * lg_f)
    cdec_b = jnp.exp(c * lg_b)
    tn_dims = (((0,), (0,)), ((), ()))
    nt_dims = (((1,), (1,)), ((), ()))

    def rows(idx):
        return slice(idx * c, (idx + 1) * c)

    def kv_chunk(idx):
        if idx < n_lat:
            return k_ref[rows(idx), :], v_ref[rows(idx), :]
        return kc_ref[rows(idx - n_lat), :], vc_ref[rows(idx - n_lat), :]

    for idx in range(n_all):
        k, v = kv_chunk(idx)
        kf = k.astype(F32)
        kd = jnp.concatenate([(kf * kdec_f).astype(BF16), (kf * kdec_b).astype(BF16)], axis=1)
        u_ref[idx] = lax.dot_general(v, kd, tn_dims, preferred_element_type=F32)

    state = jnp.zeros((dk, dk), F32)
    for idx in [*range(n_lat, n_all), *range(n_lat)]:
        s_ref[idx, :, 0:dk] = state.astype(BF16)
        state = state * cdec_f + u_ref[idx, :, 0:dk]
    state = jnp.zeros((dk, dk), F32)
    for idx in reversed(range(n_all)):
        s_ref[idx, :, dk:2 * dk] = state.astype(BF16)
        state = state * cdec_b + u_ref[idx, :, dk:2 * dk]

    gn_f = jnp.broadcast_to(gn_ref[0], (dk, c))
    gn_b = jnp.broadcast_to(gn_ref[1], (dk, c))

    def group_norm_t(o, g):
        mu = jnp.mean(o, axis=0, keepdims=True)
        d = o - mu
        var = jnp.mean(d * d, axis=0, keepdims=True)
        return d * lax.rsqrt(var + GN_EPS) * g

    def scores_t(idx):
        return lax.dot_general(k_ref[rows(idx), :], q_ref[rows(idx), :], nt_dims,
                               preferred_element_type=F32)

    ahead = [scores_t(i) for i in range(min(RET_LOOKAHEAD, n_lat))]
    for idx in range(n_lat):
        a_t = ahead.pop(0)
        if idx + RET_LOOKAHEAD < n_lat:
            ahead.append(scores_t(idx + RET_LOOKAHEAD))
        v = v_ref[rows(idx), :]
        qf = q_ref[rows(idx), :].astype(F32)
        a_b = a_t.astype(BF16)
        o_f = (lax.dot_general(v, a_b * mask_f, tn_dims, preferred_element_type=F32)
               + lax.dot_general(s_ref[idx, :, 0:dk], (qf * qdec_f).astype(BF16), nt_dims,
                                 preferred_element_type=F32))
        o_b = (lax.dot_general(v, a_b * mask_b, tn_dims, preferred_element_type=F32)
               + lax.dot_general(s_ref[idx, :, dk:2 * dk], (qf * qdec_b).astype(BF16), nt_dims,
                                 preferred_element_type=F32))
        n_t = group_norm_t(o_f, gn_f) + group_norm_t(o_b, gn_b)
        y = _silu(g_ref[rows(idx), :].astype(F32)) * n_t.T
        o_ref[rows(idx), :] = y.astype(BF16)


def _retention(p_lat, p_ctx, lgv, gn_cols):
    b, l_lat, _ = p_lat.shape
    l_ctx = p_ctx.shape[1]
    n_lat = l_lat // RET_CHUNK
    n_all = n_lat + l_ctx // RET_CHUNK
    col = lambda rows, g: pl.BlockSpec((None, rows, HEAD_W), lambda i, h: (i, 0, g * RET_HEADS + h))
    kern = functools.partial(_ret_kernel, n_lat=n_lat, n_all=n_all)
    return pl.pallas_call(
        kern,
        grid=(b, RET_HEADS),
        in_specs=[col(l_lat, 0), col(l_lat, 1), col(l_lat, 2), col(l_lat, 3),
                  col(l_ctx, CTX_GROUPS.index(1)), col(l_ctx, CTX_GROUPS.index(2)),
                  pl.BlockSpec((None, 2, HEAD_W), lambda i, h: (h, 0, 0)),
                  pl.BlockSpec((None, 2, HEAD_W, 1), lambda i, h: (h, 0, 0, 0))],
        out_specs=pl.BlockSpec((None, l_lat, HEAD_W), lambda i, h: (i, 0, h)),
        out_shape=jax.ShapeDtypeStruct((b, l_lat, RET_HEADS * HEAD_W), BF16),
        scratch_shapes=[pltpu.VMEM((n_all, HEAD_W, 2 * HEAD_W), F32),
                        pltpu.VMEM((n_all, HEAD_W, 2 * HEAD_W), BF16)],
        compiler_params=_cparams("arbitrary", "arbitrary"),
        name="ret",
    )(p_lat, p_lat, p_lat, p_lat, p_ctx, p_ctx, lgv, gn_cols)


PLAIN_EXP_MAX_LOG2 = 80.0
PLAIN_EXP_MAX_VALUE = 2.0 ** 30
ONES_ROWS = 16


def _attn_kernel(q_ref, k_ref, v_ref, kc_ref, vc_ref, lam_ref, g_ref, o_ref, vt_ref, kv_stat_ref,
                 *, tq, tk, l_lat, l_ctx):
    dv = HEAD_W
    l_all = l_lat + l_ctx
    spans = ([(False, o, min(tk, l_lat - o)) for o in range(0, l_lat, tk)]
             + [(True, o, min(tk, l_ctx - o)) for o in range(0, l_ctx, tk)])
    n_k = len(spans)

    chunks = ([(k_ref, v_ref, o, o) for o in range(0, l_lat, RET_CHUNK)]
              + [(kc_ref, vc_ref, o, l_lat + o) for o in range(0, l_ctx, RET_CHUNK)])

    @pl.when(pl.program_id(2) == 0)
    def _():
        kmax = vmax = jnp.zeros((1, HEAD_W), F32)
        for kr, vr, o, dst in chunks:
            vf = vr[o:o + RET_CHUNK, :].astype(F32)
            vt_ref[0:dv, dst:dst + RET_CHUNK] = vf.T.astype(BF16)
            vmax = jnp.maximum(vmax, jnp.max(jnp.abs(vf), axis=0, keepdims=True))
            kf = kr[o:o + RET_CHUNK, :].astype(F32)
            kmax = jnp.maximum(kmax, jnp.max(jnp.abs(kf), axis=0, keepdims=True))
        vt_ref[dv:dv + ONES_ROWS, :] = jnp.ones((ONES_ROWS, l_all), BF16)
        kv_stat_ref[0:1, :] = kmax
        kv_stat_ref[1:2, :] = vmax

    lane = lax.broadcasted_iota(jnp.int32, (tq, HEAD_W), 1)
    q = q_ref[...]
    zero = jnp.zeros_like(q)
    q1 = jnp.where(lane < DIFF_HD, q, zero)
    q2 = jnp.where(lane >= DIFF_HD, q, zero)
    nt_dims = (((1,), (1,)), ((), ()))

    r = lax.broadcasted_iota(jnp.int32, (HEAD_W, HEAD_W), 0)
    c = lax.broadcasted_iota(jnp.int32, (HEAD_W, HEAD_W), 1)
    same_map = jnp.where((r < DIFF_HD) == (c < DIFF_HD), 1.0, 0.0).astype(BF16)
    q_abs = (jnp.abs(q.astype(F32)) * kv_stat_ref[0:1, :]).astype(BF16)
    bound = jnp.dot(q_abs, same_map, preferred_element_type=F32)
    plain = jnp.logical_and(jnp.max(bound) <= PLAIN_EXP_MAX_LOG2,
                            jnp.max(kv_stat_ref[1:2, :]) <= PLAIN_EXP_MAX_VALUE)

    def scores(i):
        is_ctx, o, size = spans[i]
        kt = (kc_ref if is_ctx else k_ref)[o:o + size, :]
        return (lax.dot_general(kt, q1, nt_dims, preferred_element_type=F32),
                lax.dot_general(kt, q2, nt_dims, preferred_element_type=F32))

    def update_online(s, vt, state):
        m, acc = state
        m_new = jnp.maximum(m, jnp.max(s, axis=0, keepdims=True))
        alpha = jnp.exp2(m - m_new)
        pr = jnp.exp2(s - m_new).astype(BF16)
        return m_new, alpha * acc + jnp.dot(vt, pr, preferred_element_type=F32)

    def update_plain(s, vt, state):
        l, acc = state
        pr = jnp.exp2(s)
        return (l + jnp.sum(pr, axis=0, keepdims=True),
                acc + jnp.dot(vt[0:dv], pr.astype(BF16), preferred_element_type=F32))

    def attend(update, init):
        st1 = st2 = init
        ahead = [scores(i) for i in range(min(ATTN_LOOKAHEAD, n_k))]
        for i in range(n_k):
            s1, s2 = ahead.pop(0)
            if i + ATTN_LOOKAHEAD < n_k:
                ahead.append(scores(i + ATTN_LOOKAHEAD))
            is_ctx, o, size = spans[i]
            start = (l_lat if is_ctx else 0) + o
            vt = vt_ref[:, start:start + size]
            st1 = update(s1, vt, st1)
            st2 = update(s2, vt, st2)
        return st1, st2

    def finish(o1, l1, o2, l2):
        lp = lam_ref[...]
        lam = (jnp.exp(jnp.sum(lp[0:1, :] * lp[1:2, :], axis=-1, keepdims=True))
               - jnp.exp(jnp.sum(lp[2:3, :] * lp[3:4, :], axis=-1, keepdims=True)) + LAMBDA_INIT)
        o = o1 / l1 - lam * (o2 / l2)
        n = o * lax.rsqrt(jnp.mean(o * o, axis=0, keepdims=True) + EPS)
        o_ref[...] = (n.T * g_ref[...] * (1.0 - LAMBDA_INIT)).astype(BF16)

    @pl.when(plain)
    def _():
        (l1, acc1), (l2, acc2) = attend(
            update_plain, (jnp.zeros((1, tq), F32), jnp.zeros((dv, tq), F32)))
        finish(acc1, l1, acc2, l2)

    @pl.when(jnp.logical_not(plain))
    def _():
        (_, acc1), (_, acc2) = attend(
            update_online, (jnp.full((1, tq), -jnp.inf, F32), jnp.zeros((dv + ONES_ROWS, tq), F32)))
        finish(acc1[0:dv], acc1[dv:dv + 1], acc2[0:dv], acc2[dv:dv + 1])


def _diff_attention(p_lat, p_ctx, lam_par, subln_g, *, tq, tk):
    b, l_lat, _ = p_lat.shape
    l_ctx = p_ctx.shape[1]
    kern = functools.partial(_attn_kernel, tq=tq, tk=tk, l_lat=l_lat, l_ctx=l_ctx)
    col = lambda rows, g: pl.BlockSpec((None, rows, HEAD_W), lambda i, h, j: (i, 0, g * DIFF_HEADS + h))
    return pl.pallas_call(
        kern,
        grid=(b, DIFF_HEADS, l_lat // tq),
        in_specs=[pl.BlockSpec((None, tq, HEAD_W), lambda i, h, j: (i, j, 4 * DIFF_HEADS + h)),
                  col(l_lat, 5), col(l_lat, 6),
                  col(l_ctx, CTX_GROUPS.index(5)), col(l_ctx, CTX_GROUPS.index(6)),
                  pl.BlockSpec((4, DIFF_HD), lambda i, h, j: (0, 0)),
                  pl.BlockSpec((1, HEAD_W), lambda i, h, j: (0, h))],
        out_specs=pl.BlockSpec((None, tq, HEAD_W), lambda i, h, j: (i, j, h)),
        out_shape=jax.ShapeDtypeStruct((b, l_lat, DIFF_HEADS * HEAD_W), BF16),
        scratch_shapes=[pltpu.VMEM((HEAD_W + ONES_ROWS, l_lat + l_ctx), BF16),
                        pltpu.VMEM((8, HEAD_W), F32)],
        compiler_params=_cparams("arbitrary", "arbitrary", "arbitrary"),
        name="attn",
    )(p_lat, p_lat, p_lat, p_ctx, p_ctx, lam_par, subln_g)


HALO = 16
UP_HALO = 8


def _ffn_kernel(xp_ref, x_ref, xn_ref, rp_ref, r_ref, rn_ref, dp_ref, d_ref, dn_ref,
                wo_ref, g1_ref, a2_ref, s2_ref, wu_ref, cw_ref, cb_ref, wd_ref, g2_ref, fg_ref,
                o_ref, act_ref, *, tm, d_ff, tn):
    t = pl.program_id(1)
    nt = pl.num_programs(1)
    rows = tm + 2 * HALO
    half = r_ref.shape[-1]
    xs = jnp.concatenate([xp_ref[...], x_ref[...], xn_ref[...]], axis=0)
    yr = jnp.concatenate([rp_ref[...], r_ref[...], rn_ref[...]], axis=0)
    yd = jnp.concatenate([dp_ref[...], d_ref[...], dn_ref[...]], axis=0)
    y = (jnp.dot(yr, wo_ref[:half, :], preferred_element_type=F32)
         + jnp.dot(yd, wo_ref[half:, :], preferred_element_type=F32))
    x1s = xs + g1_ref[...] * y
    h2 = (x1s * lax.rsqrt(jnp.mean(x1s * x1s, axis=-1, keepdims=True) + EPS)) * a2_ref[...] + s2_ref[...]
    row = lax.broadcasted_iota(jnp.int32, (rows, 1), 0)
    inside = ((row >= HALO) | (t > 0)) & ((row < tm + HALO) | (t < nt - 1))
    trim = HALO - UP_HALO
    hh = jnp.where(inside, h2, 0.0)[trim:rows - trim].astype(BF16)
    up_rows = tm + 2 * UP_HALO
    for j in range(d_ff // tn):
        for part in range(2):
            c0 = part * d_ff + j * tn
            u = jnp.dot(hh, wu_ref[:, c0:c0 + tn], preferred_element_type=F32)
            w = cw_ref[:, c0:c0 + tn]
            conv = (pltpu.roll(u, 1, axis=0)[UP_HALO:UP_HALO + tm] * w[0:1, :]
                    + u[UP_HALO:UP_HALO + tm] * w[1:2, :]
                    + pltpu.roll(u, up_rows - 1, axis=0)[UP_HALO:UP_HALO + tm] * w[2:3, :]
                    + cb_ref[:, c0:c0 + tn])
            if part == 0:
                gate = _silu(conv)
            else:
                act_ref[:, j * tn:(j + 1) * tn] = (gate * conv).astype(BF16)
    rb = tm // 2
    for r in range(2):
        f = jnp.dot(act_ref[r * rb:(r + 1) * rb, :], wd_ref[...], preferred_element_type=F32)
        x2 = x1s[HALO + r * rb:HALO + (r + 1) * rb, :] + g2_ref[...] * f
        o_ref[r * rb:(r + 1) * rb, :] = (
            x2 * lax.rsqrt(jnp.mean(x2 * x2, axis=-1, keepdims=True) + EPS) * fg_ref[...])


def _ffn(x, y_ret, y_diff, w_out, g1, a2, s2, w_up, conv_w, conv_b, w_down, g2, final_g, *, tm, tn):
    b, l, d = x.shape
    half = y_ret.shape[-1]
    d_ff = w_down.shape[0]
    nh = tm // HALO
    n_halo_blocks = l // HALO
    kern = functools.partial(_ffn_kernel, tm=tm, d_ff=d_ff, tn=tn)

    def halo_specs(w):
        return [pl.BlockSpec((None, HALO, w), lambda i, t: (i, jnp.maximum(t * nh - 1, 0), 0)),
                pl.BlockSpec((None, tm, w), lambda i, t: (i, t, 0)),
                pl.BlockSpec((None, HALO, w), lambda i, t: (i, jnp.minimum((t + 1) * nh, n_halo_blocks - 1), 0))]

    full = lambda a: pl.BlockSpec(a.shape, lambda i, t: (0,) * a.ndim, pipeline_mode=pl.Buffered(1))
    vec = pl.BlockSpec((None, 1, d), lambda i, t: (i, 0, 0))
    return pl.pallas_call(
        kern,
        grid=(b, l // tm),
        in_specs=[*halo_specs(d), *halo_specs(half), *halo_specs(half),
                  full(w_out), vec, vec, vec,
                  full(w_up), full(conv_w), full(conv_b), full(w_down), vec, full(final_g)],
        out_specs=pl.BlockSpec((None, tm, d), lambda i, t: (i, t, 0)),
        out_shape=jax.ShapeDtypeStruct((b, l, d), F32),
        scratch_shapes=[pltpu.VMEM((tm, d_ff), BF16)],
        compiler_params=_cparams("arbitrary", "arbitrary"),
        name="ffn",
    )(x, x, x, y_ret, y_ret, y_ret, y_diff, y_diff, y_diff, w_out, g1, a2, s2,
      w_up, conv_w, conv_b, w_down, g2, final_g)


def _rope_tables(l):
    pos = np.arange(l)
    row = (pos // GRID_W).astype(np.float64)[:, None]
    col = (pos % GRID_W).astype(np.float64)[:, None]

    def angles(head_dim):
        n_freq = head_dim // 4
        inv = ROPE_BASE ** (-np.arange(n_freq, dtype=np.float64) / n_freq)
        return np.concatenate([row * inv, col * inv], axis=-1)

    ar = angles(RET_DK)
    ad = angles(DIFF_HD)
    cr, sr = np.cos(ar), np.sin(ar)
    cd, sd = np.cos(ad), np.sin(ad)
    zd = np.zeros_like(sd)
    return jnp.asarray(np.concatenate([
        cr, cr, -sr, sr,
        cd, cd, cd, cd,
        -sd, zd, -sd, zd,
        zd, sd, zd, sd], axis=-1), F32)


def _identity_tables(l):
    one = np.ones((l, HEAD_W), np.float32)
    zero = np.zeros((l, HEAD_W), np.float32)
    return jnp.asarray(np.concatenate([one, zero, one, zero, zero], axis=-1))


def kernel(x, c, ctx, c_ctx, w_mod, b_mod, norm1_g, w_in, ret_decay_logit, ret_gn_g, diff_lambda,
           diff_subln_g, w_out, norm2_g, w_up, conv_w, conv_b, w_down, final_g):
    b, l, d = x.shape
    l_ctx = ctx.shape[1]

    rows = -(-(b + 1) // 8) * 8
    cc = jnp.zeros((rows, d), F32).at[:b].set(c).at[b].set(c_ctx)
    m = _modulation(cc, w_mod[0], b_mod[0][None, :])
    sh1, sc1, g1, sh2, sc2, g2 = [m[:, i * d:(i + 1) * d] for i in range(N_MOD)]
    a1 = norm1_g[0][None, :] * (1.0 + sc1)
    a2 = norm2_g[0][None, :] * (1.0 + sc2)
    vec = lambda v: v[:b, None, :]
    ctx_vec = lambda v: jnp.broadcast_to(v[b][None, None, :], (b, 1, d))

    w_in_b = w_in[0].astype(BF16)
    p_lat = _inproj(x, vec(a1), vec(sh1), w_in_b, _rope_tables(l), tm=min(1024, l),
                    groups=LAT_GROUPS, name="inproj_lat")
    p_ctx = _inproj(ctx, ctx_vec(a1), ctx_vec(sh1), w_in_b, _identity_tables(l_ctx), tm=l_ctx,
                    groups=CTX_GROUPS, name="inproj_ctx")

    lg = jax.nn.log_sigmoid(ret_decay_logit[0].astype(F32))
    lgv = jnp.broadcast_to(lg.T[:, :, None], (RET_HEADS, 2, HEAD_W))
    gn_cols = ret_gn_g[0].reshape(2, RET_HEADS, HEAD_W).transpose(1, 0, 2)[..., None]
    y_ret = _retention(p_lat, p_ctx, lgv, gn_cols)
    y_diff = _diff_attention(p_lat, p_ctx, diff_lambda[0], diff_subln_g[0][None, :],
                             tq=min(2048, l), tk=512)

    return _ffn(x, y_ret, y_diff, w_out[0].astype(BF16), vec(g1), vec(a2), vec(sh2),
                w_up[0].astype(BF16), conv_w[0], conv_b[0][None, :], w_down[0].astype(BF16),
                vec(g2), final_g[None, :], tm=min(512, l), tn=256)
```

```python
import functools
import math

import jax
import jax.numpy as jnp
import numpy as np
from jax import lax
from jax.experimental import pallas as pl
from jax.experimental.pallas import tpu as pltpu

F32 = jnp.float32
BF16 = jnp.bfloat16

GRID_W = 64
RET_HEADS = 4
RET_DK = 128
DIFF_HEADS = 4
DIFF_HD = 64
HEAD_W = 128
GROUP_W = 512
LAT_GROUPS = (0, 1, 2, 3, 4, 5, 6)
CTX_GROUPS = LAT_GROUPS
RET_CHUNK = 256
ROPE_BASE = 10000.0
EPS = 1e-6
GN_EPS = 1e-5
N_MOD = 6
LAMBDA_INIT = 0.8 - 0.6 * math.exp(-0.3 * 0)
LOG2E = math.log2(math.e)
RET_LOOKAHEAD = 2
ATTN_LOOKAHEAD = 1

VMEM_LIMIT = 56 * 1024 * 1024


def _cparams(*sem):
    return pltpu.CompilerParams(dimension_semantics=sem, vmem_limit_bytes=VMEM_LIMIT)


def _silu(v):
    return v * (1.0 / (1.0 + jnp.exp(-v)))


def _split_dot(a, w):
    a_hi = a.astype(BF16)
    a_lo = (a - a_hi.astype(F32)).astype(BF16)
    w_hi = w.astype(BF16)
    w_lo = (w - w_hi.astype(F32)).astype(BF16)
    dot = functools.partial(jnp.dot, preferred_element_type=F32)
    return dot(a_hi, w_hi) + (dot(a_hi, w_lo) + dot(a_lo, w_hi))


def _mod_kernel(c_ref, w_ref, b_ref, o_ref):
    o_ref[...] = _split_dot(_silu(c_ref[...]), w_ref[...]) + b_ref[...]


def _modulation(cc, w_mod, b_mod):
    rows, d = cc.shape
    n = w_mod.shape[1]
    tn = 1024
    return pl.pallas_call(
        _mod_kernel,
        grid=(n // tn,),
        in_specs=[pl.BlockSpec((rows, d), lambda j: (0, 0)),
                  pl.BlockSpec((d, tn), lambda j: (0, j)),
                  pl.BlockSpec((1, tn), lambda j: (0, j))],
        out_specs=pl.BlockSpec((rows, tn), lambda j: (0, j)),
        out_shape=jax.ShapeDtypeStruct((rows, n), F32),
        compiler_params=_cparams("arbitrary"),
        name="mod",
    )(cc, w_mod, b_mod)


def _rope128(v, cos, sin):
    return v * cos + pltpu.roll(v, 64, axis=1) * sin


def _rope64(v, cos, sin_lo, sin_hi):
    return v * cos + pltpu.roll(v, 96, axis=1) * sin_lo + pltpu.roll(v, 32, axis=1) * sin_hi


def _inproj_kernel(x_ref, a_ref, s_ref, w_ref, t_ref, o_ref, *, groups):
    x = x_ref[...]
    xn = x * lax.rsqrt(jnp.mean(x * x, axis=-1, keepdims=True) + EPS)
    h = (xn * a_ref[...] + s_ref[...]).astype(BF16)
    cos_r = t_ref[:, 0 * HEAD_W:1 * HEAD_W]
    sin_r = t_ref[:, 1 * HEAD_W:2 * HEAD_W]
    cos_d = t_ref[:, 2 * HEAD_W:3 * HEAD_W]
    sin_dl = t_ref[:, 3 * HEAD_W:4 * HEAD_W]
    sin_dh = t_ref[:, 4 * HEAD_W:5 * HEAD_W]
    for slot, g in enumerate(groups):
        acc = jnp.dot(h, w_ref[:, g * GROUP_W:(g + 1) * GROUP_W], preferred_element_type=F32)
        for hd in range(GROUP_W // HEAD_W):
            v = acc[:, hd * HEAD_W:(hd + 1) * HEAD_W]
            if g == 0:
                v = _rope128(v, cos_r, sin_r)
            elif g == 1:
                v = _rope128(v, cos_r, sin_r) * (RET_DK ** -0.5)
            elif g == 4:
                v = _rope64(v, cos_d, sin_dl, sin_dh) * (DIFF_HD ** -0.5 * LOG2E)
            elif g == 5:
                v = _rope64(v, cos_d, sin_dl, sin_dh)
            c0 = slot * GROUP_W + hd * HEAD_W
            o_ref[:, c0:c0 + HEAD_W] = v.astype(BF16)


def _inproj(x, a, s, w_in, tables, *, tm, groups, name):
    b, lx, d = x.shape
    n_in = w_in.shape[1]
    n_cols = len(groups) * GROUP_W
    return pl.pallas_call(
        functools.partial(_inproj_kernel, groups=groups),
        grid=(lx // tm, b),
        in_specs=[pl.BlockSpec((None, tm, d), lambda t, i: (i, t, 0)),
                  pl.BlockSpec((None, 1, d), lambda t, i: (i, 0, 0)),
                  pl.BlockSpec((None, 1, d), lambda t, i: (i, 0, 0)),
                  pl.BlockSpec((d, n_in), lambda t, i: (0, 0), pipeline_mode=pl.Buffered(1)),
                  pl.BlockSpec((tm, 5 * HEAD_W), lambda t, i: (t, 0))],
        out_specs=pl.BlockSpec((None, tm, n_cols), lambda t, i: (i, t, 0)),
        out_shape=jax.ShapeDtypeStruct((b, lx, n_cols), BF16),
        compiler_params=_cparams("arbitrary", "arbitrary"),
        name=name,
    )(x, a, s, w_in, tables)


def _ret_kernel(q_ref, k_ref, v_ref, g_ref, kc_ref, vc_ref, lg_ref, gn_ref, o_ref, u_ref, s_ref,
                *, n_lat, n_all):
    c = RET_CHUNK
    dk = HEAD_W
    row = lax.broadcasted_iota(jnp.int32, (c, c), 0).astype(F32)
    col = lax.broadcasted_iota(jnp.int32, (c, c), 1).astype(F32)
    tok = lax.broadcasted_iota(jnp.int32, (c, dk), 0).astype(F32)
    lg_f = lg_ref[0:1, :]
    lg_b = lg_ref[1:2, :]
    lg_f2 = jnp.concatenate([lg_f, lg_f], axis=1)
    lg_b2 = jnp.concatenate([lg_b, lg_b], axis=1)
    mask_f = jnp.where(col >= row, jnp.exp(jnp.maximum(col - row, 0.0) * lg_f2), 0.0).astype(BF16)
    mask_b = jnp.where(row >= col, jnp.exp(jnp.maximum(row - col, 0.0) * lg_b2), 0.0).astype(BF16)
    qdec_f = jnp.exp((tok + 1.0) * lg_f)
    qdec_b = jnp.exp((c - tok) * lg_b)
    kdec_f = jnp.exp((c - 1.0 - tok) * lg_f)
    kdec_b = jnp.exp(tok * lg_b)
    cdec_f = jnp.exp(c * lg_f)
    cdec_b = jnp.exp(c * lg_b)
    tn_dims = (((0,), (0,)), ((), ()))
    nt_dims = (((1,), (1,)), ((), ()))

    def rows(idx):
        return slice(idx * c, (idx + 1) * c)

    def kv_chunk(idx):
        if idx < n_lat:
            return k_ref[rows(idx), :], v_ref[rows(idx), :]
        return kc_ref[rows(idx - n_lat), :], vc_ref[rows(idx - n_lat), :]

    for idx in range(n_all):
        k, v = kv_chunk(idx)
        kf = k.astype(F32)
        kd = jnp.concatenate([(kf * kdec_f).astype(BF16), (kf * kdec_b).astype(BF16)], axis=1)
        u_ref[idx] = lax.dot_general(v, kd, tn_dims, preferred_element_type=F32)

    state = jnp.zeros((dk, dk), F32)
    for idx in [*range(n_lat, n_all), *range(n_lat)]:
        s_ref[idx, :, 0:dk] = state.astype(BF16)
        state = state * cdec_f + u_ref[idx, :, 0:dk]
    state = jnp.zeros((dk, dk), F32)
    for idx in reversed(range(n_all)):
        s_ref[idx, :, dk:2 * dk] = state.astype(BF16)
        state = state * cdec_b + u_ref[idx, :, dk:2 * dk]

    gn_f = jnp.broadcast_to(gn_ref[0], (dk, c))
    gn_b = jnp.broadcast_to(gn_ref[1], (dk, c))

    def group_norm_t(o, g):
        mu = jnp.mean(o, axis=0, keepdims=True)
        d = o - mu
        var = jnp.mean(d * d, axis=0, keepdims=True)
        return d * lax.rsqrt(var + GN_EPS) * g

    def scores_t(idx):
        return lax.dot_general(k_ref[rows(idx), :], q_ref[rows(idx), :], nt_dims,
                               preferred_element_type=F32)

    ahead = [scores_t(i) for i in range(min(RET_LOOKAHEAD, n_lat))]
    for idx in range(n_lat):
        a_t = ahead.pop(0)
        if idx + RET_LOOKAHEAD < n_lat:
            ahead.append(scores_t(idx + RET_LOOKAHEAD))
        v = v_ref[rows(idx), :]
        qf = q_ref[rows(idx), :].astype(F32)
        a_b = a_t.astype(BF16)
        o_f = (lax.dot_general(v, a_b * mask_f, tn_dims, preferred_element_type=F32)
               + lax.dot_general(s_ref[idx, :, 0:dk], (qf * qdec_f).astype(BF16), nt_dims,
                                 preferred_element_type=F32))
        o_b = (lax.dot_general(v, a_b * mask_b, tn_dims, preferred_element_type=F32)
               + lax.dot_general(s_ref[idx, :, dk:2 * dk], (qf * qdec_b).astype(BF16), nt_dims,
                                 preferred_element_type=F32))
        n_t = group_norm_t(o_f, gn_f) + group_norm_t(o_b, gn_b)
        y = _silu(g_ref[rows(idx), :].astype(F32)) * n_t.T
        o_ref[rows(idx), :] = y.astype(BF16)


def _retention(p_lat, p_ctx, lgv, gn_cols):
    b, l_lat, _ = p_lat.shape
    l_ctx = p_ctx.shape[1]
    n_lat = l_lat // RET_CHUNK
    n_all = n_lat + l_ctx // RET_CHUNK
    col = lambda rows, g: pl.BlockSpec((None, rows, HEAD_W), lambda i, h: (i, 0, g * RET_HEADS + h))
    kern = functools.partial(_ret_kernel, n_lat=n_lat, n_all=n_all)
    return pl.pallas_call(
        kern,
        grid=(b, RET_HEADS),
        in_specs=[col(l_lat, 0), col(l_lat, 1), col(l_lat, 2), col(l_lat, 3),
                  col(l_ctx, CTX_GROUPS.index(1)), col(l_ctx, CTX_GROUPS.index(2)),
                  pl.BlockSpec((None, 2, HEAD_W), lambda i, h: (h, 0, 0)),
                  pl.BlockSpec((None, 2, HEAD_W, 1), lambda i, h: (h, 0, 0, 0))],
        out_specs=pl.BlockSpec((None, l_lat, HEAD_W), lambda i, h: (i, 0, h)),
        out_shape=jax.ShapeDtypeStruct((b, l_lat, RET_HEADS * HEAD_W), BF16),
        scratch_shapes=[pltpu.VMEM((n_all, HEAD_W, 2 * HEAD_W), F32),
                        pltpu.VMEM((n_all, HEAD_W, 2 * HEAD_W), BF16)],
        compiler_params=_cparams("arbitrary", "arbitrary"),
        name="ret",
    )(p_lat, p_lat, p_lat, p_lat, p_ctx, p_ctx, lgv, gn_cols)


PLAIN_EXP_MAX_LOG2 = 90.0
PLAIN_EXP_MAX_VALUE = 2.0 ** 20
ONES_ROWS = 16


def _attn_kernel(q_ref, k_ref, v_ref, kc_ref, vc_ref, lam_ref, g_ref, o_ref, vt_ref, kv_stat_ref,
                 *, tq, tk, l_lat, l_ctx):
    dv = HEAD_W
    l_all = l_lat + l_ctx
    spans = ([(False, o, min(tk, l_lat - o)) for o in range(0, l_lat, tk)]
             + [(True, o, min(tk, l_ctx - o)) for o in range(0, l_ctx, tk)])
    n_k = len(spans)

    chunks = ([(k_ref, v_ref, o, o) for o in range(0, l_lat, RET_CHUNK)]
              + [(kc_ref, vc_ref, o, l_lat + o) for o in range(0, l_ctx, RET_CHUNK)])

    @pl.when(pl.program_id(2) == 0)
    def _():
        kmax = vmax = jnp.zeros((1, HEAD_W), F32)
        for kr, vr, o, dst in chunks:
            vf = vr[o:o + RET_CHUNK, :].astype(F32)
            vt_ref[0:dv, dst:dst + RET_CHUNK] = vf.T.astype(BF16)
            vmax = jnp.maximum(vmax, jnp.max(jnp.abs(vf), axis=0, keepdims=True))
            kf = kr[o:o + RET_CHUNK, :].astype(F32)
            kmax = jnp.maximum(kmax, jnp.max(jnp.abs(kf), axis=0, keepdims=True))
        vt_ref[dv:dv + ONES_ROWS, :] = jnp.ones((ONES_ROWS, l_all), BF16)
        kv_stat_ref[0:1, :] = kmax
        kv_stat_ref[1:2, :] = vmax

    lane = lax.broadcasted_iota(jnp.int32, (tq, HEAD_W), 1)
    q = q_ref[...]
    zero = jnp.zeros_like(q)
    q1 = jnp.where(lane < DIFF_HD, q, zero)
    q2 = jnp.where(lane >= DIFF_HD, q, zero)
    nt_dims = (((1,), (1,)), ((), ()))

    r = lax.broadcasted_iota(jnp.int32, (HEAD_W, HEAD_W), 0)
    c = lax.broadcasted_iota(jnp.int32, (HEAD_W, HEAD_W), 1)
    same_map = jnp.where((r < DIFF_HD) == (c < DIFF_HD), 1.0, 0.0).astype(BF16)
    q_abs = (jnp.abs(q.astype(F32)) * kv_stat_ref[0:1, :]).astype(BF16)
    bound = jnp.dot(q_abs, same_map, preferred_element_type=F32)
    plain = jnp.logical_and(jnp.max(bound) <= PLAIN_EXP_MAX_LOG2,
                            jnp.max(kv_stat_ref[1:2, :]) <= PLAIN_EXP_MAX_VALUE)

    def scores(i):
        is_ctx, o, size = spans[i]
        kt = (kc_ref if is_ctx else k_ref)[o:o + size, :]
        return (lax.dot_general(kt, q1, nt_dims, preferred_element_type=F32),
                lax.dot_general(kt, q2, nt_dims, preferred_element_type=F32))

    def update_online(s, vt, state):
        m, acc = state
        m_new = jnp.maximum(m, jnp.max(s, axis=0, keepdims=True))
        alpha = jnp.exp2(m - m_new)
        pr = jnp.exp2(s - m_new).astype(BF16)
        return m_new, alpha * acc + jnp.dot(vt, pr, preferred_element_type=F32)

    def update_plain(s, vt, state):
        l, acc = state
        pr = jnp.exp2(s)
        return (l + jnp.sum(pr, axis=0, keepdims=True),
                acc + jnp.dot(vt[0:dv], pr.astype(BF16), preferred_element_type=F32))

    def attend(update, init):
        st1 = st2 = init
        ahead = [scores(i) for i in range(min(ATTN_LOOKAHEAD, n_k))]
        for i in range(n_k):
            s1, s2 = ahead.pop(0)
            if i + ATTN_LOOKAHEAD < n_k:
                ahead.append(scores(i + ATTN_LOOKAHEAD))
            is_ctx, o, size = spans[i]
            start = (l_lat if is_ctx else 0) + o
            vt = vt_ref[:, start:start + size]
            st1 = update(s1, vt, st1)
            st2 = update(s2, vt, st2)
        return st1, st2

    def finish(o1, l1, o2, l2):
        lp = lam_ref[...]
        lam = (jnp.exp(jnp.sum(lp[0:1, :] * lp[1:2, :], axis=-1, keepdims=True))
               - jnp.exp(jnp.sum(lp[2:3, :] * lp[3:4, :], axis=-1, keepdims=True)) + LAMBDA_INIT)
        o = o1 / l1 - lam * (o2 / l2)
        n = o * lax.rsqrt(jnp.mean(o * o, axis=0, keepdims=True) + EPS)
        o_ref[...] = (n.T * g_ref[...] * (1.0 - LAMBDA_INIT)).astype(BF16)

    @pl.when(plain)
    def _():
        (l1, acc1), (l2, acc2) = attend(
            update_plain, (jnp.zeros((1, tq), F32), jnp.zeros((dv, tq), F32)))
        finish(acc1, l1, acc2, l2)

    @pl.when(jnp.logical_not(plain))
    def _():
        (_, acc1), (_, acc2) = attend(
            update_online, (jnp.full((1, tq), -jnp.inf, F32), jnp.zeros((dv + ONES_ROWS, tq), F32)))
        finish(acc1[0:dv], acc1[dv:dv + 1], acc2[0:dv], acc2[dv:dv + 1])


def _diff_attention(p_lat, p_ctx, lam_par, subln_g, *, tq, tk):
    b, l_lat, _ = p_lat.shape
    l_ctx = p_ctx.shape[1]
    kern = functools.partial(_attn_kernel, tq=tq, tk=tk, l_lat=l_lat, l_ctx=l_ctx)
    col = lambda rows, g: pl.BlockSpec((None, rows, HEAD_W), lambda i, h, j: (i, 0, g * DIFF_HEADS + h))
    return pl.pallas_call(
        kern,
        grid=(b, DIFF_HEADS, l_lat // tq),
        in_specs=[pl.BlockSpec((None, tq, HEAD_W), lambda i, h, j: (i, j, 4 * DIFF_HEADS + h)),
                  col(l_lat, 5), col(l_lat, 6),
                  col(l_ctx, CTX_GROUPS.index(5)), col(l_ctx, CTX_GROUPS.index(6)),
                  pl.BlockSpec((4, DIFF_HD), lambda i, h, j: (0, 0)),
                  pl.BlockSpec((1, HEAD_W), lambda i, h, j: (0, h))],
        out_specs=pl.BlockSpec((None, tq, HEAD_W), lambda i, h, j: (i, j, h)),
        out_shape=jax.ShapeDtypeStruct((b, l_lat, DIFF_HEADS * HEAD_W), BF16),
        scratch_shapes=[pltpu.VMEM((HEAD_W + ONES_ROWS, l_lat + l_ctx), BF16),
                        pltpu.VMEM((8, HEAD_W), F32)],
        compiler_params=_cparams("arbitrary", "arbitrary", "arbitrary"),
        name="attn",
    )(p_lat, p_lat, p_lat, p_ctx, p_ctx, lam_par, subln_g)


HALO = 16
UP_HALO = 8


def _ffn_kernel(xp_ref, x_ref, xn_ref, rp_ref, r_ref, rn_ref, dp_ref, d_ref, dn_ref,
                wo_ref, g1_ref, a2_ref, s2_ref, wu_ref, cw_ref, cb_ref, wd_ref, g2_ref, fg_ref,
                o_ref, act_ref, *, tm, d_ff, tn):
    t = pl.program_id(1)
    nt = pl.num_programs(1)
    rows = tm + 2 * HALO
    half = r_ref.shape[-1]
    xs = jnp.concatenate([xp_ref[...], x_ref[...], xn_ref[...]], axis=0)
    yr = jnp.concatenate([rp_ref[...], r_ref[...], rn_ref[...]], axis=0)
    yd = jnp.concatenate([dp_ref[...], d_ref[...], dn_ref[...]], axis=0)
    y = (jnp.dot(yr, wo_ref[:half, :], preferred_element_type=F32)
         + jnp.dot(yd, wo_ref[half:, :], preferred_element_type=F32))
    x1s = xs + g1_ref[...] * y
    h2 = (x1s * lax.rsqrt(jnp.mean(x1s * x1s, axis=-1, keepdims=True) + EPS)) * a2_ref[...] + s2_ref[...]
    row = lax.broadcasted_iota(jnp.int32, (rows, 1), 0)
    inside = ((row >= HALO) | (t > 0)) & ((row < tm + HALO) | (t < nt - 1))
    trim = HALO - UP_HALO
    hh = jnp.where(inside, h2, 0.0)[trim:rows - trim].astype(BF16)
    up_rows = tm + 2 * UP_HALO
    for j in range(d_ff // tn):
        for part in range(2):
            c0 = part * d_ff + j * tn
            u = jnp.dot(hh, wu_ref[:, c0:c0 + tn], preferred_element_type=F32)
            w = cw_ref[:, c0:c0 + tn]
            conv = (pltpu.roll(u, 1, axis=0)[UP_HALO:UP_HALO + tm] * w[0:1, :]
                    + u[UP_HALO:UP_HALO + tm] * w[1:2, :]
                    + pltpu.roll(u, up_rows - 1, axis=0)[UP_HALO:UP_HALO + tm] * w[2:3, :]
                    + cb_ref[:, c0:c0 + tn])
            if part == 0:
                gate = _silu(conv)
            else:
                act_ref[:, j * tn:(j + 1) * tn] = (gate * conv).astype(BF16)
    rb = tm // 2
    for r in range(2):
        f = jnp.dot(act_ref[r * rb:(r + 1) * rb, :], wd_ref[...], preferred_element_type=F32)
        x2 = x1s[HALO + r * rb:HALO + (r + 1) * rb, :] + g2_ref[...] * f
        o_ref[r * rb:(r + 1) * rb, :] = (
            x2 * lax.rsqrt(jnp.mean(x2 * x2, axis=-1, keepdims=True) + EPS) * fg_ref[...])


def _ffn(x, y_ret, y_diff, w_out, g1, a2, s2, w_up, conv_w, conv_b, w_down, g2, final_g, *, tm, tn):
    b, l, d = x.shape
    half = y_ret.shape[-1]
    d_ff = w_down.shape[0]
    nh = tm // HALO
    n_halo_blocks = l // HALO
    kern = functools.partial(_ffn_kernel, tm=tm, d_ff=d_ff, tn=tn)

    def halo_specs(w):
        return [pl.BlockSpec((None, HALO, w), lambda i, t: (i, jnp.maximum(t * nh - 1, 0), 0)),
                pl.BlockSpec((None, tm, w), lambda i, t: (i, t, 0)),
                pl.BlockSpec((None, HALO, w), lambda i, t: (i, jnp.minimum((t + 1) * nh, n_halo_blocks - 1), 0))]

    full = lambda a: pl.BlockSpec(a.shape, lambda i, t: (0,) * a.ndim, pipeline_mode=pl.Buffered(1))
    vec = pl.BlockSpec((None, 1, d), lambda i, t: (i, 0, 0))
    return pl.pallas_call(
        kern,
        grid=(b, l // tm),
        in_specs=[*halo_specs(d), *halo_specs(half), *halo_specs(half),
                  full(w_out), vec, vec, vec,
                  full(w_up), full(conv_w), full(conv_b), full(w_down), vec, full(final_g)],
        out_specs=pl.BlockSpec((None, tm, d), lambda i, t: (i, t, 0)),
        out_shape=jax.ShapeDtypeStruct((b, l, d), F32),
        scratch_shapes=[pltpu.VMEM((tm, d_ff), BF16)],
        compiler_params=_cparams("arbitrary", "arbitrary"),
        name="ffn",
    )(x, x, x, y_ret, y_ret, y_ret, y_diff, y_diff, y_diff, w_out, g1, a2, s2,
      w_up, conv_w, conv_b, w_down, g2, final_g)


def _rope_tables(l):
    pos = np.arange(l)
    row = (pos // GRID_W).astype(np.float64)[:, None]
    col = (pos % GRID_W).astype(np.float64)[:, None]

    def angles(head_dim):
        n_freq = head_dim // 4
        inv = ROPE_BASE ** (-np.arange(n_freq, dtype=np.float64) / n_freq)
        return np.concatenate([row * inv, col * inv], axis=-1)

    ar = angles(RET_DK)
    ad = angles(DIFF_HD)
    cr, sr = np.cos(ar), np.sin(ar)
    cd, sd = np.cos(ad), np.sin(ad)
    zd = np.zeros_like(sd)
    return jnp.asarray(np.concatenate([
        cr, cr, -sr, sr,
        cd, cd, cd, cd,
        -sd, zd, -sd, zd,
        zd, sd, zd, sd], axis=-1), F32)


def _identity_tables(l):
    one = np.ones((l, HEAD_W), np.float32)
    zero = np.zeros((l, HEAD_W), np.float32)
    return jnp.asarray(np.concatenate([one, zero, one, zero, zero], axis=-1))


def kernel(x, c, ctx, c_ctx, w_mod, b_mod, norm1_g, w_in, ret_decay_logit, ret_gn_g, diff_lambda,
           diff_subln_g, w_out, norm2_g, w_up, conv_w, conv_b, w_down, final_g):
    b, l, d = x.shape
    l_ctx = ctx.shape[1]

    rows = -(-(b + 1) // 8) * 8
    cc = jnp.zeros((rows, d), F32).at[:b].set(c).at[b].set(c_ctx)
    m = _modulation(cc, w_mod[0], b_mod[0][None, :])
    sh1, sc1, g1, sh2, sc2, g2 = [m[:, i * d:(i + 1) * d] for i in range(N_MOD)]
    a1 = norm1_g[0][None, :] * (1.0 + sc1)
    a2 = norm2_g[0][None, :] * (1.0 + sc2)
    vec = lambda v: v[:b, None, :]
    ctx_vec = lambda v: jnp.broadcast_to(v[b][None, None, :], (b, 1, d))

    w_in_b = w_in[0].astype(BF16)
    p_lat = _inproj(x, vec(a1), vec(sh1), w_in_b, _rope_tables(l), tm=min(1024, l),
                    groups=LAT_GROUPS, name="inproj_lat")
    p_ctx = _inproj(ctx, ctx_vec(a1), ctx_vec(sh1), w_in_b, _identity_tables(l_ctx), tm=l_ctx,
                    groups=CTX_GROUPS, name="inproj_ctx")

    lg = jax.nn.log_sigmoid(ret_decay_logit[0].astype(F32))
    lgv = jnp.broadcast_to(lg.T[:, :, None], (RET_HEADS, 2, HEAD_W))
    gn_cols = ret_gn_g[0].reshape(2, RET_HEADS, HEAD_W).transpose(1, 0, 2)[..., None]
    y_ret = _retention(p_lat, p_ctx, lgv, gn_cols)
    y_diff = _diff_attention(p_lat, p_ctx, diff_lambda[0], diff_subln_g[0][None, :],
                             tq=min(2048, l), tk=512)

    return _ffn(x, y_ret, y_diff, w_out[0].astype(BF16), vec(g1), vec(a2), vec(sh2),
                w_up[0].astype(BF16), conv_w[0], conv_b[0][None, :], w_down[0].astype(BF16),
                vec(g2), final_g[None, :], tm=min(512, l), tn=256)
```
